```python
import jax, jax.numpy as jnp
from jax import lax
import numpy as np

D_MODEL = 2048
BATCH = 8
SEQ = 2048
DEPTH = 1

CHUNK = 64
N_META = 16
HEAD_DIM = 128
D_ATTN = D_MODEL // 2
ATTN_HEADS = D_ATTN // HEAD_DIM
POOL_WINDOWS = (2, 4, 8, 16)
N_POOL_GROUPS = len(POOL_WINDOWS)
D_POOL = D_MODEL // 2
POOL_GROUP_DIM = D_POOL // N_POOL_GROUPS
D_FF = 11 * D_MODEL // 4
CONV_WIDTH = 3
Q_BLOCK = 128
EPS = 1e-6

SPLIT_SIZES = (D_ATTN, D_ATTN, D_ATTN, ATTN_HEADS, D_POOL, D_MODEL, D_MODEL)
SPLIT_IDX = tuple(int(i) for i in np.cumsum(SPLIT_SIZES)[:-1])
D_IN = sum(SPLIT_SIZES)
F_OFF = 3 * D_ATTN

kernel_name = 'hybrid_fox_pool_convffn'


def rmsnorm(x, g):
    xf = x.astype(jnp.float32)
    y = xf * lax.rsqrt(jnp.mean(xf * xf, axis=-1, keepdims=True) + EPS)
    return (y * g.astype(jnp.float32)).astype(x.dtype)


def forgetting_attention(q, k, v, logf):
    B, L, H, Dh = q.shape
    n_blocks = -(-L // Q_BLOCK)
    Lp = n_blocks * Q_BLOCK
    pad = ((0, 0), (0, Lp - L), (0, 0), (0, 0))
    q = jnp.pad(q, pad)
    k = jnp.pad(k, pad)
    v = jnp.pad(v, pad)
    logf = jnp.pad(logf, ((0, 0), (0, Lp - L), (0, 0)))
    c = jnp.cumsum(logf, axis=1)
    c = jnp.transpose(c, (0, 2, 1))
    scale = HEAD_DIM ** -0.5
    pos = jnp.arange(Lp)
    outs = []
    for i in range(n_blocks):
        q0, q1 = i * Q_BLOCK, (i + 1) * Q_BLOCK
        qb = q[:, q0:q1]
        kb = k[:, :q1]
        vb = v[:, :q1]
        s = jnp.einsum('bqhd,bkhd->bhqk', qb, kb).astype(jnp.float32) * scale
        bias = c[:, :, q0:q1, None] - c[:, :, None, :q1]
        mask = pos[q0:q1, None] >= pos[None, :q1]
        s = jnp.where(mask[None, None], s + bias, -jnp.inf)
        p = jax.nn.softmax(s, axis=-1).astype(vb.dtype)
        outs.append(jnp.einsum('bhqk,bkhd->bqhd', p, vb))
    o = jnp.concatenate(outs, axis=1)[:, :L]
    return o.reshape(B, L, H * Dh)


def multiscale_pool(u, pool_w, pool_scale):
    B, L, _ = u.shape
    uf = u.astype(jnp.float32).reshape(B, L, N_POOL_GROUPS, POOL_GROUP_DIM)
    csum = jnp.cumsum(uf, axis=1)
    pos = jnp.arange(L)
    outs = []
    for g, w in enumerate(POOL_WINDOWS):
        cg = csum[:, :, g]
        prev = jnp.pad(cg[:, :L - w], ((0, 0), (w, 0), (0, 0)))
        cnt = jnp.minimum(pos + 1, w).astype(jnp.float32)[None, :, None]
        outs.append((cg - prev) / cnt - uf[:, :, g])
    d = jnp.stack(outs, axis=2).astype(u.dtype)
    y = jnp.einsum('blgc,gcd->blgd', d, pool_w).reshape(B, L, D_POOL)
    return y * pool_scale


def mixer_block(h, w_in, b_in, w_attn_o, pool_w, pool_scale, w_pool_o, w_out):
    B, L, _ = h.shape
    z = h @ w_in + b_in
    q, k, v, f_logit, u, g_attn, g_pool = jnp.split(z, SPLIT_IDX, axis=-1)
    q = q.reshape(B, L, ATTN_HEADS, HEAD_DIM)
    k = k.reshape(B, L, ATTN_HEADS, HEAD_DIM)
    v = v.reshape(B, L, ATTN_HEADS, HEAD_DIM)
    logf = jax.nn.log_sigmoid(f_logit.astype(jnp.float32))
    attn = forgetting_attention(q, k, v, logf)
    pool = multiscale_pool(u, pool_w, pool_scale)
    m = jax.nn.sigmoid(g_attn) * (attn @ w_attn_o) + jax.nn.sigmoid(g_pool) * (pool @ w_pool_o)
    return m @ w_out


def causal_dwconv(a, conv_w, conv_b):
    L = a.shape[1]
    ap = jnp.pad(a, ((0, 0), (CONV_WIDTH - 1, 0), (0, 0)))
    out = conv_b
    for j in range(CONV_WIDTH):
        out = out + ap[:, j:j + L] * conv_w[j]
    return out


def conv_ffn(h, w_up, conv_w, conv_b, w_down):
    a = causal_dwconv(h @ w_up, conv_w, conv_b)
    gate, val = jnp.split(a, 2, axis=-1)
    return (jax.nn.gelu(gate, approximate=True) * val) @ w_down


def setup_inputs(seed: int = 0) -> dict:
    key = jax.random.key(seed)
    ks = jax.random.split(key, 18)
    f32 = jnp.float32
    nrm = lambda k, shape, s: jax.random.normal(k, shape, f32) * s
    gain = lambda k, n: jnp.ones((DEPTH, n), f32) + nrm(k, (DEPTH, n), 0.02)
    b_in = nrm(ks[4], (DEPTH, D_IN), 0.02)
    f_bias = jax.random.uniform(ks[5], (DEPTH, ATTN_HEADS), f32, 1.0, 5.0)
    b_in = b_in.at[:, F_OFF:F_OFF + ATTN_HEADS].add(f_bias)
    return {
        'x': nrm(ks[0], (BATCH, SEQ, D_MODEL), 1.0),
        'meta_tokens': nrm(ks[1], (N_META, D_MODEL), 1.0),
        'mix_pre_g': gain(ks[2], D_MODEL),
        'w_in': nrm(ks[3], (DEPTH, D_MODEL, D_IN), D_MODEL ** -0.5),
        'b_in': b_in,
        'w_attn_o': nrm(ks[6], (DEPTH, D_ATTN, D_MODEL), D_ATTN ** -0.5),
        'pool_w': nrm(ks[7], (DEPTH, N_POOL_GROUPS, POOL_GROUP_DIM, POOL_GROUP_DIM), POOL_GROUP_DIM ** -0.5),
        'pool_scale': jnp.ones((DEPTH, D_POOL), f32) + nrm(ks[8], (DEPTH, D_POOL), 0.1),
        'w_pool_o': nrm(ks[9], (DEPTH, D_POOL, D_MODEL), D_POOL ** -0.5),
        'w_out': nrm(ks[10], (DEPTH, D_MODEL, D_MODEL), D_MODEL ** -0.5),
        'mix_post_g': gain(ks[11], D_MODEL),
        'ffn_pre_g': gain(ks[12], D_MODEL),
        'w_ffn_up': nrm(ks[13], (DEPTH, D_MODEL, 2 * D_FF), D_MODEL ** -0.5),
        'ffn_conv_w': nrm(ks[14], (DEPTH, CONV_WIDTH, 2 * D_FF), CONV_WIDTH ** -0.5),
        'ffn_conv_b': nrm(ks[15], (DEPTH, 2 * D_FF), 0.02),
        'w_ffn_down': nrm(ks[16], (DEPTH, D_FF, D_MODEL), D_FF ** -0.5),
        'ffn_post_g': gain(ks[17], D_MODEL),
    }


def reference(x, meta_tokens, mix_pre_g, w_in, b_in, w_attn_o, pool_w, pool_scale, w_pool_o,
              w_out, mix_post_g, ffn_pre_g, w_ffn_up, ffn_conv_w, ffn_conv_b, w_ffn_down,
              ffn_post_g):
    B = x.shape[0]
    meta = jnp.broadcast_to(meta_tokens.astype(x.dtype)[None], (B, N_META, D_MODEL))
    r = jnp.concatenate([meta, x], axis=1)
    for l in range(DEPTH):
        h = rmsnorm(r, mix_pre_g[l])
        mix = mixer_block(h, w_in[l], b_in[l], w_attn_o[l], pool_w[l], pool_scale[l],
                          w_pool_o[l], w_out[l])
        r = r + rmsnorm(mix, mix_post_g[l])
        h = rmsnorm(r, ffn_pre_g[l])
        ff = conv_ffn(h, w_ffn_up[l], ffn_conv_w[l], ffn_conv_b[l], w_ffn_down[l])
        r = r + rmsnorm(ff, ffn_post_g[l])
    return r[:, N_META:]
```

```python
import functools
import math

import jax
import jax.numpy as jnp
from jax import lax
from jax.experimental import pallas as pl
from jax.experimental.pallas import tpu as pltpu

HEAD_DIM = 128
POOL_WINDOWS = (2, 4, 8, 16)
HALO_ROWS = 16
EPS = 1e-6
MASK_VALUE = -1e30
F32 = jnp.float32
BF16 = jnp.bfloat16
LANES = 128
VMEM_LIMIT_BYTES = 56 * 1024 * 1024


def _pick_tile(n, preferred):
    t = min(preferred, n)
    while n % t:
        t //= 2
    return t


def _params(semantics, vmem=VMEM_LIMIT_BYTES):
    return pltpu.CompilerParams(dimension_semantics=semantics, vmem_limit_bytes=vmem)


def _rms(x, g):
    ms = jnp.mean(x * x, axis=-1, keepdims=True)
    return x * lax.rsqrt(ms + EPS) * g


def _rms_rows_to(dst_ref, dst_row0, src_ref, g_ref, chunk):
    rows = src_ref.shape[0]
    chunk = min(chunk, rows)

    def body(c, carry):
        r0 = pl.multiple_of(c * chunk, chunk)
        x = src_ref[pl.ds(r0, chunk), :]
        dst_ref[pl.ds(dst_row0 + r0, chunk), :] = _rms(x, g_ref[...]).astype(dst_ref.dtype)
        return carry

    lax.fori_loop(0, rows // chunk, body, 0)


def _norm_matmul_body(x_ref, g_ref, w_ref, b_ref, ws_ref, bs_ref, o_ref, os_ref, h_ref):
    @pl.when(pl.program_id(1) == 0)
    def _():
        _rms_rows_to(h_ref, 0, x_ref, g_ref, 128)
        os_ref[...] = jnp.dot(h_ref[...], ws_ref[...], preferred_element_type=F32) + bs_ref[...]

    acc = jnp.dot(h_ref[...], w_ref[...], preferred_element_type=F32)
    o_ref[...] = (acc + b_ref[...]).astype(o_ref.dtype)


def _norm_matmul(x, g, w, b, w_side, b_side, *, tm, tn, name):
    m, d = x.shape
    n = w.shape[1]
    ns = w_side.shape[1]
    return pl.pallas_call(
        _norm_matmul_body,
        grid=(m // tm, n // tn),
        in_specs=[
            pl.BlockSpec((tm, d), lambda i, j: (i, 0)),
            pl.BlockSpec((1, d), lambda i, j: (0, 0)),
            pl.BlockSpec((d, tn), lambda i, j: (0, j)),
            pl.BlockSpec((1, tn), lambda i, j: (0, j)),
            pl.BlockSpec((d, ns), lambda i, j: (0, 0)),
            pl.BlockSpec((1, ns), lambda i, j: (0, 0)),
        ],
        out_specs=[
            pl.BlockSpec((tm, tn), lambda i, j: (i, j)),
            pl.BlockSpec((tm, ns), lambda i, j: (i, 0)),
        ],
        out_shape=[
            jax.ShapeDtypeStruct((m, n), BF16),
            jax.ShapeDtypeStruct((m, ns), F32),
        ],
        scratch_shapes=[pltpu.VMEM((tm, d), BF16)],
        compiler_params=_params(("arbitrary", "arbitrary")),
        name=name,
    )(x, g, w, b, w_side, b_side)


def _cum_body(f_ref, o_ref, *, heads):
    rows = f_ref.shape[0]
    xt = f_ref[...].T[:heads, :]
    c = jnp.minimum(xt, 0.0) - jnp.log1p(jnp.exp(-jnp.abs(xt)))
    lane = lax.broadcasted_iota(jnp.int32, c.shape, 1)
    shift = 1
    while shift < rows:
        c = c + jnp.where(lane >= shift, pltpu.roll(c, shift, axis=1), 0.0)
        shift *= 2
    o_ref[...] = c


def _forget_cumsum(flog, *, batches, heads, name):
    rows = flog.shape[0] // batches
    return pl.pallas_call(
        functools.partial(_cum_body, heads=heads),
        grid=(batches,),
        in_specs=[pl.BlockSpec((rows, flog.shape[1]), lambda b: (b, 0))],
        out_specs=pl.BlockSpec((None, heads, rows), lambda b: (b, 0, 0)),
        out_shape=jax.ShapeDtypeStruct((batches, heads, rows), F32),
        compiler_params=_params(("arbitrary",)),
        name=name,
    )(flog)


def _attn_body(q_ref, k_ref, v_ref, c_ref, *rest, tq, scale, has_prefix):
    if has_prefix:
        kp_ref, vp_ref, cp_ref, o_ref = rest
    else:
        (o_ref,) = rest
    seq = q_ref.shape[0]
    row = lax.broadcasted_iota(jnp.int32, (tq, tq), 0)
    col = lax.broadcasted_iota(jnp.int32, (tq, tq), 1)
    causal = row >= col

    def scores(q, k, bias):
        s = lax.dot_general(q, k, (((1,), (1,)), ((), ())), preferred_element_type=F32)
        return s + bias

    def update(carry, s, v):
        m, l, acc = carry
        m_new = jnp.maximum(m, jnp.max(s, axis=-1, keepdims=True))
        alpha = jnp.exp(m - m_new)
        p = jnp.exp(s - m_new)
        l = alpha * l + jnp.sum(p, axis=-1, keepdims=True)
        acc = alpha * acc + jnp.dot(p.astype(BF16), v, preferred_element_type=F32)
        return m_new, l, acc

    def q_tile(qi, carry_unused):
        q0 = qi * tq if isinstance(qi, int) else pl.multiple_of(qi * tq, tq)
        q = (q_ref[pl.ds(q0, tq), :].astype(F32) * scale).astype(BF16)
        c_q = c_ref[:, pl.ds(q0, tq)]
        c0 = c_q[:, 0:1]

        s = scores(q, k_ref[pl.ds(q0, tq), :], c0 - c_q)
        s = jnp.where(causal, s, MASK_VALUE)
        m = jnp.max(s, axis=-1, keepdims=True)
        p = jnp.exp(s - m)
        l = jnp.sum(p, axis=-1, keepdims=True)
        acc = jnp.dot(p.astype(BF16), v_ref[pl.ds(q0, tq), :], preferred_element_type=F32)

        def kv_tile(j, carry):
            k0 = pl.multiple_of(j * tq, tq)
            s = scores(q, k_ref[pl.ds(k0, tq), :], c0 - c_ref[:, pl.ds(k0, tq)])
            return update(carry, s, v_ref[pl.ds(k0, tq), :])

        carry = lax.fori_loop(0, qi, kv_tile, (m, l, acc))
        if has_prefix:
            cp = cp_ref[...]
            npre = cp.shape[1]
            s = scores(q, kp_ref[...], c0 - (cp - cp[:, npre - 1:npre]))
            carry = update(carry, s, vp_ref[...])
        m, l, acc = carry
        o_ref[pl.ds(q0, tq), :] = (acc / l).astype(o_ref.dtype)
        return carry_unused

    if seq == tq:
        q_tile(0, 0)
    else:
        lax.fori_loop(0, seq // tq, q_tile, 0)


def _attention(z, c, prefix, *, batches, heads, name):
    seq = z.shape[0] // batches
    tq = _pick_tile(seq, 256)
    scale = HEAD_DIM ** -0.5
    has_prefix = prefix is not None
    in_specs = [
        pl.BlockSpec((seq, HEAD_DIM), lambda b, h: (b, h)),
        pl.BlockSpec((seq, HEAD_DIM), lambda b, h: (b, heads + h)),
        pl.BlockSpec((seq, HEAD_DIM), lambda b, h: (b, 2 * heads + h)),
        pl.BlockSpec((None, 1, seq), lambda b, h: (b * heads + h, 0, 0)),
    ]
    args = [z, z, z, c]
    if has_prefix:
        zp, cp = prefix
        npre = zp.shape[0]
        in_specs += [
            pl.BlockSpec((npre, HEAD_DIM), lambda b, h: (0, heads + h)),
            pl.BlockSpec((npre, HEAD_DIM), lambda b, h: (0, 2 * heads + h)),
            pl.BlockSpec((None, 1, npre), lambda b, h: (h, 0, 0)),
        ]
        args += [zp, zp, cp]
    return pl.pallas_call(
        functools.partial(_attn_body, tq=tq, scale=scale, has_prefix=has_prefix),
        grid=(batches, heads),
        in_specs=in_specs,
        out_specs=pl.BlockSpec((seq, HEAD_DIM), lambda b, h: (b, h)),
        out_shape=jax.ShapeDtypeStruct((batches * seq, heads * HEAD_DIM), BF16),
        compiler_params=_params(("arbitrary", "arbitrary")),
        name=name,
    )(*args)


def _pool_body(u_ref, *rest, gdim, pos_offset, tiles_per_batch, has_halo):
    if has_halo:
        uprev_ref, upre_ref, pw_ref, ps_ref, o_ref, ext_ref = rest
    else:
        pw_ref, ps_ref, o_ref, ext_ref = rest
    tm = u_ref.shape[0]
    if has_halo:
        first = (pl.program_id(0) % tiles_per_batch) == 0
        halo = jnp.where(first, upre_ref[...], uprev_ref[...]).astype(F32)
    else:
        halo = jnp.zeros((HALO_ROWS, u_ref.shape[1]), F32)
    ext_ref[0:HALO_ROWS, :] = halo
    ext_ref[HALO_ROWS:HALO_ROWS + tm, :] = u_ref[...].astype(F32)

    for g, w in enumerate(POOL_WINDOWS):
        cols = slice(g * gdim, (g + 1) * gdim)
        cur = ext_ref[HALO_ROWS:HALO_ROWS + tm, cols]
        wsum = cur
        for j in range(1, w):
            wsum = wsum + ext_ref[HALO_ROWS - j:HALO_ROWS - j + tm, cols]
        if pos_offset + 1 >= w:
            mean = wsum * (1.0 / w)
        else:
            pos = lax.broadcasted_iota(jnp.int32, (tm, 1), 0) + pos_offset
            mean = wsum / jnp.minimum(pos + 1, w).astype(F32)
        d = (mean - cur).astype(BF16)
        y = jnp.dot(d, pw_ref[g], preferred_element_type=F32) * ps_ref[:, cols]
        o_ref[:, cols] = y.astype(o_ref.dtype)


def _pool(z, pool_w, pool_scale, prefix_z, *, batches, ublock, name):
    groups, gdim, _ = pool_w.shape
    dp = groups * gdim
    rows = z.shape[0]
    seq = rows // batches
    has_halo = prefix_z is not None
    tm = _pick_tile(seq, 512)
    tiles_per_batch = seq // tm
    in_specs = [pl.BlockSpec((tm, dp), lambda i: (i, ublock))]
    args = [z]
    if has_halo:
        halo_blocks = tm // HALO_ROWS
        in_specs += [
            pl.BlockSpec((HALO_ROWS, dp), lambda i: (jnp.maximum(i * halo_blocks - 1, 0), ublock)),
            pl.BlockSpec((HALO_ROWS, dp), lambda i: (prefix_z.shape[0] // HALO_ROWS - 1, ublock)),
        ]
        args += [z, prefix_z]
        pos_offset = prefix_z.shape[0]
    else:
        pos_offset = 0
    in_specs += [
        pl.BlockSpec((groups, gdim, gdim), lambda i: (0, 0, 0)),
        pl.BlockSpec((1, dp), lambda i: (0, 0)),
    ]
    args += [pool_w, pool_scale]
    return pl.pallas_call(
        functools.partial(_pool_body, gdim=gdim, pos_offset=pos_offset,
                          tiles_per_batch=tiles_per_batch, has_halo=has_halo),
        grid=(rows // tm,),
        in_specs=in_specs,
        out_specs=pl.BlockSpec((tm, dp), lambda i: (i, 0)),
        out_shape=jax.ShapeDtypeStruct((rows, dp), BF16),
        scratch_shapes=[pltpu.VMEM((tm + HALO_ROWS, dp), F32)],
        compiler_params=_params(("arbitrary",)),
        name=name,
    )(*args)


def _mix_body(a_ref, p_ref, ga_ref, gp_ref, x_ref, wa_ref, wp_ref, wo_ref, g_ref, o_ref):
    ya = jnp.dot(a_ref[...], wa_ref[...], preferred_element_type=F32)
    yp = jnp.dot(p_ref[...], wp_ref[...], preferred_element_type=F32)
    m = (jax.nn.sigmoid(ga_ref[...].astype(F32)) * ya
         + jax.nn.sigmoid(gp_ref[...].astype(F32)) * yp)
    mo = jnp.dot(m.astype(BF16), wo_ref[...], preferred_element_type=F32)
    o_ref[...] = x_ref[...] + _rms(mo, g_ref[...])


def _mix(attn, pool, z, x, w_attn_o, w_pool_o, w_out, g_post, *, gate_block, name):
    rows, d = x.shape
    da = attn.shape[1]
    dp = pool.shape[1]
    tm = _pick_tile(rows, 256)
    resident = dict(pipeline_mode=pl.Buffered(1))
    return pl.pallas_call(
        _mix_body,
        grid=(rows // tm,),
        in_specs=[
            pl.BlockSpec((tm, da), lambda i: (i, 0)),
            pl.BlockSpec((tm, dp), lambda i: (i, 0)),
            pl.BlockSpec((tm, d), lambda i: (i, gate_block)),
            pl.BlockSpec((tm, d), lambda i: (i, gate_block + 1)),
            pl.BlockSpec((tm, d), lambda i: (i, 0)),
            pl.BlockSpec((da, d), lambda i: (0, 0), **resident),
            pl.BlockSpec((dp, d), lambda i: (0, 0), **resident),
            pl.BlockSpec((d, d), lambda i: (0, 0), **resident),
            pl.BlockSpec((1, d), lambda i: (0, 0)),
        ],
        out_specs=pl.BlockSpec((tm, d), lambda i: (i, 0)),
        out_shape=jax.ShapeDtypeStruct((rows, d), F32),
        compiler_params=_params(("arbitrary",)),
        name=name,
    )(attn, pool, z, z, x, w_attn_o, w_pool_o, w_out, g_post)


def _gelu_tanh(x):
    c = math.sqrt(2.0 / math.pi)
    return 0.5 * x * (1.0 + jnp.tanh(c * (x + 0.044715 * (x * x * x))))


def _ffn_body(r_ref, rprev_ref, rpre_ref, gpre_ref, wg_ref, wv_ref, cwg_ref, cwv_ref,
              cbg_ref, cbv_ref, wd_ref, gpost_ref, o_ref, h_ref, *, tiles_per_batch):
    i = pl.program_id(0)
    f = pl.program_id(1)
    tm = r_ref.shape[0]

    @pl.when(f == 0)
    def _():
        first = (i % tiles_per_batch) == 0
        halo = jnp.where(first, rpre_ref[...], rprev_ref[...])
        h_ref[0:HALO_ROWS, :] = _rms(halo, gpre_ref[...]).astype(BF16)
        _rms_rows_to(h_ref, HALO_ROWS, r_ref, gpre_ref, 128)

    h = h_ref[...]

    def conv(w_ref, cw_ref, cb_ref):
        a = jnp.dot(h, w_ref[...], preferred_element_type=F32)
        a1 = pltpu.roll(a, 1, axis=0)
        a2 = pltpu.roll(a, 2, axis=0)
        out = cb_ref[...] + a2 * cw_ref[0:1, :]
        out = out + a1 * cw_ref[1:2, :]
        out = out + a * cw_ref[2:3, :]
        return out[HALO_ROWS:, :]

    gate = conv(wg_ref, cwg_ref, cbg_ref)
    val = conv(wv_ref, cwv_ref, cbv_ref)
    act = (_gelu_tanh(gate) * val).astype(BF16)
    contrib = jnp.dot(act, wd_ref[...], preferred_element_type=F32)

    @pl.when(f == 0)
    def _():
        o_ref[...] = contrib

    @pl.when(f > 0)
    def _():
        o_ref[...] += contrib

    @pl.when(f == pl.num_programs(1) - 1)
    def _():
        chunk = min(128, tm)

        def body(c, carry):
            r0 = pl.multiple_of(c * chunk, chunk)
            rows = pl.ds(r0, chunk)
            o_ref[rows, :] = r_ref[rows, :] + _rms(o_ref[rows, :], gpost_ref[...])
            return carry

        lax.fori_loop(0, tm // chunk, body, 0)


def _conv_ffn(r, r_prefix, g_pre, w_up, conv_w, conv_b, w_down, g_post, *, batches, name):
    rows, d = r.shape
    dff = w_down.shape[0]
    seq = rows // batches
    tm = _pick_tile(seq, 512)
    tf = _pick_tile(dff, 512)
    nf = dff // tf
    tiles_per_batch = seq // tm
    halo_blocks = tm // HALO_ROWS
    kw = conv_w.shape[0]
    return pl.pallas_call(
        functools.partial(_ffn_body, tiles_per_batch=tiles_per_batch),
        grid=(rows // tm, nf),
        in_specs=[
            pl.BlockSpec((tm, d), lambda i, f: (i, 0)),
            pl.BlockSpec((HALO_ROWS, d), lambda i, f: (jnp.maximum(i * halo_blocks - 1, 0), 0)),
            pl.BlockSpec((HALO_ROWS, d), lambda i, f: (r_prefix.shape[0] // HALO_ROWS - 1, 0)),
            pl.BlockSpec((1, d), lambda i, f: (0, 0)),
            pl.BlockSpec((d, tf), lambda i, f: (0, f)),
            pl.BlockSpec((d, tf), lambda i, f: (0, nf + f)),
            pl.BlockSpec((kw, tf), lambda i, f: (0, f)),
            pl.BlockSpec((kw, tf), lambda i, f: (0, nf + f)),
            pl.BlockSpec((1, tf), lambda i, f: (0, f)),
            pl.BlockSpec((1, tf), lambda i, f: (0, nf + f)),
            pl.BlockSpec((tf, d), lambda i, f: (f, 0)),
            pl.BlockSpec((1, d), lambda i, f: (0, 0)),
        ],
        out_specs=pl.BlockSpec((tm, d), lambda i, f: (i, 0)),
        out_shape=jax.ShapeDtypeStruct((rows, d), F32),
        scratch_shapes=[pltpu.VMEM((HALO_ROWS + tm, d), BF16)],
        compiler_params=_params(("arbitrary", "arbitrary")),
        name=name,
    )(r, r, r_prefix, g_pre, w_up, w_up, conv_w, conv_w, conv_b, conv_b, w_down, g_post)


def kernel(x, meta_tokens, mix_pre_g, w_in, b_in, w_attn_o, pool_w, pool_scale, w_pool_o, w_out,
           mix_post_g, ffn_pre_g, w_ffn_up, ffn_conv_w, ffn_conv_b, w_ffn_down, ffn_post_g):
    batches, seq, d = x.shape
    n_meta = meta_tokens.shape[0]
    assert w_in.shape[0] == 1, "single-layer block"
    assert n_meta == HALO_ROWS and n_meta >= max(POOL_WINDOWS) and ffn_conv_w.shape[1] <= HALO_ROWS
    d_attn = w_attn_o.shape[1]
    d_pool = w_pool_o.shape[1]
    heads = d_attn // HEAD_DIM
    assert d_attn == d_pool and 2 * d_attn == d
    assert w_in.shape[2] == 3 * d_attn + heads + d_pool + 2 * d

    f0 = 3 * d_attn
    w = w_in[0]
    b = b_in[0]
    w_main = jnp.concatenate([w[:, :f0], w[:, f0 + heads:]], axis=1).astype(BF16)
    b_main = jnp.concatenate([b[:f0], b[f0 + heads:]])[None, :]
    w_f = jnp.pad(w[:, f0:f0 + heads], ((0, 0), (0, LANES - heads))).astype(BF16)
    b_f = jnp.pad(b[f0:f0 + heads], (0, LANES - heads))[None, :]
    wa = w_attn_o[0].astype(BF16)
    wp = w_pool_o[0].astype(BF16)
    wo = w_out[0].astype(BF16)
    pw = pool_w[0].astype(BF16)
    w_up = w_ffn_up[0].astype(BF16)
    w_down = w_ffn_down[0].astype(BF16)
    ublock = f0 // d_pool
    gate_block = (f0 + d_pool) // d

    def mixer(rows_in, prefix, nb, tag):
        n, _ = rows_in.shape
        s = n // nb
        z, flog = _norm_matmul(rows_in, mix_pre_g, w_main, b_main, w_f, b_f,
                               tm=_pick_tile(s, 1024), tn=_pick_tile(w_main.shape[1], 1024),
                               name=f"inproj_{tag}")
        if prefix is None:
            flog = jnp.pad(flog, ((0, LANES - n), (0, 0)))
            c = _forget_cumsum(flog, batches=nb, heads=heads, name=f"cumsum_{tag}")[:, :, :n]
        else:
            c = _forget_cumsum(flog, batches=nb, heads=heads, name=f"cumsum_{tag}")
        c = c.reshape(nb * heads, 1, s)
        attn_prefix = None if prefix is None else (prefix[0], prefix[1])
        attn = _attention(z, c, attn_prefix, batches=nb, heads=heads, name=f"attention_{tag}")
        pool = _pool(z, pw, pool_scale, None if prefix is None else prefix[0],
                     batches=nb, ublock=ublock, name=f"pool_{tag}")
        r = _mix(attn, pool, z, rows_in, wa, wp, wo, mix_post_g,
                 gate_block=gate_block, name=f"mix_{tag}")
        return r, z, c

    r_meta, z_meta, c_meta = mixer(meta_tokens.astype(x.dtype), None, 1, "prefix")
    r, _, _ = mixer(x.reshape(batches * seq, d), (z_meta, c_meta), batches, "main")
    out = _conv_ffn(r, r_meta, ffn_pre_g, w_up, ffn_conv_w[0], ffn_conv_b, w_down, ffn_post_g,
                    batches=batches, name="conv_ffn")
    return out.reshape(batches, seq, d)
```

```python
import functools
import math

import jax
import jax.numpy as jnp
from jax import lax
from jax.experimental import pallas as pl
from jax.experimental.pallas import tpu as pltpu

HEAD_DIM = 128
POOL_WINDOWS = (2, 4, 8, 16)
HALO_ROWS = 16
EPS = 1e-6
MASK_VALUE = -1e30
F32 = jnp.float32
BF16 = jnp.bfloat16
LANES = 128
LOG2E = math.log2(math.e)
N_SPLIT = 3
VMEM_LIMIT_BYTES = 56 * 1024 * 1024


def _pick_tile(n, preferred):
    t = min(preferred, n)
    while n % t:
        t //= 2
    return t


def _params(semantics, vmem=VMEM_LIMIT_BYTES):
    return pltpu.CompilerParams(dimension_semantics=semantics, vmem_limit_bytes=vmem)


def _rms(x, g):
    ms = jnp.mean(x * x, axis=-1, keepdims=True)
    return x * lax.rsqrt(ms + EPS) * g


def _rms_rows_to(dst_ref, dst_row0, src_ref, g_ref, chunk):
    rows = src_ref.shape[0]
    chunk = min(chunk, rows)

    def body(c, carry):
        r0 = pl.multiple_of(c * chunk, chunk)
        x = src_ref[pl.ds(r0, chunk), :]
        dst_ref[pl.ds(dst_row0 + r0, chunk), :] = _rms(x, g_ref[...]).astype(dst_ref.dtype)
        return carry

    lax.fori_loop(0, rows // chunk, body, 0)


def _norm_matmul_body(x_ref, g_ref, w_ref, b_ref, ws_ref, bs_ref, o_ref, os_ref, h_ref):
    @pl.when(pl.program_id(1) == 0)
    def _():
        _rms_rows_to(h_ref, 0, x_ref, g_ref, 128)
        os_ref[...] = jnp.dot(h_ref[...], ws_ref[...], preferred_element_type=F32) + bs_ref[...]

    acc = jnp.dot(h_ref[...], w_ref[...], preferred_element_type=F32)
    o_ref[...] = (acc + b_ref[...]).astype(o_ref.dtype)


def _norm_matmul(x, g, w, b, w_side, b_side, *, tm, tn, name):
    m, d = x.shape
    n = w.shape[1]
    ns = w_side.shape[1]
    return pl.pallas_call(
        _norm_matmul_body,
        grid=(m // tm, n // tn),
        in_specs=[
            pl.BlockSpec((tm, d), lambda i, j: (i, 0)),
            pl.BlockSpec((1, d), lambda i, j: (0, 0)),
            pl.BlockSpec((d, tn), lambda i, j: (0, j)),
            pl.BlockSpec((1, tn), lambda i, j: (0, j)),
            pl.BlockSpec((d, ns), lambda i, j: (0, 0)),
            pl.BlockSpec((1, ns), lambda i, j: (0, 0)),
        ],
        out_specs=[
            pl.BlockSpec((tm, tn), lambda i, j: (i, j)),
            pl.BlockSpec((tm, ns), lambda i, j: (i, 0)),
        ],
        out_shape=[
            jax.ShapeDtypeStruct((m, n), BF16),
            jax.ShapeDtypeStruct((m, ns), F32),
        ],
        scratch_shapes=[pltpu.VMEM((tm, d), BF16)],
        compiler_params=_params(("arbitrary", "arbitrary")),
        name=name,
    )(x, g, w, b, w_side, b_side)


def _cum_body(f_ref, col_ref, row_ref, *, heads):
    x = f_ref[...]
    rows = x.shape[0]
    c = jnp.minimum(x, 0.0) - jnp.log1p(jnp.exp(-jnp.abs(x)))
    t = lax.broadcasted_iota(jnp.int32, c.shape, 0)
    shift = 1
    while shift < rows:
        c = c + jnp.where(t >= shift, pltpu.roll(c, shift, axis=0), 0.0)
        shift *= 2
    col_ref[...] = c
    row_ref[...] = c.T[:heads, :]


def _forget_cumsum(flog, *, batches, heads, name):
    rows = flog.shape[0] // batches
    return pl.pallas_call(
        functools.partial(_cum_body, heads=heads),
        grid=(batches,),
        in_specs=[pl.BlockSpec((rows, flog.shape[1]), lambda b: (b, 0))],
        out_specs=[
            pl.BlockSpec((rows, flog.shape[1]), lambda b: (b, 0)),
            pl.BlockSpec((None, heads, rows), lambda b: (b, 0, 0)),
        ],
        out_shape=[
            jax.ShapeDtypeStruct(flog.shape, F32),
            jax.ShapeDtypeStruct((batches, heads, rows), F32),
        ],
        compiler_params=_params(("arbitrary",)),
        name=name,
    )(flog)


def _attn_prefix_body(q_ref, k_ref, v_ref, c_ref, o_ref, *, scale):
    n = q_ref.shape[0]
    q = (q_ref[...].astype(F32) * scale).astype(BF16)
    s = lax.dot_general(q, k_ref[...], (((1,), (1,)), ((), ())), preferred_element_type=F32)
    s = s - c_ref[...]
    row = lax.broadcasted_iota(jnp.int32, (n, n), 0)
    col = lax.broadcasted_iota(jnp.int32, (n, n), 1)
    s = jnp.where(row >= col, s, MASK_VALUE)
    p = jnp.exp(s - jnp.max(s, axis=-1, keepdims=True))
    l = jnp.sum(p, axis=-1, keepdims=True)
    acc = jnp.dot(p.astype(BF16), v_ref[...], preferred_element_type=F32)
    o_ref[...] = (acc / l).astype(o_ref.dtype)


def _attention_prefix(z, c_row, *, heads, name):
    n = z.shape[0]
    return pl.pallas_call(
        functools.partial(_attn_prefix_body, scale=HEAD_DIM ** -0.5),
        grid=(heads,),
        in_specs=[
            pl.BlockSpec((n, HEAD_DIM), lambda h: (0, h)),
            pl.BlockSpec((n, HEAD_DIM), lambda h: (0, heads + h)),
            pl.BlockSpec((n, HEAD_DIM), lambda h: (0, 2 * heads + h)),
            pl.BlockSpec((None, 1, n), lambda h: (h, 0, 0)),
        ],
        out_specs=pl.BlockSpec((n, HEAD_DIM), lambda h: (0, h)),
        out_shape=jax.ShapeDtypeStruct((n, heads * HEAD_DIM), BF16),
        compiler_params=_params(("arbitrary",)),
        name=name,
    )(z, z, z, c_row)


def _split3(x):
    p1 = x.astype(BF16)
    r1 = x - p1.astype(F32)
    p2 = r1.astype(BF16)
    p3 = (r1 - p2.astype(F32)).astype(BF16)
    return p1, p2, p3


def _bias_columns(pieces, head):
    r = lax.broadcasted_iota(jnp.int32, (N_SPLIT * LANES, LANES), 0)
    c = lax.broadcasted_iota(jnp.int32, (N_SPLIT * LANES, LANES), 1)
    sel = jnp.where(r == c * LANES + head, 1.0, 0.0).astype(BF16)
    return jnp.dot(pieces, sel, preferred_element_type=F32).astype(BF16)


def _attn_main_body(q_ref, k_ref, v_ref, c_ref, kp_ref, vp_ref, cp_ref, o_ref,
                    kaug_ref, vt_ref, kpaug_ref, vpt_ref, *, tq, hp, scale):
    seq = q_ref.shape[0]
    npre = kp_ref.shape[0]
    group = pl.program_id(1)

    pieces = jnp.concatenate(_split3(c_ref[...] * (-LOG2E)), axis=1)
    cp = cp_ref[0:npre, :]
    pieces_p = jnp.concatenate(_split3((cp - cp[npre - 1:npre, :]) * (-LOG2E)), axis=1)
    for hh in range(hp):
        lanes = slice(hh * HEAD_DIM, (hh + 1) * HEAD_DIM)
        head = group * hp + hh
        kaug_ref[hh, :, 0:HEAD_DIM] = k_ref[:, lanes]
        kaug_ref[hh, :, HEAD_DIM:] = _bias_columns(pieces, head)
        vt_ref[hh] = v_ref[:, lanes].astype(F32).T.astype(BF16)
        kpaug_ref[hh, :, 0:HEAD_DIM] = kp_ref[:, lanes]
        kpaug_ref[hh, :, HEAD_DIM:] = _bias_columns(pieces_p, head)
        vp = jnp.concatenate([vp_ref[:, lanes].astype(F32),
                              jnp.zeros((LANES - npre, HEAD_DIM), F32)], axis=0)
        vpt_ref[hh] = vp.T.astype(BF16)

    ones = jnp.where(lax.broadcasted_iota(jnp.int32, (tq, HEAD_DIM), 1) < N_SPLIT, 1.0, 0.0).astype(BF16)
    key = lax.broadcasted_iota(jnp.int32, (tq, tq), 0)
    qry = lax.broadcasted_iota(jnp.int32, (tq, tq), 1)
    causal = key <= qry

    def scores_t(kaug, qaug):
        return lax.dot_general(kaug, qaug, (((1,), (1,)), ((), ())), preferred_element_type=F32)

    def update(carry, st, vt):
        m, l, acc = carry
        m_new = jnp.maximum(m, jnp.max(st, axis=0, keepdims=True))
        alpha = jnp.exp2(m - m_new)
        p = jnp.exp2(st - m_new)
        l = alpha * l + jnp.sum(p, axis=0, keepdims=True)
        acc = alpha * acc + jnp.dot(vt, p.astype(BF16), preferred_element_type=F32)
        return m_new, l, acc

    def q_tile(qi, carry_unused):
        q0 = pl.multiple_of(qi * tq, tq)
        qaugs = []
        carries = []
        for hh in range(hp):
            lanes = slice(hh * HEAD_DIM, (hh + 1) * HEAD_DIM)
            q = (q_ref[pl.ds(q0, tq), lanes].astype(F32) * (scale * LOG2E)).astype(BF16)
            qaug = jnp.concatenate([q, ones], axis=1)
            st = scores_t(kaug_ref[hh, pl.ds(q0, tq), :], qaug)
            st = jnp.where(causal, st, MASK_VALUE)
            m = jnp.max(st, axis=0, keepdims=True)
            p = jnp.exp2(st - m)
            l = jnp.sum(p, axis=0, keepdims=True)
            acc = jnp.dot(vt_ref[hh, :, pl.ds(q0, tq)], p.astype(BF16), preferred_element_type=F32)
            qaugs.append(qaug)
            carries.append((m, l, acc))

        def kv_tile(j, carries):
            k0 = pl.multiple_of(j * tq, tq)
            return tuple(
                update(carries[hh], scores_t(kaug_ref[hh, pl.ds(k0, tq), :], qaugs[hh]),
                       vt_ref[hh, :, pl.ds(k0, tq)])
                for hh in range(hp))

        carries = lax.fori_loop(0, qi, kv_tile, tuple(carries))
        for hh in range(hp):
            lanes = slice(hh * HEAD_DIM, (hh + 1) * HEAD_DIM)
            m, l, acc = carries[hh]
            st = scores_t(kpaug_ref[hh], qaugs[hh])
            m_new = jnp.maximum(m, jnp.max(st, axis=0, keepdims=True))
            alpha = jnp.exp2(m - m_new)
            p = jnp.exp2(st - m_new)
            l = alpha * l + jnp.sum(p, axis=0, keepdims=True)
            p_pad = jnp.concatenate([p.astype(BF16), jnp.zeros((LANES - npre, tq), BF16)], axis=0)
            acc = alpha * acc + jnp.dot(vpt_ref[hh], p_pad, preferred_element_type=F32)
            o_ref[pl.ds(q0, tq), lanes] = (acc / l).T.astype(o_ref.dtype)
        return carry_unused

    lax.fori_loop(0, seq // tq, q_tile, 0)


def _attention_main(z, c_col, z_prefix, c_col_prefix, *, batches, heads, name):
    seq = z.shape[0] // batches
    npre = z_prefix.shape[0]
    tq = _pick_tile(seq, 256)
    hp = _pick_tile(heads, 4)
    groups = heads // hp
    width = hp * HEAD_DIM
    return pl.pallas_call(
        functools.partial(_attn_main_body, tq=tq, hp=hp, scale=HEAD_DIM ** -0.5),
        grid=(batches, groups),
        in_specs=[
            pl.BlockSpec((seq, width), lambda b, g: (b, g)),
            pl.BlockSpec((seq, width), lambda b, g: (b, groups + g)),
            pl.BlockSpec((seq, width), lambda b, g: (b, 2 * groups + g)),
            pl.BlockSpec((seq, LANES), lambda b, g: (b, 0)),
            pl.BlockSpec((npre, width), lambda b, g: (0, groups + g)),
            pl.BlockSpec((npre, width), lambda b, g: (0, 2 * groups + g)),
            pl.BlockSpec(c_col_prefix.shape, lambda b, g: (0, 0)),
        ],
        out_specs=pl.BlockSpec((seq, width), lambda b, g: (b, g)),
        out_shape=jax.ShapeDtypeStruct((batches * seq, heads * HEAD_DIM), BF16),
        scratch_shapes=[
            pltpu.VMEM((hp, seq, 2 * HEAD_DIM), BF16),
            pltpu.VMEM((hp, HEAD_DIM, seq), BF16),
            pltpu.VMEM((hp, npre, 2 * HEAD_DIM), BF16),
            pltpu.VMEM((hp, HEAD_DIM, LANES), BF16),
        ],
        compiler_params=_params(("arbitrary", "arbitrary")),
        name=name,
    )(z, z, z, c_col, z_prefix, z_prefix, c_col_prefix)


def _pool_body(u_ref, *rest, gdim, pos_offset, tiles_per_batch, has_halo):
    if has_halo:
        uprev_ref, upre_ref, pw_ref, ps_ref, o_ref, ext_ref = rest
    else:
        pw_ref, ps_ref, o_ref, ext_ref = rest
    tm = u_ref.shape[0]
    if has_halo:
        first = (pl.program_id(0) % tiles_per_batch) == 0
        halo = jnp.where(first, upre_ref[...], uprev_ref[...]).astype(F32)
    else:
        halo = jnp.zeros((HALO_ROWS, u_ref.shape[1]), F32)
    ext_ref[0:HALO_ROWS, :] = halo
    ext_ref[HALO_ROWS:HALO_ROWS + tm, :] = u_ref[...].astype(F32)

    for g, w in enumerate(POOL_WINDOWS):
        cols = slice(g * gdim, (g + 1) * gdim)
        cur = ext_ref[HALO_ROWS:HALO_ROWS + tm, cols]
        wsum = cur
        for j in range(1, w):
            wsum = wsum + ext_ref[HALO_ROWS - j:HALO_ROWS - j + tm, cols]
        if pos_offset + 1 >= w:
            mean = wsum * (1.0 / w)
        else:
            pos = lax.broadcasted_iota(jnp.int32, (tm, 1), 0) + pos_offset
            mean = wsum / jnp.minimum(pos + 1, w).astype(F32)
        d = (mean - cur).astype(BF16)
        y = jnp.dot(d, pw_ref[g], preferred_element_type=F32) * ps_ref[:, cols]
        o_ref[:, cols] = y.astype(o_ref.dtype)


def _pool(z, pool_w, pool_scale, prefix_z, *, batches, ublock, name):
    groups, gdim, _ = pool_w.shape
    dp = groups * gdim
    rows = z.shape[0]
    seq = rows // batches
    has_halo = prefix_z is not None
    tm = _pick_tile(seq, 512)
    tiles_per_batch = seq // tm
    in_specs = [pl.BlockSpec((tm, dp), lambda i: (i, ublock))]
    args = [z]
    if has_halo:
        halo_blocks = tm // HALO_ROWS
        in_specs += [
            pl.BlockSpec((HALO_ROWS, dp), lambda i: (jnp.maximum(i * halo_blocks - 1, 0), ublock)),
            pl.BlockSpec((HALO_ROWS, dp), lambda i: (prefix_z.shape[0] // HALO_ROWS - 1, ublock)),
        ]
        args += [z, prefix_z]
        pos_offset = prefix_z.shape[0]
    else:
        pos_offset = 0
    in_specs += [
        pl.BlockSpec((groups, gdim, gdim), lambda i: (0, 0, 0)),
        pl.BlockSpec((1, dp), lambda i: (0, 0)),
    ]
    args += [pool_w, pool_scale]
    return pl.pallas_call(
        functools.partial(_pool_body, gdim=gdim, pos_offset=pos_offset,
                          tiles_per_batch=tiles_per_batch, has_halo=has_halo),
        grid=(rows // tm,),
        in_specs=in_specs,
        out_specs=pl.BlockSpec((tm, dp), lambda i: (i, 0)),
        out_shape=jax.ShapeDtypeStruct((rows, dp), BF16),
        scratch_shapes=[pltpu.VMEM((tm + HALO_ROWS, dp), F32)],
        compiler_params=_params(("arbitrary",)),
        name=name,
    )(*args)


def _mix_body(a_ref, p_ref, ga_ref, gp_ref, x_ref, wa_ref, wp_ref, wo_ref, g_ref, o_ref):
    ya = jnp.dot(a_ref[...], wa_ref[...], preferred_element_type=F32)
    yp = jnp.dot(p_ref[...], wp_ref[...], preferred_element_type=F32)
    m = (jax.nn.sigmoid(ga_ref[...].astype(F32)) * ya
         + jax.nn.sigmoid(gp_ref[...].astype(F32)) * yp)
    mo = jnp.dot(m.astype(BF16), wo_ref[...], preferred_element_type=F32)
    o_ref[...] = x_ref[...] + _rms(mo, g_ref[...])


def _mix(attn, pool, z, x, w_attn_o, w_pool_o, w_out, g_post, *, gate_block, name):
    rows, d = x.shape
    da = attn.shape[1]
    dp = pool.shape[1]
    tm = _pick_tile(rows, 256)
    resident = dict(pipeline_mode=pl.Buffered(1))
    return pl.pallas_call(
        _mix_body,
        grid=(rows // tm,),
        in_specs=[
            pl.BlockSpec((tm, da), lambda i: (i, 0)),
            pl.BlockSpec((tm, dp), lambda i: (i, 0)),
            pl.BlockSpec((tm, d), lambda i: (i, gate_block)),
            pl.BlockSpec((tm, d), lambda i: (i, gate_block + 1)),
            pl.BlockSpec((tm, d), lambda i: (i, 0)),
            pl.BlockSpec((da, d), lambda i: (0, 0), **resident),
            pl.BlockSpec((dp, d), lambda i: (0, 0), **resident),
            pl.BlockSpec((d, d), lambda i: (0, 0), **resident),
            pl.BlockSpec((1, d), lambda i: (0, 0)),
        ],
        out_specs=pl.BlockSpec((tm, d), lambda i: (i, 0)),
        out_shape=jax.ShapeDtypeStruct((rows, d), F32),
        compiler_params=_params(("arbitrary",)),
        name=name,
    )(attn, pool, z, z, x, w_attn_o, w_pool_o, w_out, g_post)


def _gelu_tanh(x):
    c = math.sqrt(2.0 / math.pi)
    return 0.5 * x * (1.0 + jnp.tanh(c * (x + 0.044715 * (x * x * x))))


def _ffn_body(r_ref, rprev_ref, rpre_ref, gpre_ref, wg_ref, wv_ref, cwg_ref, cwv_ref,
              cbg_ref, cbv_ref, wd_ref, gpost_ref, o_ref, h_ref, *, tiles_per_batch):
    i = pl.program_id(0)
    f = pl.program_id(1)
    tm = r_ref.shape[0]

    @pl.when(f == 0)
    def _():
        first = (i % tiles_per_batch) == 0
        halo = jnp.where(first, rpre_ref[...], rprev_ref[...])
        h_ref[0:HALO_ROWS, :] = _rms(halo, gpre_ref[...]).astype(BF16)
        _rms_rows_to(h_ref, HALO_ROWS, r_ref, gpre_ref, 128)

    h = h_ref[...]

    def conv(w_ref, cw_ref, cb_ref):
        a = jnp.dot(h, w_ref[...], preferred_element_type=F32)
        a1 = pltpu.roll(a, 1, axis=0)
        a2 = pltpu.roll(a, 2, axis=0)
        out = cb_ref[...] + a2 * cw_ref[0:1, :]
        out = out + a1 * cw_ref[1:2, :]
        out = out + a * cw_ref[2:3, :]
        return out[HALO_ROWS:, :]

    gate = conv(wg_ref, cwg_ref, cbg_ref)
    val = conv(wv_ref, cwv_ref, cbv_ref)
    act = (_gelu_tanh(gate) * val).astype(BF16)
    contrib = jnp.dot(act, wd_ref[...], preferred_element_type=F32)

    @pl.when(f == 0)
    def _():
        o_ref[...] = contrib

    @pl.when(f > 0)
    def _():
        o_ref[...] += contrib

    @pl.when(f == pl.num_programs(1) - 1)
    def _():
        chunk = min(128, tm)

        def body(c, carry):
            r0 = pl.multiple_of(c * chunk, chunk)
            rows = pl.ds(r0, chunk)
            o_ref[rows, :] = r_ref[rows, :] + _rms(o_ref[rows, :], gpost_ref[...])
            return carry

        lax.fori_loop(0, tm // chunk, body, 0)


def _conv_ffn(r, r_prefix, g_pre, w_up, conv_w, conv_b, w_down, g_post, *, batches, name):
    rows, d = r.shape
    dff = w_down.shape[0]
    seq = rows // batches
    tm = _pick_tile(seq, 512)
    tf = _pick_tile(dff, 512)
    nf = dff // tf
    tiles_per_batch = seq // tm
    halo_blocks = tm // HALO_ROWS
    kw = conv_w.shape[0]
    return pl.pallas_call(
        functools.partial(_ffn_body, tiles_per_batch=tiles_per_batch),
        grid=(rows // tm, nf),
        in_specs=[
            pl.BlockSpec((tm, d), lambda i, f: (i, 0)),
            pl.BlockSpec((HALO_ROWS, d), lambda i, f: (jnp.maximum(i * halo_blocks - 1, 0), 0)),
            pl.BlockSpec((HALO_ROWS, d), lambda i, f: (r_prefix.shape[0] // HALO_ROWS - 1, 0)),
            pl.BlockSpec((1, d), lambda i, f: (0, 0)),
            pl.BlockSpec((d, tf), lambda i, f: (0, f)),
            pl.BlockSpec((d, tf), lambda i, f: (0, nf + f)),
            pl.BlockSpec((kw, tf), lambda i, f: (0, f)),
            pl.BlockSpec((kw, tf), lambda i, f: (0, nf + f)),
            pl.BlockSpec((1, tf), lambda i, f: (0, f)),
            pl.BlockSpec((1, tf), lambda i, f: (0, nf + f)),
            pl.BlockSpec((tf, d), lambda i, f: (f, 0)),
            pl.BlockSpec((1, d), lambda i, f: (0, 0)),
        ],
        out_specs=pl.BlockSpec((tm, d), lambda i, f: (i, 0)),
        out_shape=jax.ShapeDtypeStruct((rows, d), F32),
        scratch_shapes=[pltpu.VMEM((HALO_ROWS + tm, d), BF16)],
        compiler_params=_params(("arbitrary", "arbitrary")),
        name=name,
    )(r, r, r_prefix, g_pre, w_up, w_up, conv_w, conv_w, conv_b, conv_b, w_down, g_post)


def kernel(x, meta_tokens, mix_pre_g, w_in, b_in, w_attn_o, pool_w, pool_scale, w_pool_o, w_out,
           mix_post_g, ffn_pre_g, w_ffn_up, ffn_conv_w, ffn_conv_b, w_ffn_down, ffn_post_g):
    batches, seq, d = x.shape
    n_meta = meta_tokens.shape[0]
    assert w_in.shape[0] == 1, "single-layer block"
    assert n_meta == HALO_ROWS and n_meta >= max(POOL_WINDOWS) and ffn_conv_w.shape[1] <= HALO_ROWS
    d_attn = w_attn_o.shape[1]
    d_pool = w_pool_o.shape[1]
    heads = d_attn // HEAD_DIM
    assert d_attn == d_pool and 2 * d_attn == d
    assert w_in.shape[2] == 3 * d_attn + heads + d_pool + 2 * d

    f0 = 3 * d_attn
    w = w_in[0]
    b = b_in[0]
    w_main = jnp.concatenate([w[:, :f0], w[:, f0 + heads:]], axis=1).astype(BF16)
    b_main = jnp.concatenate([b[:f0], b[f0 + heads:]])[None, :]
    w_f = jnp.pad(w[:, f0:f0 + heads], ((0, 0), (0, LANES - heads))).astype(BF16)
    b_f = jnp.pad(b[f0:f0 + heads], (0, LANES - heads))[None, :]
    wa = w_attn_o[0].astype(BF16)
    wp = w_pool_o[0].astype(BF16)
    wo = w_out[0].astype(BF16)
    pw = pool_w[0].astype(BF16)
    w_up = w_ffn_up[0].astype(BF16)
    w_down = w_ffn_down[0].astype(BF16)
    ublock = f0 // d_pool
    gate_block = (f0 + d_pool) // d

    def mixer(rows_in, prefix, nb, tag):
        n, _ = rows_in.shape
        s = n // nb
        z, flog = _norm_matmul(rows_in, mix_pre_g, w_main, b_main, w_f, b_f,
                               tm=_pick_tile(s, 1024), tn=_pick_tile(w_main.shape[1], 1024),
                               name=f"inproj_{tag}")
        if prefix is None:
            flog = jnp.pad(flog, ((0, LANES - n), (0, 0)))
            c_col, c_row = _forget_cumsum(flog, batches=nb, heads=heads, name=f"cumsum_{tag}")
            attn = _attention_prefix(z, c_row[0, :, :n].reshape(heads, 1, n), heads=heads,
                                     name=f"attention_{tag}")
        else:
            c_col, _ = _forget_cumsum(flog, batches=nb, heads=heads, name=f"cumsum_{tag}")
            attn = _attention_main(z, c_col, prefix[0], prefix[1], batches=nb, heads=heads,
                                   name=f"attention_{tag}")
        pool = _pool(z, pw, pool_scale, None if prefix is None else prefix[0],
                     batches=nb, ublock=ublock, name=f"pool_{tag}")
        r = _mix(attn, pool, z, rows_in, wa, wp, wo, mix_post_g,
                 gate_block=gate_block, name=f"mix_{tag}")
        return r, z, c_col

    r_meta, z_meta, c_meta = mixer(meta_tokens.astype(x.dtype), None, 1, "prefix")
    r, _, _ = mixer(x.reshape(batches * seq, d), (z_meta, c_meta), batches, "main")
    out = _conv_ffn(r, r_meta, ffn_pre_g, w_up, ffn_conv_w[0], ffn_conv_b, w_down, ffn_post_g,
                    batches=batches, name="conv_ffn")
    return out.reshape(batches, seq, d)
```

```python
import functools
import math

import jax
import jax.numpy as jnp
from jax import lax
from jax.experimental import pallas as pl
from jax.experimental.pallas import tpu as pltpu

HEAD_DIM = 128
POOL_WINDOWS = (2, 4, 8, 16)
HALO_ROWS = 16
EPS = 1e-6
MASK_VALUE = -1e30
F32 = jnp.float32
BF16 = jnp.bfloat16
LANES = 128
LOG2E = math.log2(math.e)
N_SPLIT = 3
VMEM_LIMIT_BYTES = 56 * 1024 * 1024


def _pick_tile(n, preferred):
    t = min(preferred, n)
    while n % t:
        t //= 2
    return t


def _params(semantics, vmem=VMEM_LIMIT_BYTES):
    return pltpu.CompilerParams(dimension_semantics=semantics, vmem_limit_bytes=vmem)


def _rms(x, g):
    ms = jnp.mean(x * x, axis=-1, keepdims=True)
    return x * lax.rsqrt(ms + EPS) * g


def _rms_rows_to(dst_ref, dst_row0, src_ref, g_ref, chunk):
    rows = src_ref.shape[0]
    chunk = min(chunk, rows)

    def body(c, carry):
        r0 = pl.multiple_of(c * chunk, chunk)
        x = src_ref[pl.ds(r0, chunk), :]
        dst_ref[pl.ds(dst_row0 + r0, chunk), :] = _rms(x, g_ref[...]).astype(dst_ref.dtype)
        return carry

    lax.fori_loop(0, rows // chunk, body, 0)


def _norm_matmul_body(x_ref, g_ref, w_ref, b_ref, ws_ref, bs_ref, o_ref, os_ref, h_ref):
    @pl.when(pl.program_id(1) == 0)
    def _():
        _rms_rows_to(h_ref, 0, x_ref, g_ref, 128)
        os_ref[...] = jnp.dot(h_ref[...], ws_ref[...], preferred_element_type=F32) + bs_ref[...]

    acc = jnp.dot(h_ref[...], w_ref[...], preferred_element_type=F32)
    o_ref[...] = (acc + b_ref[...]).astype(o_ref.dtype)


def _norm_matmul(x, g, w, b, w_side, b_side, *, tm, tn, name):
    m, d = x.shape
    n = w.shape[1]
    ns = w_side.shape[1]
    return pl.pallas_call(
        _norm_matmul_body,
        grid=(m // tm, n // tn),
        in_specs=[
            pl.BlockSpec((tm, d), lambda i, j: (i, 0)),
            pl.BlockSpec((1, d), lambda i, j: (0, 0)),
            pl.BlockSpec((d, tn), lambda i, j: (0, j)),
            pl.BlockSpec((1, tn), lambda i, j: (0, j)),
            pl.BlockSpec((d, ns), lambda i, j: (0, 0)),
            pl.BlockSpec((1, ns), lambda i, j: (0, 0)),
        ],
        out_specs=[
            pl.BlockSpec((tm, tn), lambda i, j: (i, j)),
            pl.BlockSpec((tm, ns), lambda i, j: (i, 0)),
        ],
        out_shape=[
            jax.ShapeDtypeStruct((m, n), BF16),
            jax.ShapeDtypeStruct((m, ns), F32),
        ],
        scratch_shapes=[pltpu.VMEM((tm, d), BF16)],
        compiler_params=_params(("arbitrary", "arbitrary")),
        name=name,
    )(x, g, w, b, w_side, b_side)


def _cum_body(f_ref, col_ref, row_ref, *, heads):
    x = f_ref[...]
    rows = x.shape[0]
    c = jnp.minimum(x, 0.0) - jnp.log1p(jnp.exp(-jnp.abs(x)))
    t = lax.broadcasted_iota(jnp.int32, c.shape, 0)
    shift = 1
    while shift < rows:
        c = c + jnp.where(t >= shift, pltpu.roll(c, shift, axis=0), 0.0)
        shift *= 2
    col_ref[...] = c
    row_ref[...] = c.T[:heads, :]


def _forget_cumsum(flog, *, batches, heads, name):
    rows = flog.shape[0] // batches
    return pl.pallas_call(
        functools.partial(_cum_body, heads=heads),
        grid=(batches,),
        in_specs=[pl.BlockSpec((rows, flog.shape[1]), lambda b: (b, 0))],
        out_specs=[
            pl.BlockSpec((rows, flog.shape[1]), lambda b: (b, 0)),
            pl.BlockSpec((None, heads, rows), lambda b: (b, 0, 0)),
        ],
        out_shape=[
            jax.ShapeDtypeStruct(flog.shape, F32),
            jax.ShapeDtypeStruct((batches, heads, rows), F32),
        ],
        compiler_params=_params(("arbitrary",)),
        name=name,
    )(flog)


def _attn_prefix_body(q_ref, k_ref, v_ref, c_ref, o_ref, *, scale):
    n = q_ref.shape[0]
    q = (q_ref[...].astype(F32) * scale).astype(BF16)
    s = lax.dot_general(q, k_ref[...], (((1,), (1,)), ((), ())), preferred_element_type=F32)
    s = s - c_ref[...]
    row = lax.broadcasted_iota(jnp.int32, (n, n), 0)
    col = lax.broadcasted_iota(jnp.int32, (n, n), 1)
    s = jnp.where(row >= col, s, MASK_VALUE)
    p = jnp.exp(s - jnp.max(s, axis=-1, keepdims=True))
    l = jnp.sum(p, axis=-1, keepdims=True)
    acc = jnp.dot(p.astype(BF16), v_ref[...], preferred_element_type=F32)
    o_ref[...] = (acc / l).astype(o_ref.dtype)


def _attention_prefix(z, c_row, *, heads, name):
    n = z.shape[0]
    return pl.pallas_call(
        functools.partial(_attn_prefix_body, scale=HEAD_DIM ** -0.5),
        grid=(heads,),
        in_specs=[
            pl.BlockSpec((n, HEAD_DIM), lambda h: (0, h)),
            pl.BlockSpec((n, HEAD_DIM), lambda h: (0, heads + h)),
            pl.BlockSpec((n, HEAD_DIM), lambda h: (0, 2 * heads + h)),
            pl.BlockSpec((None, 1, n), lambda h: (h, 0, 0)),
        ],
        out_specs=pl.BlockSpec((n, HEAD_DIM), lambda h: (0, h)),
        out_shape=jax.ShapeDtypeStruct((n, heads * HEAD_DIM), BF16),
        compiler_params=_params(("arbitrary",)),
        name=name,
    )(z, z, z, c_row)


def _split3(x):
    p1 = x.astype(BF16)
    r1 = x - p1.astype(F32)
    p2 = r1.astype(BF16)
    p3 = (r1 - p2.astype(F32)).astype(BF16)
    return p1, p2, p3


def _bias_columns(pieces, head):
    r = lax.broadcasted_iota(jnp.int32, (N_SPLIT * LANES, LANES), 0)
    c = lax.broadcasted_iota(jnp.int32, (N_SPLIT * LANES, LANES), 1)
    sel = jnp.where(r == c * LANES + head, 1.0, 0.0).astype(BF16)
    return jnp.dot(pieces, sel, preferred_element_type=F32).astype(BF16)


def _attn_main_body(q_ref, k_ref, v_ref, c_ref, kp_ref, vp_ref, cp_ref, o_ref,
                    kaug_ref, vt_ref, kpaug_ref, vpt_ref, qaug_ref, m_ref, l_ref, acc_ref, *, tq, hp,
                    scale):
    seq = q_ref.shape[0]
    npre = kp_ref.shape[0]
    group = pl.program_id(1)

    pieces = jnp.concatenate(_split3(c_ref[...] * (-LOG2E)), axis=1)
    cp = cp_ref[0:npre, :]
    pieces_p = jnp.concatenate(_split3((cp - cp[npre - 1:npre, :]) * (-LOG2E)), axis=1)
    for hh in range(hp):
        lanes = slice(hh * HEAD_DIM, (hh + 1) * HEAD_DIM)
        head = group * hp + hh
        kaug_ref[hh, :, 0:HEAD_DIM] = k_ref[:, lanes]
        kaug_ref[hh, :, HEAD_DIM:] = _bias_columns(pieces, head)
        vt_ref[hh] = v_ref[:, lanes].astype(F32).T.astype(BF16)
        kpaug_ref[hh, :, 0:HEAD_DIM] = kp_ref[:, lanes]
        kpaug_ref[hh, :, HEAD_DIM:] = _bias_columns(pieces_p, head)
        vp = jnp.concatenate([vp_ref[:, lanes].astype(F32),
                              jnp.zeros((LANES - npre, HEAD_DIM), F32)], axis=0)
        vpt_ref[hh] = vp.T.astype(BF16)

    ones = jnp.where(lax.broadcasted_iota(jnp.int32, (tq, HEAD_DIM), 1) < N_SPLIT, 1.0, 0.0).astype(BF16)
    key = lax.broadcasted_iota(jnp.int32, (tq, tq), 0)
    qry = lax.broadcasted_iota(jnp.int32, (tq, tq), 1)
    causal = key <= qry
    heads = range(hp)

    def scores_t(kaug, hh):
        return lax.dot_general(kaug, qaug_ref[hh], (((1,), (1,)), ((), ())),
                               preferred_element_type=F32)

    def softmax_update(hh, st):
        m = m_ref[hh]
        m_new = jnp.maximum(m, jnp.max(st, axis=0, keepdims=True))
        alpha = jnp.exp2(m - m_new)
        p = jnp.exp2(st - m_new)
        m_ref[hh] = m_new
        l_ref[hh] = alpha * l_ref[hh] + jnp.sum(p, axis=0, keepdims=True)
        return alpha, p.astype(BF16)

    def q_tile(qi, carry_unused):
        q0 = pl.multiple_of(qi * tq, tq)
        for hh in heads:
            lanes = slice(hh * HEAD_DIM, (hh + 1) * HEAD_DIM)
            q = (q_ref[pl.ds(q0, tq), lanes].astype(F32) * (scale * LOG2E)).astype(BF16)
            qaug_ref[hh] = jnp.concatenate([q, ones], axis=1)

        sts = [scores_t(kaug_ref[hh, pl.ds(q0, tq), :], hh) for hh in heads]
        ps = []
        for hh in heads:
            st = jnp.where(causal, sts[hh], MASK_VALUE)
            m = jnp.max(st, axis=0, keepdims=True)
            p = jnp.exp2(st - m)
            m_ref[hh] = m
            l_ref[hh] = jnp.sum(p, axis=0, keepdims=True)
            ps.append(p.astype(BF16))
        for hh in heads:
            acc_ref[hh] = jnp.dot(vt_ref[hh, :, pl.ds(q0, tq)], ps[hh], preferred_element_type=F32)

        def kv_tile(j, carry):
            k0 = pl.multiple_of(j * tq, tq)
            sts = [scores_t(kaug_ref[hh, pl.ds(k0, tq), :], hh) for hh in heads]
            aps = [softmax_update(hh, sts[hh]) for hh in heads]
            for hh in heads:
                alpha, p = aps[hh]
                acc_ref[hh] = alpha * acc_ref[hh] + jnp.dot(vt_ref[hh, :, pl.ds(k0, tq)], p,
                                                            preferred_element_type=F32)
            return carry

        lax.fori_loop(0, qi, kv_tile, 0)

        sts = [scores_t(kpaug_ref[hh], hh) for hh in heads]
        aps = [softmax_update(hh, sts[hh]) for hh in heads]
        for hh in heads:
            lanes = slice(hh * HEAD_DIM, (hh + 1) * HEAD_DIM)
            alpha, p = aps[hh]
            p_pad = jnp.concatenate([p, jnp.zeros((LANES - npre, tq), BF16)], axis=0)
            acc = alpha * acc_ref[hh] + jnp.dot(vpt_ref[hh], p_pad, preferred_element_type=F32)
            o_ref[pl.ds(q0, tq), lanes] = (acc / l_ref[hh]).T.astype(o_ref.dtype)
        return carry_unused

    lax.fori_loop(0, seq // tq, q_tile, 0)


def _attention_main(z, c_col, z_prefix, c_col_prefix, *, batches, heads, name):
    seq = z.shape[0] // batches
    npre = z_prefix.shape[0]
    tq = _pick_tile(seq, 256)
    hp = _pick_tile(heads, 8)
    groups = heads // hp
    width = hp * HEAD_DIM
    return pl.pallas_call(
        functools.partial(_attn_main_body, tq=tq, hp=hp, scale=HEAD_DIM ** -0.5),
        grid=(batches, groups),
        in_specs=[
            pl.BlockSpec((seq, width), lambda b, g: (b, g)),
            pl.BlockSpec((seq, width), lambda b, g: (b, groups + g)),
            pl.BlockSpec((seq, width), lambda b, g: (b, 2 * groups + g)),
            pl.BlockSpec((seq, LANES), lambda b, g: (b, 0)),
            pl.BlockSpec((npre, width), lambda b, g: (0, groups + g)),
            pl.BlockSpec((npre, width), lambda b, g: (0, 2 * groups + g)),
            pl.BlockSpec(c_col_prefix.shape, lambda b, g: (0, 0)),
        ],
        out_specs=pl.BlockSpec((seq, width), lambda b, g: (b, g)),
        out_shape=jax.ShapeDtypeStruct((batches * seq, heads * HEAD_DIM), BF16),
        scratch_shapes=[
            pltpu.VMEM((hp, seq, 2 * HEAD_DIM), BF16),
            pltpu.VMEM((hp, HEAD_DIM, seq), BF16),
            pltpu.VMEM((hp, npre, 2 * HEAD_DIM), BF16),
            pltpu.VMEM((hp, HEAD_DIM, LANES), BF16),
            pltpu.VMEM((hp, tq, 2 * HEAD_DIM), BF16),
            pltpu.VMEM((hp, 1, tq), F32),
            pltpu.VMEM((hp, 1, tq), F32),
            pltpu.VMEM((hp, HEAD_DIM, tq), F32),
        ],
        compiler_params=_params(("arbitrary", "arbitrary")),
        name=name,
    )(z, z, z, c_col, z_prefix, z_prefix, c_col_prefix)


def _pool_body(u_ref, *rest, gdim, pos_offset, tiles_per_batch, has_halo):
    if has_halo:
        uprev_ref, upre_ref, pw_ref, ps_ref, o_ref, ext_ref = rest
    else:
        pw_ref, ps_ref, o_ref, ext_ref = rest
    tm = u_ref.shape[0]
    if has_halo:
        first = (pl.program_id(0) % tiles_per_batch) == 0
        halo = jnp.where(first, upre_ref[...], uprev_ref[...]).astype(F32)
    else:
        halo = jnp.zeros((HALO_ROWS, u_ref.shape[1]), F32)
    ext_ref[0:HALO_ROWS, :] = halo
    ext_ref[HALO_ROWS:HALO_ROWS + tm, :] = u_ref[...].astype(F32)

    for g, w in enumerate(POOL_WINDOWS):
        cols = slice(g * gdim, (g + 1) * gdim)
        cur = ext_ref[HALO_ROWS:HALO_ROWS + tm, cols]
        wsum = cur
        for j in range(1, w):
            wsum = wsum + ext_ref[HALO_ROWS - j:HALO_ROWS - j + tm, cols]
        if pos_offset + 1 >= w:
            mean = wsum * (1.0 / w)
        else:
            pos = lax.broadcasted_iota(jnp.int32, (tm, 1), 0) + pos_offset
            mean = wsum / jnp.minimum(pos + 1, w).astype(F32)
        d = (mean - cur).astype(BF16)
        y = jnp.dot(d, pw_ref[g], preferred_element_type=F32) * ps_ref[:, cols]
        o_ref[:, cols] = y.astype(o_ref.dtype)


def _pool(z, pool_w, pool_scale, prefix_z, *, batches, ublock, name):
    groups, gdim, _ = pool_w.shape
    dp = groups * gdim
    rows = z.shape[0]
    seq = rows // batches
    has_halo = prefix_z is not None
    tm = _pick_tile(seq, 512)
    tiles_per_batch = seq // tm
    in_specs = [pl.BlockSpec((tm, dp), lambda i: (i, ublock))]
    args = [z]
    if has_halo:
        halo_blocks = tm // HALO_ROWS
        in_specs += [
            pl.BlockSpec((HALO_ROWS, dp), lambda i: (jnp.maximum(i * halo_blocks - 1, 0), ublock)),
            pl.BlockSpec((HALO_ROWS, dp), lambda i: (prefix_z.shape[0] // HALO_ROWS - 1, ublock)),
        ]
        args += [z, prefix_z]
        pos_offset = prefix_z.shape[0]
    else:
        pos_offset = 0
    in_specs += [
        pl.BlockSpec((groups, gdim, gdim), lambda i: (0, 0, 0)),
        pl.BlockSpec((1, dp), lambda i: (0, 0)),
    ]
    args += [pool_w, pool_scale]
    return pl.pallas_call(
        functools.partial(_pool_body, gdim=gdim, pos_offset=pos_offset,
                          tiles_per_batch=tiles_per_batch, has_halo=has_halo),
        grid=(rows // tm,),
        in_specs=in_specs,
        out_specs=pl.BlockSpec((tm, dp), lambda i: (i, 0)),
        out_shape=jax.ShapeDtypeStruct((rows, dp), BF16),
        scratch_shapes=[pltpu.VMEM((tm + HALO_ROWS, dp), F32)],
        compiler_params=_params(("arbitrary",)),
        name=name,
    )(*args)


def _mix_body(a_ref, p_ref, ga_ref, gp_ref, x_ref, wa_ref, wp_ref, wo_ref, g_ref, o_ref):
    ya = jnp.dot(a_ref[...], wa_ref[...], preferred_element_type=F32)
    yp = jnp.dot(p_ref[...], wp_ref[...], preferred_element_type=F32)
    m = (jax.nn.sigmoid(ga_ref[...].astype(F32)) * ya
         + jax.nn.sigmoid(gp_ref[...].astype(F32)) * yp)
    mo = jnp.dot(m.astype(BF16), wo_ref[...], preferred_element_type=F32)
    o_ref[...] = x_ref[...] + _rms(mo, g_ref[...])


def _mix(attn, pool, z, x, w_attn_o, w_pool_o, w_out, g_post, *, gate_block, name):
    rows, d = x.shape
    da = attn.shape[1]
    dp = pool.shape[1]
    tm = _pick_tile(rows, 256)
    resident = dict(pipeline_mode=pl.Buffered(1))
    return pl.pallas_call(
        _mix_body,
        grid=(rows // tm,),
        in_specs=[
            pl.BlockSpec((tm, da), lambda i: (i, 0)),
            pl.BlockSpec((tm, dp), lambda i: (i, 0)),
            pl.BlockSpec((tm, d), lambda i: (i, gate_block)),
            pl.BlockSpec((tm, d), lambda i: (i, gate_block + 1)),
            pl.BlockSpec((tm, d), lambda i: (i, 0)),
            pl.BlockSpec((da, d), lambda i: (0, 0), **resident),
            pl.BlockSpec((dp, d), lambda i: (0, 0), **resident),
            pl.BlockSpec((d, d), lambda i: (0, 0), **resident),
            pl.BlockSpec((1, d), lambda i: (0, 0)),
        ],
        out_specs=pl.BlockSpec((tm, d), lambda i: (i, 0)),
        out_shape=jax.ShapeDtypeStruct((rows, d), F32),
        compiler_params=_params(("arbitrary",)),
        name=name,
    )(attn, pool, z, z, x, w_attn_o, w_pool_o, w_out, g_post)


def _gelu_tanh(x):
    c = math.sqrt(2.0 / math.pi)
    return 0.5 * x * (1.0 + jnp.tanh(c * (x + 0.044715 * (x * x * x))))


def _ffn_body(ru_ref, rprev_ref, rpre_ref, rd_ref, gpre_ref, wg_ref, wv_ref, cwg_ref, cwv_ref,
              cbg_ref, cbv_ref, wd_ref, gpost_ref, o_ref, h_ref, a0_ref, a1_ref, act_ref, *, nf, n_steps,
              tiles_per_batch):
    t = pl.program_id(0)
    tu = jnp.minimum(t, n_steps - 1)
    fu = tu % nf
    iu = tu // nf
    fd = jnp.maximum(t - 1, 0) % nf
    tm = o_ref.shape[0]

    @pl.when(t == 0)
    def _():
        a1_ref[...] = jnp.zeros(a1_ref.shape, a1_ref.dtype)

    @pl.when(jnp.logical_and(fu == 0, t < n_steps))
    def _():
        first = (iu % tiles_per_batch) == 0
        halo = jnp.where(first, rpre_ref[...], rprev_ref[...])
        h_ref[0:HALO_ROWS, :] = _rms(halo, gpre_ref[...]).astype(BF16)
        _rms_rows_to(h_ref, HALO_ROWS, ru_ref, gpre_ref, 128)

    @pl.when(fd == 0)
    def _():
        o_ref[...] = jnp.zeros(o_ref.shape, o_ref.dtype)

    def conv(a, cw_ref, cb_ref):
        a1 = pltpu.roll(a, 1, axis=0)
        a2 = pltpu.roll(a, 2, axis=0)
        out = cb_ref[...] + a2 * cw_ref[0:1, :]
        out = out + a1 * cw_ref[1:2, :]
        out = out + a * cw_ref[2:3, :]
        return out[HALO_ROWS:, :]

    def step(au_ref, ad_ref):
        h = h_ref[...]
        tf = wg_ref.shape[1]
        n_chunks = 4
        cw = 2 * tf // n_chunks
        rw = tm // n_chunks
        for c in range(n_chunks):
            half, c0 = divmod(c * cw, tf)
            w_ref = wv_ref if half else wg_ref
            au_ref[half, :, c0:c0 + cw] = jnp.dot(h, w_ref[:, c0:c0 + cw], preferred_element_type=F32)
            rows = slice(c * rw, c * rw + rw + HALO_ROWS)
            gate = conv(ad_ref[0, rows, :], cwg_ref, cbg_ref)
            val = conv(ad_ref[1, rows, :], cwv_ref, cbv_ref)
            act_ref[c * rw:(c + 1) * rw, :] = (_gelu_tanh(gate) * val).astype(BF16)
        o_ref[...] += jnp.dot(act_ref[...], wd_ref[...], preferred_element_type=F32)

    @pl.when(t % 2 == 0)
    def _():
        step(a0_ref, a1_ref)

    @pl.when(t % 2 == 1)
    def _():
        step(a1_ref, a0_ref)

    @pl.when(jnp.logical_and(fd == nf - 1, t > 0))
    def _():
        chunk = min(128, tm)

        def body(c, carry):
            r0 = pl.multiple_of(c * chunk, chunk)
            rows = pl.ds(r0, chunk)
            o_ref[rows, :] = rd_ref[rows, :] + _rms(o_ref[rows, :], gpost_ref[...])
            return carry

        lax.fori_loop(0, tm // chunk, body, 0)


def _conv_ffn(r, r_prefix, g_pre, w_up, conv_w, conv_b, w_down, g_post, *, batches, name):
    rows, d = r.shape
    dff = w_down.shape[0]
    seq = rows // batches
    tm = _pick_tile(seq, 512)
    tf = _pick_tile(dff, 512)
    nf = dff // tf
    assert nf >= 2
    n_steps = (rows // tm) * nf
    tiles_per_batch = seq // tm
    halo_blocks = tm // HALO_ROWS
    kw = conv_w.shape[0]

    def up_item(t):
        tu = jnp.minimum(t, n_steps - 1)
        return tu // nf, tu % nf

    def down_item(t):
        td = jnp.maximum(t - 1, 0)
        return td // nf, td % nf

    return pl.pallas_call(
        functools.partial(_ffn_body, nf=nf, n_steps=n_steps, tiles_per_batch=tiles_per_batch),
        grid=(n_steps + 1,),
        in_specs=[
            pl.BlockSpec((tm, d), lambda t: (up_item(t)[0], 0)),
            pl.BlockSpec((HALO_ROWS, d), lambda t: (jnp.maximum(up_item(t)[0] * halo_blocks - 1, 0), 0)),
            pl.BlockSpec((HALO_ROWS, d), lambda t: (r_prefix.shape[0] // HALO_ROWS - 1, 0)),
            pl.BlockSpec((tm, d), lambda t: (down_item(t)[0], 0)),
            pl.BlockSpec((1, d), lambda t: (0, 0)),
            pl.BlockSpec((d, tf), lambda t: (0, up_item(t)[1])),
            pl.BlockSpec((d, tf), lambda t: (0, nf + up_item(t)[1])),
            pl.BlockSpec((kw, tf), lambda t: (0, down_item(t)[1])),
            pl.BlockSpec((kw, tf), lambda t: (0, nf + down_item(t)[1])),
            pl.BlockSpec((1, tf), lambda t: (0, down_item(t)[1])),
            pl.BlockSpec((1, tf), lambda t: (0, nf + down_item(t)[1])),
            pl.BlockSpec((tf, d), lambda t: (down_item(t)[1], 0)),
            pl.BlockSpec((1, d), lambda t: (0, 0)),
        ],
        out_specs=pl.BlockSpec((tm, d), lambda t: (down_item(t)[0], 0)),
        out_shape=jax.ShapeDtypeStruct((rows, d), F32),
        scratch_shapes=[
            pltpu.VMEM((HALO_ROWS + tm, d), BF16),
            pltpu.VMEM((2, HALO_ROWS + tm, tf), F32),
            pltpu.VMEM((2, HALO_ROWS + tm, tf), F32),
            pltpu.VMEM((tm, tf), BF16),
        ],
        compiler_params=_params(("arbitrary",)),
        name=name,
    )(r, r, r_prefix, r, g_pre, w_up, w_up, conv_w, conv_w, conv_b, conv_b, w_down, g_post)


def kernel(x, meta_tokens, mix_pre_g, w_in, b_in, w_attn_o, pool_w, pool_scale, w_pool_o, w_out,
           mix_post_g, ffn_pre_g, w_ffn_up, ffn_conv_w, ffn_conv_b, w_ffn_down, ffn_post_g):
    batches, seq, d = x.shape
    n_meta = meta_tokens.shape[0]
    assert w_in.shape[0] == 1, "single-layer block"
    assert n_meta == HALO_ROWS and n_meta >= max(POOL_WINDOWS) and ffn_conv_w.shape[1] <= HALO_ROWS
    d_attn = w_attn_o.shape[1]
    d_pool = w_pool_o.shape[1]
    heads = d_attn // HEAD_DIM
    assert d_attn == d_pool and 2 * d_attn == d
    assert w_in.shape[2] == 3 * d_attn + heads + d_pool + 2 * d

    f0 = 3 * d_attn
    w = w_in[0]
    b = b_in[0]
    w_main = jnp.concatenate([w[:, :f0], w[:, f0 + heads:]], axis=1).astype(BF16)
    b_main = jnp.concatenate([b[:f0], b[f0 + heads:]])[None, :]
    w_f = jnp.pad(w[:, f0:f0 + heads], ((0, 0), (0, LANES - heads))).astype(BF16)
    b_f = jnp.pad(b[f0:f0 + heads], (0, LANES - heads))[None, :]
    wa = w_attn_o[0].astype(BF16)
    wp = w_pool_o[0].astype(BF16)
    wo = w_out[0].astype(BF16)
    pw = pool_w[0].astype(BF16)
    w_up = w_ffn_up[0].astype(BF16)
    w_down = w_ffn_down[0].astype(BF16)
    ublock = f0 // d_pool
    gate_block = (f0 + d_pool) // d

    def mixer(rows_in, prefix, nb, tag):
        n, _ = rows_in.shape
        s = n // nb
        z, flog = _norm_matmul(rows_in, mix_pre_g, w_main, b_main, w_f, b_f,
                               tm=_pick_tile(s, 1024), tn=_pick_tile(w_main.shape[1], 1024),
                               name=f"inproj_{tag}")
        if prefix is None:
            flog = jnp.pad(flog, ((0, LANES - n), (0, 0)))
            c_col, c_row = _forget_cumsum(flog, batches=nb, heads=heads, name=f"cumsum_{tag}")
            attn = _attention_prefix(z, c_row[0, :, :n].reshape(heads, 1, n), heads=heads,
                                     name=f"attention_{tag}")
        else:
            c_col, _ = _forget_cumsum(flog, batches=nb, heads=heads, name=f"cumsum_{tag}")
            attn = _attention_main(z, c_col, prefix[0], prefix[1], batches=nb, heads=heads,
                                   name=f"attention_{tag}")
        pool = _pool(z, pw, pool_scale, None if prefix is None else prefix[0],
                     batches=nb, ublock=ublock, name=f"pool_{tag}")
        r = _mix(attn, pool, z, rows_in, wa, wp, wo, mix_post_g,
                 gate_block=gate_block, name=f"mix_{tag}")
        return r, z, c_col

    r_meta, z_meta, c_meta = mixer(meta_tokens.astype(x.dtype), None, 1, "prefix")
    r, _, _ = mixer(x.reshape(batches * seq, d), (z_meta, c_meta), batches, "main")
    out = _conv_ffn(r, r_meta, ffn_pre_g, w_up, ffn_conv_w[0], ffn_conv_b, w_down, ffn_post_g,
                    batches=batches, name="conv_ffn")
    return out.reshape(batches, seq, d)
```

```python
import functools
import math

import jax
import jax.numpy as jnp
from jax import lax
from jax.experimental import pallas as pl
from jax.experimental.pallas import tpu as pltpu

HEAD_DIM = 128
POOL_WINDOWS = (2, 4, 8, 16)
HALO_ROWS = 16
EPS = 1e-6
MASK_VALUE = -1e30
F32 = jnp.float32
BF16 = jnp.bfloat16
LANES = 128
LOG2E = math.log2(math.e)
N_SPLIT = 3
VMEM_LIMIT_BYTES = 56 * 1024 * 1024


def _pick_tile(n, preferred):
    t = min(preferred, n)
    while n % t:
        t //= 2
    return t


def _params(semantics, vmem=VMEM_LIMIT_BYTES):
    return pltpu.CompilerParams(dimension_semantics=semantics, vmem_limit_bytes=vmem)


def _rms(x, g):
    ms = jnp.mean(x * x, axis=-1, keepdims=True)
    return x * lax.rsqrt(ms + EPS) * g


def _rms_rows_to(dst_ref, dst_row0, src_ref, g_ref, chunk):
    rows = src_ref.shape[0]
    chunk = min(chunk, rows)

    def body(c, carry):
        r0 = pl.multiple_of(c * chunk, chunk)
        x = src_ref[pl.ds(r0, chunk), :]
        dst_ref[pl.ds(dst_row0 + r0, chunk), :] = _rms(x, g_ref[...]).astype(dst_ref.dtype)
        return carry

    lax.fori_loop(0, rows // chunk, body, 0)


def _norm_matmul_body(x_ref, g_ref, wa_ref, wb_ref, b_ref, ws_ref, bs_ref, o_ref, os_ref, h_ref, *, na):
    j = pl.program_id(1)

    @pl.when(j == 0)
    def _():
        _rms_rows_to(h_ref, 0, x_ref, g_ref, 128)
        os_ref[...] = jnp.dot(h_ref[...], ws_ref[...], preferred_element_type=F32) + bs_ref[...]

    def project(w_ref):
        acc = jnp.dot(h_ref[...], w_ref[...], preferred_element_type=F32)
        o_ref[...] = (acc + b_ref[...]).astype(o_ref.dtype)

    @pl.when(j < na)
    def _():
        project(wa_ref)

    @pl.when(j >= na)
    def _():
        project(wb_ref)


def _norm_matmul(x, g, w_a, w_b, b, w_side, b_side, *, tm, tn, name):
    m, d = x.shape
    na = w_a.shape[1] // tn
    nb = w_b.shape[1] // tn
    ns = w_side.shape[1]
    return pl.pallas_call(
        functools.partial(_norm_matmul_body, na=na),
        grid=(m // tm, na + nb),
        in_specs=[
            pl.BlockSpec((tm, d), lambda i, j: (i, 0)),
            pl.BlockSpec((1, d), lambda i, j: (0, 0)),
            pl.BlockSpec((d, tn), lambda i, j: (0, jnp.minimum(j, na - 1))),
            pl.BlockSpec((d, tn), lambda i, j: (0, jnp.maximum(j - na, 0))),
            pl.BlockSpec((1, tn), lambda i, j: (0, j)),
            pl.BlockSpec((d, ns), lambda i, j: (0, 0)),
            pl.BlockSpec((1, ns), lambda i, j: (0, 0)),
        ],
        out_specs=[
            pl.BlockSpec((tm, tn), lambda i, j: (i, j)),
            pl.BlockSpec((tm, ns), lambda i, j: (i, 0)),
        ],
        out_shape=[
            jax.ShapeDtypeStruct((m, (na + nb) * tn), BF16),
            jax.ShapeDtypeStruct((m, ns), F32),
        ],
        scratch_shapes=[pltpu.VMEM((tm, d), BF16)],
        compiler_params=_params(("arbitrary", "arbitrary")),
        name=name,
    )(x, g, w_a, w_b, b, w_side, b_side)


def _cum_body(f_ref, col_ref, row_ref, *, heads):
    x = f_ref[...]
    rows = x.shape[0]
    c = jnp.minimum(x, 0.0) - jnp.log1p(jnp.exp(-jnp.abs(x)))
    t = lax.broadcasted_iota(jnp.int32, c.shape, 0)
    shift = 1
    while shift < rows:
        c = c + jnp.where(t >= shift, pltpu.roll(c, shift, axis=0), 0.0)
        shift *= 2
    col_ref[...] = c
    row_ref[...] = c.T[:heads, :]


def _forget_cumsum(flog, *, batches, heads, name):
    rows = flog.shape[0] // batches
    return pl.pallas_call(
        functools.partial(_cum_body, heads=heads),
        grid=(batches,),
        in_specs=[pl.BlockSpec((rows, flog.shape[1]), lambda b: (b, 0))],
        out_specs=[
            pl.BlockSpec((rows, flog.shape[1]), lambda b: (b, 0)),
            pl.BlockSpec((None, heads, rows), lambda b: (b, 0, 0)),
        ],
        out_shape=[
            jax.ShapeDtypeStruct(flog.shape, F32),
            jax.ShapeDtypeStruct((batches, heads, rows), F32),
        ],
        compiler_params=_params(("arbitrary",)),
        name=name,
    )(flog)


def _attn_prefix_body(q_ref, k_ref, v_ref, c_ref, o_ref, *, scale):
    n = q_ref.shape[0]
    q = (q_ref[...].astype(F32) * scale).astype(BF16)
    s = lax.dot_general(q, k_ref[...], (((1,), (1,)), ((), ())), preferred_element_type=F32)
    s = s - c_ref[...]
    row = lax.broadcasted_iota(jnp.int32, (n, n), 0)
    col = lax.broadcasted_iota(jnp.int32, (n, n), 1)
    s = jnp.where(row >= col, s, MASK_VALUE)
    p = jnp.exp(s - jnp.max(s, axis=-1, keepdims=True))
    l = jnp.sum(p, axis=-1, keepdims=True)
    acc = jnp.dot(p.astype(BF16), v_ref[...], preferred_element_type=F32)
    o_ref[...] = (acc / l).astype(o_ref.dtype)


def _attention_prefix(z, c_row, *, heads, name):
    n = z.shape[0]
    return pl.pallas_call(
        functools.partial(_attn_prefix_body, scale=HEAD_DIM ** -0.5),
        grid=(heads,),
        in_specs=[
            pl.BlockSpec((n, HEAD_DIM), lambda h: (0, h)),
            pl.BlockSpec((n, HEAD_DIM), lambda h: (0, heads + h)),
            pl.BlockSpec((n, HEAD_DIM), lambda h: (0, 2 * heads + h)),
            pl.BlockSpec((None, 1, n), lambda h: (h, 0, 0)),
        ],
        out_specs=pl.BlockSpec((n, HEAD_DIM), lambda h: (0, h)),
        out_shape=jax.ShapeDtypeStruct((n, heads * HEAD_DIM), BF16),
        compiler_params=_params(("arbitrary",)),
        name=name,
    )(z, z, z, c_row)


def _split3(x):
    p1 = x.astype(BF16)
    r1 = x - p1.astype(F32)
    p2 = r1.astype(BF16)
    p3 = (r1 - p2.astype(F32)).astype(BF16)
    return p1, p2, p3


def _bias_columns(pieces, head):
    r = lax.broadcasted_iota(jnp.int32, (N_SPLIT * LANES, LANES), 0)
    c = lax.broadcasted_iota(jnp.int32, (N_SPLIT * LANES, LANES), 1)
    sel = jnp.where(r == c * LANES + head, 1.0, 0.0).astype(BF16)
    return jnp.dot(pieces, sel, preferred_element_type=F32).astype(BF16)


def _attn_main_body(q_ref, k_ref, v_ref, c_ref, kp_ref, vp_ref, cp_ref, o_ref,
                    kaug_ref, vt_ref, kpaug_ref, vpt_ref, qaug_ref, m_ref, l_ref, acc_ref, *, tq, hp,
                    scale):
    seq = q_ref.shape[0]
    npre = kp_ref.shape[0]
    group = pl.program_id(1)

    pieces = jnp.concatenate(_split3(c_ref[...] * (-LOG2E)), axis=1)
    cp = cp_ref[0:npre, :]
    pieces_p = jnp.concatenate(_split3((cp - cp[npre - 1:npre, :]) * (-LOG2E)), axis=1)
    for hh in range(hp):
        lanes = slice(hh * HEAD_DIM, (hh + 1) * HEAD_DIM)
        head = group * hp + hh
        kaug_ref[hh, :, 0:HEAD_DIM] = k_ref[:, lanes]
        kaug_ref[hh, :, HEAD_DIM:] = _bias_columns(pieces, head)
        vt_ref[hh] = v_ref[:, lanes].astype(F32).T.astype(BF16)
        kpaug_ref[hh, :, 0:HEAD_DIM] = kp_ref[:, lanes]
        kpaug_ref[hh, :, HEAD_DIM:] = _bias_columns(pieces_p, head)
        vp = jnp.concatenate([vp_ref[:, lanes].astype(F32),
                              jnp.zeros((LANES - npre, HEAD_DIM), F32)], axis=0)
        vpt_ref[hh] = vp.T.astype(BF16)

    ones = jnp.where(lax.broadcasted_iota(jnp.int32, (tq, HEAD_DIM), 1) < N_SPLIT, 1.0, 0.0).astype(BF16)
    key = lax.broadcasted_iota(jnp.int32, (tq, tq), 0)
    qry = lax.broadcasted_iota(jnp.int32, (tq, tq), 1)
    causal = key <= qry
    heads = range(hp)

    def scores_t(kaug, hh):
        return lax.dot_general(kaug, qaug_ref[hh], (((1,), (1,)), ((), ())),
                               preferred_element_type=F32)

    def softmax_update(hh, st):
        m = m_ref[hh]
        m_new = jnp.maximum(m, jnp.max(st, axis=0, keepdims=True))
        alpha = jnp.exp2(m - m_new)
        p = jnp.exp2(st - m_new)
        m_ref[hh] = m_new
        l_ref[hh] = alpha * l_ref[hh] + jnp.sum(p, axis=0, keepdims=True)
        return alpha, p.astype(BF16)

    def q_tile(qi, carry_unused):
        q0 = pl.multiple_of(qi * tq, tq)
        for hh in heads:
            lanes = slice(hh * HEAD_DIM, (hh + 1) * HEAD_DIM)
            q = (q_ref[pl.ds(q0, tq), lanes].astype(F32) * (scale * LOG2E)).astype(BF16)
            qaug_ref[hh] = jnp.concatenate([q, ones], axis=1)

        sts = [scores_t(kaug_ref[hh, pl.ds(q0, tq), :], hh) for hh in heads]
        ps = []
        for hh in heads:
            st = jnp.where(causal, sts[hh], MASK_VALUE)
            m = jnp.max(st, axis=0, keepdims=True)
            p = jnp.exp2(st - m)
            m_ref[hh] = m
            l_ref[hh] = jnp.sum(p, axis=0, keepdims=True)
            ps.append(p.astype(BF16))
        for hh in heads:
            acc_ref[hh] = jnp.dot(vt_ref[hh, :, pl.ds(q0, tq)], ps[hh], preferred_element_type=F32)

        def kv_tile(j, carry):
            k0 = pl.multiple_of(j * tq, tq)
            sts = [scores_t(kaug_ref[hh, pl.ds(k0, tq), :], hh) for hh in heads]
            aps = [softmax_update(hh, sts[hh]) for hh in heads]
            for hh in heads:
                alpha, p = aps[hh]
                acc_ref[hh] = alpha * acc_ref[hh] + jnp.dot(vt_ref[hh, :, pl.ds(k0, tq)], p,
                                                            preferred_element_type=F32)
            return carry

        lax.fori_loop(0, qi, kv_tile, 0)

        sts = [scores_t(kpaug_ref[hh], hh) for hh in heads]
        aps = [softmax_update(hh, sts[hh]) for hh in heads]
        for hh in heads:
            lanes = slice(hh * HEAD_DIM, (hh + 1) * HEAD_DIM)
            alpha, p = aps[hh]
            p_pad = jnp.concatenate([p, jnp.zeros((LANES - npre, tq), BF16)], axis=0)
            acc = alpha * acc_ref[hh] + jnp.dot(vpt_ref[hh], p_pad, preferred_element_type=F32)
            o_ref[pl.ds(q0, tq), lanes] = (acc / l_ref[hh]).T.astype(o_ref.dtype)
        return carry_unused

    lax.fori_loop(0, seq // tq, q_tile, 0)


def _attention_main(z, c_col, z_prefix, c_col_prefix, *, batches, heads, name):
    seq = z.shape[0] // batches
    npre = z_prefix.shape[0]
    tq = _pick_tile(seq, 256)
    hp = _pick_tile(heads, 8)
    groups = heads // hp
    width = hp * HEAD_DIM
    return pl.pallas_call(
        functools.partial(_attn_main_body, tq=tq, hp=hp, scale=HEAD_DIM ** -0.5),
        grid=(batches, groups),
        in_specs=[
            pl.BlockSpec((seq, width), lambda b, g: (b, g)),
            pl.BlockSpec((seq, width), lambda b, g: (b, groups + g)),
            pl.BlockSpec((seq, width), lambda b, g: (b, 2 * groups + g)),
            pl.BlockSpec((seq, LANES), lambda b, g: (b, 0)),
            pl.BlockSpec((npre, width), lambda b, g: (0, groups + g)),
            pl.BlockSpec((npre, width), lambda b, g: (0, 2 * groups + g)),
            pl.BlockSpec(c_col_prefix.shape, lambda b, g: (0, 0)),
        ],
        out_specs=pl.BlockSpec((seq, width), lambda b, g: (b, g)),
        out_shape=jax.ShapeDtypeStruct((batches * seq, heads * HEAD_DIM), BF16),
        scratch_shapes=[
            pltpu.VMEM((hp, seq, 2 * HEAD_DIM), BF16),
            pltpu.VMEM((hp, HEAD_DIM, seq), BF16),
            pltpu.VMEM((hp, npre, 2 * HEAD_DIM), BF16),
            pltpu.VMEM((hp, HEAD_DIM, LANES), BF16),
            pltpu.VMEM((hp, tq, 2 * HEAD_DIM), BF16),
            pltpu.VMEM((hp, 1, tq), F32),
            pltpu.VMEM((hp, 1, tq), F32),
            pltpu.VMEM((hp, HEAD_DIM, tq), F32),
        ],
        compiler_params=_params(("arbitrary", "arbitrary")),
        name=name,
    )(z, z, z, c_col, z_prefix, z_prefix, c_col_prefix)


def _pool_body(u_ref, *rest, gdim, pos_offset, tiles_per_batch, has_halo):
    if has_halo:
        uprev_ref, upre_ref, pw_ref, ps_ref, o_ref, ext_ref = rest
    else:
        pw_ref, ps_ref, o_ref, ext_ref = rest
    tm = u_ref.shape[0]
    if has_halo:
        first = (pl.program_id(0) % tiles_per_batch) == 0
        halo = jnp.where(first, upre_ref[...], uprev_ref[...]).astype(F32)
    else:
        halo = jnp.zeros((HALO_ROWS, u_ref.shape[1]), F32)
    ext_ref[0:HALO_ROWS, :] = halo
    ext_ref[HALO_ROWS:HALO_ROWS + tm, :] = u_ref[...].astype(F32)

    for g, w in enumerate(POOL_WINDOWS):
        cols = slice(g * gdim, (g + 1) * gdim)
        cur = ext_ref[HALO_ROWS:HALO_ROWS + tm, cols]
        wsum = cur
        for j in range(1, w):
            wsum = wsum + ext_ref[HALO_ROWS - j:HALO_ROWS - j + tm, cols]
        if pos_offset + 1 >= w:
            mean = wsum * (1.0 / w)
        else:
            pos = lax.broadcasted_iota(jnp.int32, (tm, 1), 0) + pos_offset
            mean = wsum / jnp.minimum(pos + 1, w).astype(F32)
        d = (mean - cur).astype(BF16)
        y = jnp.dot(d, pw_ref[g], preferred_element_type=F32) * ps_ref[:, cols]
        o_ref[:, cols] = y.astype(o_ref.dtype)


def _pool(z, pool_w, pool_scale, prefix_z, *, batches, ublock, name):
    groups, gdim, _ = pool_w.shape
    dp = groups * gdim
    rows = z.shape[0]
    seq = rows // batches
    has_halo = prefix_z is not None
    tm = _pick_tile(seq, 512)
    tiles_per_batch = seq // tm
    in_specs = [pl.BlockSpec((tm, dp), lambda i: (i, ublock))]
    args = [z]
    if has_halo:
        halo_blocks = tm // HALO_ROWS
        in_specs += [
            pl.BlockSpec((HALO_ROWS, dp), lambda i: (jnp.maximum(i * halo_blocks - 1, 0), ublock)),
            pl.BlockSpec((HALO_ROWS, dp), lambda i: (prefix_z.shape[0] // HALO_ROWS - 1, ublock)),
        ]
        args += [z, prefix_z]
        pos_offset = prefix_z.shape[0]
    else:
        pos_offset = 0
    in_specs += [
        pl.BlockSpec((groups, gdim, gdim), lambda i: (0, 0, 0)),
        pl.BlockSpec((1, dp), lambda i: (0, 0)),
    ]
    args += [pool_w, pool_scale]
    return pl.pallas_call(
        functools.partial(_pool_body, gdim=gdim, pos_offset=pos_offset,
                          tiles_per_batch=tiles_per_batch, has_halo=has_halo),
        grid=(rows // tm,),
        in_specs=in_specs,
        out_specs=pl.BlockSpec((tm, dp), lambda i: (i, 0)),
        out_shape=jax.ShapeDtypeStruct((rows, dp), BF16),
        scratch_shapes=[pltpu.VMEM((tm + HALO_ROWS, dp), F32)],
        compiler_params=_params(("arbitrary",)),
        name=name,
    )(*args)


def _mix_body(a_ref, p_ref, ga_ref, gp_ref, x_ref, wa_ref, wp_ref, wo_ref, g_ref, o_ref):
    ya = jnp.dot(a_ref[...], wa_ref[...], preferred_element_type=F32)
    yp = jnp.dot(p_ref[...], wp_ref[...], preferred_element_type=F32)
    m = (jax.nn.sigmoid(ga_ref[...].astype(F32)) * ya
         + jax.nn.sigmoid(gp_ref[...].astype(F32)) * yp)
    mo = jnp.dot(m.astype(BF16), wo_ref[...], preferred_element_type=F32)
    o_ref[...] = x_ref[...] + _rms(mo, g_ref[...])


def _mix(attn, pool, z, x, w_attn_o, w_pool_o, w_out, g_post, *, gate_block, name):
    rows, d = x.shape
    da = attn.shape[1]
    dp = pool.shape[1]
    tm = _pick_tile(rows, 256)
    resident = dict(pipeline_mode=pl.Buffered(1))
    return pl.pallas_call(
        _mix_body,
        grid=(rows // tm,),
        in_specs=[
            pl.BlockSpec((tm, da), lambda i: (i, 0)),
            pl.BlockSpec((tm, dp), lambda i: (i, 0)),
            pl.BlockSpec((tm, d), lambda i: (i, gate_block)),
            pl.BlockSpec((tm, d), lambda i: (i, gate_block + 1)),
            pl.BlockSpec((tm, d), lambda i: (i, 0)),
            pl.BlockSpec((da, d), lambda i: (0, 0), **resident),
            pl.BlockSpec((dp, d), lambda i: (0, 0), **resident),
            pl.BlockSpec((d, d), lambda i: (0, 0), **resident),
            pl.BlockSpec((1, d), lambda i: (0, 0)),
        ],
        out_specs=pl.BlockSpec((tm, d), lambda i: (i, 0)),
        out_shape=jax.ShapeDtypeStruct((rows, d), F32),
        compiler_params=_params(("arbitrary",)),
        name=name,
    )(attn, pool, z, z, x, w_attn_o, w_pool_o, w_out, g_post)


def _gelu_tanh(x):
    c = math.sqrt(2.0 / math.pi)
    v = (x * x).astype(F32) * (0.044715 * c) + c
    return (0.5 * x) * (1.0 + jnp.tanh(x * v.astype(x.dtype)))


def _ffn_body(ru_ref, rprev_ref, rpre_ref, rd_ref, gpre_ref, wg_ref, wv_ref, cwg_ref, cwv_ref,
              cbg_ref, cbv_ref, wd_ref, gpost_ref, o_ref, h_ref, a0_ref, a1_ref, *, nf, n_steps,
              tiles_per_batch):
    t = pl.program_id(0)
    tu = jnp.minimum(t, n_steps - 1)
    fu = tu % nf
    iu = tu // nf
    fd = jnp.maximum(t - 1, 0) % nf
    tm = o_ref.shape[0]

    @pl.when(t == 0)
    def _():
        a1_ref[...] = jnp.zeros(a1_ref.shape, a1_ref.dtype)

    @pl.when(jnp.logical_and(fu == 0, t < n_steps))
    def _():
        first = (iu % tiles_per_batch) == 0
        halo = jnp.where(first, rpre_ref[...], rprev_ref[...])
        h_ref[0:HALO_ROWS, :] = _rms(halo, gpre_ref[...]).astype(BF16)
        _rms_rows_to(h_ref, HALO_ROWS, ru_ref, gpre_ref, 128)

    @pl.when(fd == 0)
    def _():
        o_ref[...] = jnp.zeros(o_ref.shape, o_ref.dtype)

    def conv(a_ref, half, cw_ref, cb_ref):
        cw = cw_ref[...].astype(BF16)
        out = cb_ref[...].astype(BF16) + a_ref[half, 2] * cw[0:1, :]
        out = out + a_ref[half, 1] * cw[1:2, :]
        out = out + a_ref[half, 0] * cw[2:3, :]
        return out

    def up_project(a_ref, half, w_ref):
        a = jnp.dot(h_ref[...], w_ref[...], preferred_element_type=F32)
        for k in range(3):
            shifted = a if k == 0 else pltpu.roll(a, k, axis=0)
            a_ref[half, k] = shifted[HALO_ROWS:, :].astype(BF16)

    def step(au_ref, ad_ref):
        up_project(au_ref, 0, wg_ref)
        up_project(au_ref, 1, wv_ref)
        gate = conv(ad_ref, 0, cwg_ref, cbg_ref)
        val = conv(ad_ref, 1, cwv_ref, cbv_ref)
        act = _gelu_tanh(gate) * val
        o_ref[...] += jnp.dot(act, wd_ref[...], preferred_element_type=F32)

    @pl.when(t % 2 == 0)
    def _():
        step(a0_ref, a1_ref)

    @pl.when(t % 2 == 1)
    def _():
        step(a1_ref, a0_ref)

    @pl.when(jnp.logical_and(fd == nf - 1, t > 0))
    def _():
        chunk = min(128, tm)

        def body(c, carry):
            r0 = pl.multiple_of(c * chunk, chunk)
            rows = pl.ds(r0, chunk)
            o_ref[rows, :] = rd_ref[rows, :] + _rms(o_ref[rows, :], gpost_ref[...])
            return carry

        lax.fori_loop(0, tm // chunk, body, 0)


def _conv_ffn(r, r_prefix, g_pre, w_up, conv_w, conv_b, w_down, g_post, *, batches, name):
    rows, d = r.shape
    dff = w_down.shape[0]
    seq = rows // batches
    tm = _pick_tile(seq, 512)
    tf = _pick_tile(dff, 512)
    nf = dff // tf
    assert nf >= 2
    n_steps = (rows // tm) * nf
    tiles_per_batch = seq // tm
    halo_blocks = tm // HALO_ROWS
    kw = conv_w.shape[0]

    def up_item(t):
        tu = jnp.minimum(t, n_steps - 1)
        return tu // nf, tu % nf

    def down_item(t):
        td = jnp.maximum(t - 1, 0)
        return td // nf, td % nf

    return pl.pallas_call(
        functools.partial(_ffn_body, nf=nf, n_steps=n_steps, tiles_per_batch=tiles_per_batch),
        grid=(n_steps + 1,),
        in_specs=[
            pl.BlockSpec((tm, d), lambda t: (up_item(t)[0], 0)),
            pl.BlockSpec((HALO_ROWS, d), lambda t: (jnp.maximum(up_item(t)[0] * halo_blocks - 1, 0), 0)),
            pl.BlockSpec((HALO_ROWS, d), lambda t: (r_prefix.shape[0] // HALO_ROWS - 1, 0)),
            pl.BlockSpec((tm, d), lambda t: (down_item(t)[0], 0)),
            pl.BlockSpec((1, d), lambda t: (0, 0)),
            pl.BlockSpec((d, tf), lambda t: (0, up_item(t)[1])),
            pl.BlockSpec((d, tf), lambda t: (0, nf + up_item(t)[1])),
            pl.BlockSpec((kw, tf), lambda t: (0, down_item(t)[1])),
            pl.BlockSpec((kw, tf), lambda t: (0, nf + down_item(t)[1])),
            pl.BlockSpec((1, tf), lambda t: (0, down_item(t)[1])),
            pl.BlockSpec((1, tf), lambda t: (0, nf + down_item(t)[1])),
            pl.BlockSpec((tf, d), lambda t: (down_item(t)[1], 0)),
            pl.BlockSpec((1, d), lambda t: (0, 0)),
        ],
        out_specs=pl.BlockSpec((tm, d), lambda t: (down_item(t)[0], 0)),
        out_shape=jax.ShapeDtypeStruct((rows, d), F32),
        scratch_shapes=[
            pltpu.VMEM((HALO_ROWS + tm, d), BF16),
            pltpu.VMEM((2, 3, tm, tf), BF16),
            pltpu.VMEM((2, 3, tm, tf), BF16),
        ],
        compiler_params=_params(("arbitrary",)),
        name=name,
    )(r, r, r_prefix, r, g_pre, w_up, w_up, conv_w, conv_w, conv_b, conv_b, w_down, g_post)


def kernel(x, meta_tokens, mix_pre_g, w_in, b_in, w_attn_o, pool_w, pool_scale, w_pool_o, w_out,
           mix_post_g, ffn_pre_g, w_ffn_up, ffn_conv_w, ffn_conv_b, w_ffn_down, ffn_post_g):
    batches, seq, d = x.shape
    n_meta = meta_tokens.shape[0]
    assert w_in.shape[0] == 1, "single-layer block"
    assert n_meta == HALO_ROWS and n_meta >= max(POOL_WINDOWS) and ffn_conv_w.shape[1] <= HALO_ROWS
    d_attn = w_attn_o.shape[1]
    d_pool = w_pool_o.shape[1]
    heads = d_attn // HEAD_DIM
    assert d_attn == d_pool and 2 * d_attn == d
    assert w_in.shape[2] == 3 * d_attn + heads + d_pool + 2 * d

    f0 = 3 * d_attn
    w = w_in[0]
    b = b_in[0]
    w_qkv = w[:, :f0].astype(BF16)
    w_rest = w[:, f0 + heads:].astype(BF16)
    b_main = jnp.concatenate([b[:f0], b[f0 + heads:]])[None, :]
    w_f = jnp.pad(w[:, f0:f0 + heads], ((0, 0), (0, LANES - heads))).astype(BF16)
    b_f = jnp.pad(b[f0:f0 + heads], (0, LANES - heads))[None, :]
    wa = w_attn_o[0].astype(BF16)
    wp = w_pool_o[0].astype(BF16)
    wo = w_out[0].astype(BF16)
    pw = pool_w[0].astype(BF16)
    w_up = w_ffn_up[0].astype(BF16)
    w_down = w_ffn_down[0].astype(BF16)
    ublock = f0 // d_pool
    gate_block = (f0 + d_pool) // d

    def mixer(rows_in, prefix, nb, tag):
        n, _ = rows_in.shape
        s = n // nb
        z, flog = _norm_matmul(rows_in, mix_pre_g, w_qkv, w_rest, b_main, w_f, b_f,
                               tm=_pick_tile(s, 1024), tn=_pick_tile(f0, 1024),
                               name=f"inproj_{tag}")
        if prefix is None:
            flog = jnp.pad(flog, ((0, LANES - n), (0, 0)))
            c_col, c_row = _forget_cumsum(flog, batches=nb, heads=heads, name=f"cumsum_{tag}")
            attn = _attention_prefix(z, c_row[0, :, :n].reshape(heads, 1, n), heads=heads,
                                     name=f"attention_{tag}")
        else:
            c_col, _ = _forget_cumsum(flog, batches=nb, heads=heads, name=f"cumsum_{tag}")
            attn = _attention_main(z, c_col, prefix[0], prefix[1], batches=nb, heads=heads,
                                   name=f"attention_{tag}")
        pool = _pool(z, pw, pool_scale, None if prefix is None else prefix[0],
                     batches=nb, ublock=ublock, name=f"pool_{tag}")
        r = _mix(attn, pool, z, rows_in, wa, wp, wo, mix_post_g,
                 gate_block=gate_block, name=f"mix_{tag}")
        return r, z, c_col

    r_meta, z_meta, c_meta = mixer(meta_tokens.astype(x.dtype), None, 1, "prefix")
    r, _, _ = mixer(x.reshape(batches * seq, d), (z_meta, c_meta), batches, "main")
    out = _conv_ffn(r, r_meta, ffn_pre_g, w_up, ffn_conv_w[0], ffn_conv_b, w_down, ffn_post_g,
                    batches=batches, name="conv_ffn")
    return out.reshape(batches, seq, d)
```

```python
import functools
import math

import jax
import jax.numpy as jnp
from jax import lax
from jax.experimental import pallas as pl
from jax.experimental.pallas import tpu as pltpu

HEAD_DIM = 128
POOL_WINDOWS = (2, 4, 8, 16)
HALO_ROWS = 16
EPS = 1e-6
MASK_VALUE = -1e30
F32 = jnp.float32
BF16 = jnp.bfloat16
LANES = 128
LOG2E = math.log2(math.e)
N_SPLIT = 3
VMEM_LIMIT_BYTES = 56 * 1024 * 1024


def _pick_tile(n, preferred):
    t = min(preferred, n)
    while n % t:
        t //= 2
    return t


def _params(semantics, vmem=VMEM_LIMIT_BYTES):
    return pltpu.CompilerParams(dimension_semantics=semantics, vmem_limit_bytes=vmem)


def _rms(x, g):
    ms = jnp.mean(x * x, axis=-1, keepdims=True)
    return x * lax.rsqrt(ms + EPS) * g


def _rms_rows_to(dst_ref, dst_row0, src_ref, g_ref, chunk):
    rows = src_ref.shape[0]
    chunk = min(chunk, rows)

    def body(c, carry):
        r0 = pl.multiple_of(c * chunk, chunk)
        x = src_ref[pl.ds(r0, chunk), :]
        dst_ref[pl.ds(dst_row0 + r0, chunk), :] = _rms(x, g_ref[...]).astype(dst_ref.dtype)
        return carry

    lax.fori_loop(0, rows // chunk, body, 0)


def _norm_matmul_body(x_ref, g_ref, w_ref, b_ref, ws_ref, bs_ref, o_ref, os_ref, h_ref):
    @pl.when(pl.program_id(1) == 0)
    def _():
        _rms_rows_to(h_ref, 0, x_ref, g_ref, 128)
        os_ref[...] = jnp.dot(h_ref[...], ws_ref[...], preferred_element_type=F32) + bs_ref[...]

    acc = jnp.dot(h_ref[...], w_ref[...], preferred_element_type=F32)
    o_ref[...] = (acc + b_ref[...]).astype(o_ref.dtype)


def _norm_matmul(x, g, w, b, w_side, b_side, *, tm, tn, name):
    m, d = x.shape
    n = w.shape[1]
    ns = w_side.shape[1]
    return pl.pallas_call(
        _norm_matmul_body,
        grid=(m // tm, n // tn),
        in_specs=[
            pl.BlockSpec((tm, d), lambda i, j: (i, 0)),
            pl.BlockSpec((1, d), lambda i, j: (0, 0)),
            pl.BlockSpec((d, tn), lambda i, j: (0, j)),
            pl.BlockSpec((1, tn), lambda i, j: (0, j)),
            pl.BlockSpec((d, ns), lambda i, j: (0, 0)),
            pl.BlockSpec((1, ns), lambda i, j: (0, 0)),
        ],
        out_specs=[
            pl.BlockSpec((tm, tn), lambda i, j: (i, j)),
            pl.BlockSpec((tm, ns), lambda i, j: (i, 0)),
        ],
        out_shape=[
            jax.ShapeDtypeStruct((m, n), BF16),
            jax.ShapeDtypeStruct((m, ns), F32),
        ],
        scratch_shapes=[pltpu.VMEM((tm, d), BF16)],
        compiler_params=_params(("arbitrary", "arbitrary")),
        name=name,
    )(x, g, w, b, w_side, b_side)


def _cum_body(f_ref, col_ref, row_ref, *, heads):
    x = f_ref[...]
    rows = x.shape[0]
    c = jnp.minimum(x, 0.0) - jnp.log1p(jnp.exp(-jnp.abs(x)))
    t = lax.broadcasted_iota(jnp.int32, c.shape, 0)
    shift = 1
    while shift < rows:
        c = c + jnp.where(t >= shift, pltpu.roll(c, shift, axis=0), 0.0)
        shift *= 2
    col_ref[...] = c
    row_ref[...] = c.T[:heads, :]


def _forget_cumsum(flog, *, batches, heads, name):
    rows = flog.shape[0] // batches
    return pl.pallas_call(
        functools.partial(_cum_body, heads=heads),
        grid=(batches,),
        in_specs=[pl.BlockSpec((rows, flog.shape[1]), lambda b: (b, 0))],
        out_specs=[
            pl.BlockSpec((rows, flog.shape[1]), lambda b: (b, 0)),
            pl.BlockSpec((None, heads, rows), lambda b: (b, 0, 0)),
        ],
        out_shape=[
            jax.ShapeDtypeStruct(flog.shape, F32),
            jax.ShapeDtypeStruct((batches, heads, rows), F32),
        ],
        compiler_params=_params(("arbitrary",)),
        name=name,
    )(flog)


def _attn_prefix_body(q_ref, k_ref, v_ref, c_ref, o_ref, *, scale):
    n = q_ref.shape[0]
    q = (q_ref[...].astype(F32) * scale).astype(BF16)
    s = lax.dot_general(q, k_ref[...], (((1,), (1,)), ((), ())), preferred_element_type=F32)
    s = s - c_ref[...]
    row = lax.broadcasted_iota(jnp.int32, (n, n), 0)
    col = lax.broadcasted_iota(jnp.int32, (n, n), 1)
    s = jnp.where(row >= col, s, MASK_VALUE)
    p = jnp.exp(s - jnp.max(s, axis=-1, keepdims=True))
    l = jnp.sum(p, axis=-1, keepdims=True)
    acc = jnp.dot(p.astype(BF16), v_ref[...], preferred_element_type=F32)
    o_ref[...] = (acc / l).astype(o_ref.dtype)


def _attention_prefix(z, c_row, *, heads, name):
    n = z.shape[0]
    return pl.pallas_call(
        functools.partial(_attn_prefix_body, scale=HEAD_DIM ** -0.5),
        grid=(heads,),
        in_specs=[
            pl.BlockSpec((n, HEAD_DIM), lambda h: (0, h)),
            pl.BlockSpec((n, HEAD_DIM), lambda h: (0, heads + h)),
            pl.BlockSpec((n, HEAD_DIM), lambda h: (0, 2 * heads + h)),
            pl.BlockSpec((None, 1, n), lambda h: (h, 0, 0)),
        ],
        out_specs=pl.BlockSpec((n, HEAD_DIM), lambda h: (0, h)),
        out_shape=jax.ShapeDtypeStruct((n, heads * HEAD_DIM), BF16),
        compiler_params=_params(("arbitrary",)),
        name=name,
    )(z, z, z, c_row)


def _split3(x):
    p1 = x.astype(BF16)
    r1 = x - p1.astype(F32)
    p2 = r1.astype(BF16)
    p3 = (r1 - p2.astype(F32)).astype(BF16)
    return p1, p2, p3


def _bias_columns(pieces, head):
    r = lax.broadcasted_iota(jnp.int32, (N_SPLIT * LANES, LANES), 0)
    c = lax.broadcasted_iota(jnp.int32, (N_SPLIT * LANES, LANES), 1)
    sel = jnp.where(r == c * LANES + head, 1.0, 0.0).astype(BF16)
    return jnp.dot(pieces, sel, preferred_element_type=F32).astype(BF16)


def _attn_main_body(q_ref, k_ref, v_ref, c_ref, kp_ref, vp_ref, cp_ref, o_ref,
                    kaug_ref, vt_ref, kpaug_ref, vpt_ref, qaug_ref, m_ref, l_ref, acc_ref, *, tq, hp,
                    scale):
    seq = q_ref.shape[0]
    npre = kp_ref.shape[0]
    group = pl.program_id(1)

    pieces = jnp.concatenate(_split3(c_ref[...] * (-LOG2E)), axis=1)
    cp = cp_ref[0:npre, :]
    pieces_p = jnp.concatenate(_split3((cp - cp[npre - 1:npre, :]) * (-LOG2E)), axis=1)
    for hh in range(hp):
        lanes = slice(hh * HEAD_DIM, (hh + 1) * HEAD_DIM)
        head = group * hp + hh
        kaug_ref[hh, :, 0:HEAD_DIM] = k_ref[:, lanes]
        kaug_ref[hh, :, HEAD_DIM:] = _bias_columns(pieces, head)
        vt_ref[hh] = v_ref[:, lanes].astype(F32).T.astype(BF16)
        kpaug_ref[hh, :, 0:HEAD_DIM] = kp_ref[:, lanes]
        kpaug_ref[hh, :, HEAD_DIM:] = _bias_columns(pieces_p, head)
        vp = jnp.concatenate([vp_ref[:, lanes].astype(F32),
                              jnp.zeros((LANES - npre, HEAD_DIM), F32)], axis=0)
        vpt_ref[hh] = vp.T.astype(BF16)

    ones = jnp.where(lax.broadcasted_iota(jnp.int32, (tq, HEAD_DIM), 1) < N_SPLIT, 1.0, 0.0).astype(BF16)
    key = lax.broadcasted_iota(jnp.int32, (tq, tq), 0)
    qry = lax.broadcasted_iota(jnp.int32, (tq, tq), 1)
    causal = key <= qry
    heads = range(hp)

    def scores_t(kaug, hh):
        return lax.dot_general(kaug, qaug_ref[hh], (((1,), (1,)), ((), ())),
                               preferred_element_type=F32)

    def softmax_update(hh, st):
        m = m_ref[hh]
        m_new = jnp.maximum(m, jnp.max(st, axis=0, keepdims=True))
        alpha = jnp.exp2(m - m_new)
        p = jnp.exp2(st - m_new)
        m_ref[hh] = m_new
        l_ref[hh] = alpha * l_ref[hh] + jnp.sum(p, axis=0, keepdims=True)
        return alpha, p.astype(BF16)

    def q_tile(qi, carry_unused):
        q0 = pl.multiple_of(qi * tq, tq)
        for hh in heads:
            lanes = slice(hh * HEAD_DIM, (hh + 1) * HEAD_DIM)
            q = (q_ref[pl.ds(q0, tq), lanes].astype(F32) * (scale * LOG2E)).astype(BF16)
            qaug_ref[hh] = jnp.concatenate([q, ones], axis=1)

        sts = [scores_t(kaug_ref[hh, pl.ds(q0, tq), :], hh) for hh in heads]
        ps = []
        for hh in heads:
            st = jnp.where(causal, sts[hh], MASK_VALUE)
            m = jnp.max(st, axis=0, keepdims=True)
            p = jnp.exp2(st - m)
            m_ref[hh] = m
            l_ref[hh] = jnp.sum(p, axis=0, keepdims=True)
            ps.append(p.astype(BF16))
        for hh in heads:
            acc_ref[hh] = jnp.dot(vt_ref[hh, :, pl.ds(q0, tq)], ps[hh], preferred_element_type=F32)

        def kv_tile(j, carry):
            k0 = pl.multiple_of(j * tq, tq)
            sts = [scores_t(kaug_ref[hh, pl.ds(k0, tq), :], hh) for hh in heads]
            aps = [softmax_update(hh, sts[hh]) for hh in heads]
            for hh in heads:
                alpha, p = aps[hh]
                acc_ref[hh] = alpha * acc_ref[hh] + jnp.dot(vt_ref[hh, :, pl.ds(k0, tq)], p,
                                                            preferred_element_type=F32)
            return carry

        lax.fori_loop(0, qi, kv_tile, 0)

        sts = [scores_t(kpaug_ref[hh], hh) for hh in heads]
        aps = [softmax_update(hh, sts[hh]) for hh in heads]
        for hh in heads:
            lanes = slice(hh * HEAD_DIM, (hh + 1) * HEAD_DIM)
            alpha, p = aps[hh]
            p_pad = jnp.concatenate([p, jnp.zeros((LANES - npre, tq), BF16)], axis=0)
            acc = alpha * acc_ref[hh] + jnp.dot(vpt_ref[hh], p_pad, preferred_element_type=F32)
            o_ref[pl.ds(q0, tq), lanes] = (acc / l_ref[hh]).T.astype(o_ref.dtype)
        return carry_unused

    lax.fori_loop(0, seq // tq, q_tile, 0)


def _attention_main(z, c_col, z_prefix, c_col_prefix, *, batches, heads, name):
    seq = z.shape[0] // batches
    npre = z_prefix.shape[0]
    tq = _pick_tile(seq, 256)
    hp = _pick_tile(heads, 8)
    groups = heads // hp
    width = hp * HEAD_DIM
    return pl.pallas_call(
        functools.partial(_attn_main_body, tq=tq, hp=hp, scale=HEAD_DIM ** -0.5),
        grid=(batches, groups),
        in_specs=[
            pl.BlockSpec((seq, width), lambda b, g: (b, g)),
            pl.BlockSpec((seq, width), lambda b, g: (b, groups + g)),
            pl.BlockSpec((seq, width), lambda b, g: (b, 2 * groups + g)),
            pl.BlockSpec((seq, LANES), lambda b, g: (b, 0)),
            pl.BlockSpec((npre, width), lambda b, g: (0, groups + g)),
            pl.BlockSpec((npre, width), lambda b, g: (0, 2 * groups + g)),
            pl.BlockSpec(c_col_prefix.shape, lambda b, g: (0, 0)),
        ],
        out_specs=pl.BlockSpec((seq, width), lambda b, g: (b, g)),
        out_shape=jax.ShapeDtypeStruct((batches * seq, heads * HEAD_DIM), BF16),
        scratch_shapes=[
            pltpu.VMEM((hp, seq, 2 * HEAD_DIM), BF16),
            pltpu.VMEM((hp, HEAD_DIM, seq), BF16),
            pltpu.VMEM((hp, npre, 2 * HEAD_DIM), BF16),
            pltpu.VMEM((hp, HEAD_DIM, LANES), BF16),
            pltpu.VMEM((hp, tq, 2 * HEAD_DIM), BF16),
            pltpu.VMEM((hp, 1, tq), F32),
            pltpu.VMEM((hp, 1, tq), F32),
            pltpu.VMEM((hp, HEAD_DIM, tq), F32),
        ],
        compiler_params=_params(("arbitrary", "arbitrary")),
        name=name,
    )(z, z, z, c_col, z_prefix, z_prefix, c_col_prefix)


def _pool_body(u_ref, *rest, gdim, pos_offset, tiles_per_batch, has_halo):
    if has_halo:
        uprev_ref, upre_ref, pw_ref, ps_ref, o_ref, ext_ref = rest
    else:
        pw_ref, ps_ref, o_ref, ext_ref = rest
    tm = u_ref.shape[0]
    if has_halo:
        first = (pl.program_id(0) % tiles_per_batch) == 0
        halo = jnp.where(first, upre_ref[...], uprev_ref[...]).astype(F32)
    else:
        halo = jnp.zeros((HALO_ROWS, u_ref.shape[1]), F32)
    ext_ref[0:HALO_ROWS, :] = halo
    ext_ref[HALO_ROWS:HALO_ROWS + tm, :] = u_ref[...].astype(F32)

    for g, w in enumerate(POOL_WINDOWS):
        cols = slice(g * gdim, (g + 1) * gdim)
        cur = ext_ref[HALO_ROWS:HALO_ROWS + tm, cols]
        wsum = cur
        for j in range(1, w):
            wsum = wsum + ext_ref[HALO_ROWS - j:HALO_ROWS - j + tm, cols]
        if pos_offset + 1 >= w:
            mean = wsum * (1.0 / w)
        else:
            pos = lax.broadcasted_iota(jnp.int32, (tm, 1), 0) + pos_offset
            mean = wsum / jnp.minimum(pos + 1, w).astype(F32)
        d = (mean - cur).astype(BF16)
        y = jnp.dot(d, pw_ref[g], preferred_element_type=F32) * ps_ref[:, cols]
        o_ref[:, cols] = y.astype(o_ref.dtype)


def _pool(z, pool_w, pool_scale, prefix_z, *, batches, ublock, name):
    groups, gdim, _ = pool_w.shape
    dp = groups * gdim
    rows = z.shape[0]
    seq = rows // batches
    has_halo = prefix_z is not None
    tm = _pick_tile(seq, 512)
    tiles_per_batch = seq // tm
    in_specs = [pl.BlockSpec((tm, dp), lambda i: (i, ublock))]
    args = [z]
    if has_halo:
        halo_blocks = tm // HALO_ROWS
        in_specs += [
            pl.BlockSpec((HALO_ROWS, dp), lambda i: (jnp.maximum(i * halo_blocks - 1, 0), ublock)),
            pl.BlockSpec((HALO_ROWS, dp), lambda i: (prefix_z.shape[0] // HALO_ROWS - 1, ublock)),
        ]
        args += [z, prefix_z]
        pos_offset = prefix_z.shape[0]
    else:
        pos_offset = 0
    in_specs += [
        pl.BlockSpec((groups, gdim, gdim), lambda i: (0, 0, 0)),
        pl.BlockSpec((1, dp), lambda i: (0, 0)),
    ]
    args += [pool_w, pool_scale]
    return pl.pallas_call(
        functools.partial(_pool_body, gdim=gdim, pos_offset=pos_offset,
                          tiles_per_batch=tiles_per_batch, has_halo=has_halo),
        grid=(rows // tm,),
        in_specs=in_specs,
        out_specs=pl.BlockSpec((tm, dp), lambda i: (i, 0)),
        out_shape=jax.ShapeDtypeStruct((rows, dp), BF16),
        scratch_shapes=[pltpu.VMEM((tm + HALO_ROWS, dp), F32)],
        compiler_params=_params(("arbitrary",)),
        name=name,
    )(*args)


def _mix_body(a_ref, p_ref, ga_ref, gp_ref, x_ref, wa_ref, wp_ref, wo_ref, g_ref, o_ref):
    ya = jnp.dot(a_ref[...], wa_ref[...], preferred_element_type=F32)
    yp = jnp.dot(p_ref[...], wp_ref[...], preferred_element_type=F32)
    m = (jax.nn.sigmoid(ga_ref[...].astype(F32)) * ya
         + jax.nn.sigmoid(gp_ref[...].astype(F32)) * yp)
    mo = jnp.dot(m.astype(BF16), wo_ref[...], preferred_element_type=F32)
    o_ref[...] = x_ref[...] + _rms(mo, g_ref[...])


def _mix(attn, pool, z, x, w_attn_o, w_pool_o, w_out, g_post, *, gate_block, name):
    rows, d = x.shape
    da = attn.shape[1]
    dp = pool.shape[1]
    tm = _pick_tile(rows, 256)
    resident = dict(pipeline_mode=pl.Buffered(1))
    return pl.pallas_call(
        _mix_body,
        grid=(rows // tm,),
        in_specs=[
            pl.BlockSpec((tm, da), lambda i: (i, 0)),
            pl.BlockSpec((tm, dp), lambda i: (i, 0)),
            pl.BlockSpec((tm, d), lambda i: (i, gate_block)),
            pl.BlockSpec((tm, d), lambda i: (i, gate_block + 1)),
            pl.BlockSpec((tm, d), lambda i: (i, 0)),
            pl.BlockSpec((da, d), lambda i: (0, 0), **resident),
            pl.BlockSpec((dp, d), lambda i: (0, 0), **resident),
            pl.BlockSpec((d, d), lambda i: (0, 0), **resident),
            pl.BlockSpec((1, d), lambda i: (0, 0)),
        ],
        out_specs=pl.BlockSpec((tm, d), lambda i: (i, 0)),
        out_shape=jax.ShapeDtypeStruct((rows, d), F32),
        compiler_params=_params(("arbitrary",)),
        name=name,
    )(attn, pool, z, z, x, w_attn_o, w_pool_o, w_out, g_post)


def _gelu_tanh(x):
    c = math.sqrt(2.0 / math.pi)
    v = (x * x).astype(F32) * (0.044715 * c) + c
    return (0.5 * x) * (1.0 + jnp.tanh(x * v.astype(x.dtype)))


def _ffn_body(ru_ref, rprev_ref, rpre_ref, rd_ref, gpre_ref, wg_ref, wv_ref, cwg_ref, cwv_ref,
              cbg_ref, cbv_ref, wd_ref, gpost_ref, o_ref, h_ref, gate_ref, act0_ref, act1_ref, *, nf,
              n_steps, tiles_per_batch):
    t = pl.program_id(0)
    tu = jnp.minimum(t, n_steps - 1)
    fu = tu % nf
    iu = tu // nf
    fd = jnp.maximum(t - 1, 0) % nf
    tm = o_ref.shape[0]

    @pl.when(t == 0)
    def _():
        act1_ref[...] = jnp.zeros(act1_ref.shape, act1_ref.dtype)

    @pl.when(jnp.logical_and(fu == 0, t < n_steps))
    def _():
        first = (iu % tiles_per_batch) == 0
        halo = jnp.where(first, rpre_ref[...], rprev_ref[...])
        h_ref[0:HALO_ROWS, :] = _rms(halo, gpre_ref[...]).astype(BF16)
        _rms_rows_to(h_ref, HALO_ROWS, ru_ref, gpre_ref, 128)

    @pl.when(fd == 0)
    def _():
        o_ref[...] = jnp.zeros(o_ref.shape, o_ref.dtype)

    def up_conv(w_ref, cw_ref, cb_ref):
        a = jnp.dot(h_ref[...], w_ref[...], preferred_element_type=F32)
        a0, a1, a2 = [(a if k == 0 else pltpu.roll(a, k, axis=0))[HALO_ROWS:, :].astype(BF16)
                      for k in range(3)]
        cw = cw_ref[...].astype(BF16)
        out = cb_ref[...].astype(BF16) + a2 * cw[0:1, :]
        out = out + a1 * cw[1:2, :]
        out = out + a0 * cw[2:3, :]
        return out

    def step(act_u_ref, act_d_ref):
        gate_ref[...] = up_conv(wg_ref, cwg_ref, cbg_ref)
        val = up_conv(wv_ref, cwv_ref, cbv_ref)
        act_u_ref[...] = _gelu_tanh(gate_ref[...]) * val
        o_ref[...] += jnp.dot(act_d_ref[...], wd_ref[...], preferred_element_type=F32)

    @pl.when(t % 2 == 0)
    def _():
        step(act0_ref, act1_ref)

    @pl.when(t % 2 == 1)
    def _():
        step(act1_ref, act0_ref)

    @pl.when(jnp.logical_and(fd == nf - 1, t > 0))
    def _():
        chunk = min(128, tm)

        def body(c, carry):
            r0 = pl.multiple_of(c * chunk, chunk)
            rows = pl.ds(r0, chunk)
            o_ref[rows, :] = rd_ref[rows, :] + _rms(o_ref[rows, :], gpost_ref[...])
            return carry

        lax.fori_loop(0, tm // chunk, body, 0)


def _conv_ffn(r, r_prefix, g_pre, w_up, conv_w, conv_b, w_down, g_post, *, batches, name):
    rows, d = r.shape
    dff = w_down.shape[0]
    seq = rows // batches
    tm = _pick_tile(seq, 512)
    tf = _pick_tile(dff, 512)
    nf = dff // tf
    assert nf >= 2
    n_steps = (rows // tm) * nf
    tiles_per_batch = seq // tm
    halo_blocks = tm // HALO_ROWS
    kw = conv_w.shape[0]

    def up_item(t):
        tu = jnp.minimum(t, n_steps - 1)
        return tu // nf, tu % nf

    def down_item(t):
        td = jnp.maximum(t - 1, 0)
        return td // nf, td % nf

    return pl.pallas_call(
        functools.partial(_ffn_body, nf=nf, n_steps=n_steps, tiles_per_batch=tiles_per_batch),
        grid=(n_steps + 1,),
        in_specs=[
            pl.BlockSpec((tm, d), lambda t: (up_item(t)[0], 0)),
            pl.BlockSpec((HALO_ROWS, d), lambda t: (jnp.maximum(up_item(t)[0] * halo_blocks - 1, 0), 0)),
            pl.BlockSpec((HALO_ROWS, d), lambda t: (r_prefix.shape[0] // HALO_ROWS - 1, 0)),
            pl.BlockSpec((tm, d), lambda t: (down_item(t)[0], 0)),
            pl.BlockSpec((1, d), lambda t: (0, 0)),
            pl.BlockSpec((d, tf), lambda t: (0, up_item(t)[1])),
            pl.BlockSpec((d, tf), lambda t: (0, nf + up_item(t)[1])),
            pl.BlockSpec((kw, tf), lambda t: (0, up_item(t)[1])),
            pl.BlockSpec((kw, tf), lambda t: (0, nf + up_item(t)[1])),
            pl.BlockSpec((1, tf), lambda t: (0, up_item(t)[1])),
            pl.BlockSpec((1, tf), lambda t: (0, nf + up_item(t)[1])),
            pl.BlockSpec((tf, d), lambda t: (down_item(t)[1], 0)),
            pl.BlockSpec((1, d), lambda t: (0, 0)),
        ],
        out_specs=pl.BlockSpec((tm, d), lambda t: (down_item(t)[0], 0)),
        out_shape=jax.ShapeDtypeStruct((rows, d), F32),
        scratch_shapes=[
            pltpu.VMEM((HALO_ROWS + tm, d), BF16),
            pltpu.VMEM((tm, tf), BF16),
            pltpu.VMEM((tm, tf), BF16),
            pltpu.VMEM((tm, tf), BF16),
        ],
        compiler_params=_params(("arbitrary",)),
        name=name,
    )(r, r, r_prefix, r, g_pre, w_up, w_up, conv_w, conv_w, conv_b, conv_b, w_down, g_post)


def kernel(x, meta_tokens, mix_pre_g, w_in, b_in, w_attn_o, pool_w, pool_scale, w_pool_o, w_out,
           mix_post_g, ffn_pre_g, w_ffn_up, ffn_conv_w, ffn_conv_b, w_ffn_down, ffn_post_g):
    batches, seq, d = x.shape
    n_meta = meta_tokens.shape[0]
    assert w_in.shape[0] == 1, "single-layer block"
    assert n_meta == HALO_ROWS and n_meta >= max(POOL_WINDOWS) and ffn_conv_w.shape[1] <= HALO_ROWS
    d_attn = w_attn_o.shape[1]
    d_pool = w_pool_o.shape[1]
    heads = d_attn // HEAD_DIM
    assert d_attn == d_pool and 2 * d_attn == d
    assert w_in.shape[2] == 3 * d_attn + heads + d_pool + 2 * d

    f0 = 3 * d_attn
    w = w_in[0]
    b = b_in[0]
    w_main = jnp.concatenate([w[:, :f0], w[:, f0 + heads:]], axis=1).astype(BF16)
    b_main = jnp.concatenate([b[:f0], b[f0 + heads:]])[None, :]
    w_f = jnp.pad(w[:, f0:f0 + heads], ((0, 0), (0, LANES - heads))).astype(BF16)
    b_f = jnp.pad(b[f0:f0 + heads], (0, LANES - heads))[None, :]
    wa = w_attn_o[0].astype(BF16)
    wp = w_pool_o[0].astype(BF16)
    wo = w_out[0].astype(BF16)
    pw = pool_w[0].astype(BF16)
    w_up = w_ffn_up[0].astype(BF16)
    w_down = w_ffn_down[0].astype(BF16)
    ublock = f0 // d_pool
    gate_block = (f0 + d_pool) // d

    def mixer(rows_in, prefix, nb, tag):
        n, _ = rows_in.shape
        s = n // nb
        z, flog = _norm_matmul(rows_in, mix_pre_g, w_main, b_main, w_f, b_f,
                               tm=_pick_tile(s, 1024), tn=_pick_tile(w_main.shape[1], 1024),
                               name=f"inproj_{tag}")
        if prefix is None:
            flog = jnp.pad(flog, ((0, LANES - n), (0, 0)))
            c_col, c_row = _forget_cumsum(flog, batches=nb, heads=heads, name=f"cumsum_{tag}")
            attn = _attention_prefix(z, c_row[0, :, :n].reshape(heads, 1, n), heads=heads,
                                     name=f"attention_{tag}")
        else:
            c_col, _ = _forget_cumsum(flog, batches=nb, heads=heads, name=f"cumsum_{tag}")
            attn = _attention_main(z, c_col, prefix[0], prefix[1], batches=nb, heads=heads,
                                   name=f"attention_{tag}")
        pool = _pool(z, pw, pool_scale, None if prefix is None else prefix[0],
                     batches=nb, ublock=ublock, name=f"pool_{tag}")
        r = _mix(attn, pool, z, rows_in, wa, wp, wo, mix_post_g,
                 gate_block=gate_block, name=f"mix_{tag}")
        return r, z, c_col

    r_meta, z_meta, c_meta = mixer(meta_tokens.astype(x.dtype), None, 1, "prefix")
    r, _, _ = mixer(x.reshape(batches * seq, d), (z_meta, c_meta), batches, "main")
    out = _conv_ffn(r, r_meta, ffn_pre_g, w_up, ffn_conv_w[0], ffn_conv_b, w_down, ffn_post_g,
                    batches=batches, name="conv_ffn")
    return out.reshape(batches, seq, d)
```

```python
import functools
import math

import jax
import jax.numpy as jnp
from jax import lax
from jax.experimental import pallas as pl
from jax.experimental.pallas import tpu as pltpu

HEAD_DIM = 128
POOL_WINDOWS = (2, 4, 8, 16)
HALO_ROWS = 16
EPS = 1e-6
MASK_VALUE = -1e30
F32 = jnp.float32
BF16 = jnp.bfloat16
LANES = 128
LOG2E = math.log2(math.e)
N_SPLIT = 3
VMEM_LIMIT_BYTES = 56 * 1024 * 1024


def _pick_tile(n, preferred):
    t = min(preferred, n)
    while n % t:
        t //= 2
    return t


def _params(semantics, vmem=VMEM_LIMIT_BYTES):
    return pltpu.CompilerParams(dimension_semantics=semantics, vmem_limit_bytes=vmem)


def _rms(x, g):
    ms = jnp.mean(x * x, axis=-1, keepdims=True)
    return x * lax.rsqrt(ms + EPS) * g


def _rms_rows_to(dst_ref, dst_row0, src_ref, g_ref, chunk):
    rows = src_ref.shape[0]
    chunk = min(chunk, rows)

    def body(c, carry):
        r0 = pl.multiple_of(c * chunk, chunk)
        x = src_ref[pl.ds(r0, chunk), :]
        dst_ref[pl.ds(dst_row0 + r0, chunk), :] = _rms(x, g_ref[...]).astype(dst_ref.dtype)
        return carry

    lax.fori_loop(0, rows // chunk, body, 0)


def _norm_matmul_body(x_ref, g_ref, w_ref, b_ref, ws_ref, bs_ref, o_ref, os_ref, h_ref):
    @pl.when(pl.program_id(1) == 0)
    def _():
        _rms_rows_to(h_ref, 0, x_ref, g_ref, 128)
        os_ref[...] = jnp.dot(h_ref[...], ws_ref[...], preferred_element_type=F32) + bs_ref[...]

    acc = jnp.dot(h_ref[...], w_ref[...], preferred_element_type=F32)
    o_ref[...] = (acc + b_ref[...]).astype(o_ref.dtype)


def _norm_matmul(x, g, w, b, w_side, b_side, *, tm, tn, name):
    m, d = x.shape
    n = w.shape[1]
    ns = w_side.shape[1]
    return pl.pallas_call(
        _norm_matmul_body,
        grid=(m // tm, n // tn),
        in_specs=[
            pl.BlockSpec((tm, d), lambda i, j: (i, 0)),
            pl.BlockSpec((1, d), lambda i, j: (0, 0)),
            pl.BlockSpec((d, tn), lambda i, j: (0, j)),
            pl.BlockSpec((1, tn), lambda i, j: (0, j)),
            pl.BlockSpec((d, ns), lambda i, j: (0, 0)),
            pl.BlockSpec((1, ns), lambda i, j: (0, 0)),
        ],
        out_specs=[
            pl.BlockSpec((tm, tn), lambda i, j: (i, j)),
            pl.BlockSpec((tm, ns), lambda i, j: (i, 0)),
        ],
        out_shape=[
            jax.ShapeDtypeStruct((m, n), BF16),
            jax.ShapeDtypeStruct((m, ns), F32),
        ],
        scratch_shapes=[pltpu.VMEM((tm, d), BF16)],
        compiler_params=_params(("arbitrary", "arbitrary")),
        name=name,
    )(x, g, w, b, w_side, b_side)


def _stage_body(a_ref, b_ref, o_ref, *, first_shifted, shift):
    j = pl.program_id(1)

    @pl.when(j < first_shifted)
    def _():
        o_ref[...] = a_ref[...].astype(o_ref.dtype)

    @pl.when(j >= first_shifted)
    def _():
        both = jnp.concatenate([a_ref[...], b_ref[...]], axis=1)
        width = both.shape[1]
        o_ref[...] = pltpu.roll(both, width - shift, axis=1)[:, :o_ref.shape[1]].astype(o_ref.dtype)


def _stage_in_weights(w, *, skip_start, skip, name):
    d, n = w.shape
    tn = _pick_tile(skip_start, 1024)
    n_out = n - skip
    assert skip < LANES and n_out % tn == 0 and skip_start % tn == 0
    tr = _pick_tile(d, 1024)
    edge = tn // LANES
    return pl.pallas_call(
        functools.partial(_stage_body, first_shifted=skip_start // tn, shift=skip),
        grid=(d // tr, n_out // tn),
        in_specs=[
            pl.BlockSpec((tr, tn), lambda i, j: (i, j)),
            pl.BlockSpec((tr, LANES), lambda i, j: (i, (j + 1) * edge)),
        ],
        out_specs=pl.BlockSpec((tr, tn), lambda i, j: (i, j)),
        out_shape=jax.ShapeDtypeStruct((d, n_out), BF16),
        compiler_params=_params(("arbitrary", "arbitrary")),
        name=name,
    )(w, w)


def _cum_body(f_ref, col_ref, row_ref, *, heads):
    x = f_ref[...]
    rows = x.shape[0]
    c = jnp.minimum(x, 0.0) - jnp.log1p(jnp.exp(-jnp.abs(x)))
    t = lax.broadcasted_iota(jnp.int32, c.shape, 0)
    shift = 1
    while shift < rows:
        c = c + jnp.where(t >= shift, pltpu.roll(c, shift, axis=0), 0.0)
        shift *= 2
    col_ref[...] = c
    row_ref[...] = c.T[:heads, :]


def _forget_cumsum(flog, *, batches, heads, name):
    rows = flog.shape[0] // batches
    return pl.pallas_call(
        functools.partial(_cum_body, heads=heads),
        grid=(batches,),
        in_specs=[pl.BlockSpec((rows, flog.shape[1]), lambda b: (b, 0))],
        out_specs=[
            pl.BlockSpec((rows, flog.shape[1]), lambda b: (b, 0)),
            pl.BlockSpec((None, heads, rows), lambda b: (b, 0, 0)),
        ],
        out_shape=[
            jax.ShapeDtypeStruct(flog.shape, F32),
            jax.ShapeDtypeStruct((batches, heads, rows), F32),
        ],
        compiler_params=_params(("arbitrary",)),
        name=name,
    )(flog)


def _attn_prefix_body(q_ref, k_ref, v_ref, c_ref, o_ref, *, scale):
    n = q_ref.shape[0]
    q = (q_ref[...].astype(F32) * scale).astype(BF16)
    s = lax.dot_general(q, k_ref[...], (((1,), (1,)), ((), ())), preferred_element_type=F32)
    s = s - c_ref[...]
    row = lax.broadcasted_iota(jnp.int32, (n, n), 0)
    col = lax.broadcasted_iota(jnp.int32, (n, n), 1)
    s = jnp.where(row >= col, s, MASK_VALUE)
    p = jnp.exp(s - jnp.max(s, axis=-1, keepdims=True))
    l = jnp.sum(p, axis=-1, keepdims=True)
    acc = jnp.dot(p.astype(BF16), v_ref[...], preferred_element_type=F32)
    o_ref[...] = (acc / l).astype(o_ref.dtype)


def _attention_prefix(z, c_row, *, heads, name):
    n = z.shape[0]
    return pl.pallas_call(
        functools.partial(_attn_prefix_body, scale=HEAD_DIM ** -0.5),
        grid=(heads,),
        in_specs=[
            pl.BlockSpec((n, HEAD_DIM), lambda h: (0, h)),
            pl.BlockSpec((n, HEAD_DIM), lambda h: (0, heads + h)),
            pl.BlockSpec((n, HEAD_DIM), lambda h: (0, 2 * heads + h)),
            pl.BlockSpec((None, 1, n), lambda h: (h, 0, 0)),
        ],
        out_specs=pl.BlockSpec((n, HEAD_DIM), lambda h: (0, h)),
        out_shape=jax.ShapeDtypeStruct((n, heads * HEAD_DIM), BF16),
        compiler_params=_params(("arbitrary",)),
        name=name,
    )(z, z, z, c_row)


def _split3(x):
    p1 = x.astype(BF16)
    r1 = x - p1.astype(F32)
    p2 = r1.astype(BF16)
    p3 = (r1 - p2.astype(F32)).astype(BF16)
    return p1, p2, p3


def _bias_columns(pieces, head):
    r = lax.broadcasted_iota(jnp.int32, (N_SPLIT * LANES, LANES), 0)
    c = lax.broadcasted_iota(jnp.int32, (N_SPLIT * LANES, LANES), 1)
    sel = jnp.where(r == c * LANES + head, 1.0, 0.0).astype(BF16)
    return jnp.dot(pieces, sel, preferred_element_type=F32).astype(BF16)


def _attn_main_body(q_ref, k_ref, v_ref, c_ref, kp_ref, vp_ref, cp_ref, o_ref,
                    kaug_ref, vt_ref, kpaug_ref, vpt_ref, qaug_ref, m_ref, l_ref, acc_ref, *, tq, hp,
                    scale):
    seq = q_ref.shape[0]
    npre = kp_ref.shape[0]
    group = pl.program_id(1)

    pieces = jnp.concatenate(_split3(c_ref[...] * (-LOG2E)), axis=1)
    cp = cp_ref[0:npre, :]
    pieces_p = jnp.concatenate(_split3((cp - cp[npre - 1:npre, :]) * (-LOG2E)), axis=1)
    for hh in range(hp):
        lanes = slice(hh * HEAD_DIM, (hh + 1) * HEAD_DIM)
        head = group * hp + hh
        kaug_ref[hh, :, 0:HEAD_DIM] = k_ref[:, lanes]
        kaug_ref[hh, :, HEAD_DIM:] = _bias_columns(pieces, head)
        vt_ref[hh] = v_ref[:, lanes].astype(F32).T.astype(BF16)
        kpaug_ref[hh, :, 0:HEAD_DIM] = kp_ref[:, lanes]
        kpaug_ref[hh, :, HEAD_DIM:] = _bias_columns(pieces_p, head)
        vp = jnp.concatenate([vp_ref[:, lanes].astype(F32),
                              jnp.zeros((LANES - npre, HEAD_DIM), F32)], axis=0)
        vpt_ref[hh] = vp.T.astype(BF16)

    ones = jnp.where(lax.broadcasted_iota(jnp.int32, (tq, HEAD_DIM), 1) < N_SPLIT, 1.0, 0.0).astype(BF16)
    key = lax.broadcasted_iota(jnp.int32, (tq, tq), 0)
    qry = lax.broadcasted_iota(jnp.int32, (tq, tq), 1)
    causal = key <= qry
    heads = range(hp)

    def scores_t(kaug, hh):
        return lax.dot_general(kaug, qaug_ref[hh], (((1,), (1,)), ((), ())),
                               preferred_element_type=F32)

    def softmax_update(hh, st):
        m = m_ref[hh]
        m_new = jnp.maximum(m, jnp.max(st, axis=0, keepdims=True))
        alpha = jnp.exp2(m - m_new)
        p = jnp.exp2(st - m_new)
        m_ref[hh] = m_new
        l_ref[hh] = alpha * l_ref[hh] + jnp.sum(p, axis=0, keepdims=True)
        return alpha, p.astype(BF16)

    def q_tile(qi, carry_unused):
        q0 = pl.multiple_of(qi * tq, tq)
        for hh in heads:
            lanes = slice(hh * HEAD_DIM, (hh + 1) * HEAD_DIM)
            q = (q_ref[pl.ds(q0, tq), lanes].astype(F32) * (scale * LOG2E)).astype(BF16)
            qaug_ref[hh] = jnp.concatenate([q, ones], axis=1)

        sts = [scores_t(kaug_ref[hh, pl.ds(q0, tq), :], hh) for hh in heads]
        ps = []
        for hh in heads:
            st = jnp.where(causal, sts[hh], MASK_VALUE)
            m = jnp.max(st, axis=0, keepdims=True)
            p = jnp.exp2(st - m)
            m_ref[hh] = m
            l_ref[hh] = jnp.sum(p, axis=0, keepdims=True)
            ps.append(p.astype(BF16))
        for hh in heads:
            acc_ref[hh] = jnp.dot(vt_ref[hh, :, pl.ds(q0, tq)], ps[hh], preferred_element_type=F32)

        def kv_tile(j, carry):
            k0 = pl.multiple_of(j * tq, tq)
            sts = [scores_t(kaug_ref[hh, pl.ds(k0, tq), :], hh) for hh in heads]
            aps = [softmax_update(hh, sts[hh]) for hh in heads]
            for hh in heads:
                alpha, p = aps[hh]
                acc_ref[hh] = alpha * acc_ref[hh] + jnp.dot(vt_ref[hh, :, pl.ds(k0, tq)], p,
                                                            preferred_element_type=F32)
            return carry

        lax.fori_loop(0, qi, kv_tile, 0)

        sts = [scores_t(kpaug_ref[hh], hh) for hh in heads]
        aps = [softmax_update(hh, sts[hh]) for hh in heads]
        for hh in heads:
            lanes = slice(hh * HEAD_DIM, (hh + 1) * HEAD_DIM)
            alpha, p = aps[hh]
            p_pad = jnp.concatenate([p, jnp.zeros((LANES - npre, tq), BF16)], axis=0)
            acc = alpha * acc_ref[hh] + jnp.dot(vpt_ref[hh], p_pad, preferred_element_type=F32)
            o_ref[pl.ds(q0, tq), lanes] = (acc / l_ref[hh]).T.astype(o_ref.dtype)
        return carry_unused

    lax.fori_loop(0, seq // tq, q_tile, 0)


def _attention_main(z, c_col, z_prefix, c_col_prefix, *, batches, heads, name):
    seq = z.shape[0] // batches
    npre = z_prefix.shape[0]
    tq = _pick_tile(seq, 256)
    hp = _pick_tile(heads, 8)
    groups = heads // hp
    width = hp * HEAD_DIM
    return pl.pallas_call(
        functools.partial(_attn_main_body, tq=tq, hp=hp, scale=HEAD_DIM ** -0.5),
        grid=(batches, groups),
        in_specs=[
            pl.BlockSpec((seq, width), lambda b, g: (b, g)),
            pl.BlockSpec((seq, width), lambda b, g: (b, groups + g)),
            pl.BlockSpec((seq, width), lambda b, g: (b, 2 * groups + g)),
            pl.BlockSpec((seq, LANES), lambda b, g: (b, 0)),
            pl.BlockSpec((npre, width), lambda b, g: (0, groups + g)),
            pl.BlockSpec((npre, width), lambda b, g: (0, 2 * groups + g)),
            pl.BlockSpec(c_col_prefix.shape, lambda b, g: (0, 0)),
        ],
        out_specs=pl.BlockSpec((seq, width), lambda b, g: (b, g)),
        out_shape=jax.ShapeDtypeStruct((batches * seq, heads * HEAD_DIM), BF16),
        scratch_shapes=[
            pltpu.VMEM((hp, seq, 2 * HEAD_DIM), BF16),
            pltpu.VMEM((hp, HEAD_DIM, seq), BF16),
            pltpu.VMEM((hp, npre, 2 * HEAD_DIM), BF16),
            pltpu.VMEM((hp, HEAD_DIM, LANES), BF16),
            pltpu.VMEM((hp, tq, 2 * HEAD_DIM), BF16),
            pltpu.VMEM((hp, 1, tq), F32),
            pltpu.VMEM((hp, 1, tq), F32),
            pltpu.VMEM((hp, HEAD_DIM, tq), F32),
        ],
        compiler_params=_params(("arbitrary", "arbitrary")),
        name=name,
    )(z, z, z, c_col, z_prefix, z_prefix, c_col_prefix)


def _pool_body(u_ref, *rest, gdim, pos_offset, tiles_per_batch, has_halo):
    if has_halo:
        uprev_ref, upre_ref, pw_ref, ps_ref, o_ref, ext_ref = rest
    else:
        pw_ref, ps_ref, o_ref, ext_ref = rest
    tm = u_ref.shape[0]
    if has_halo:
        first = (pl.program_id(0) % tiles_per_batch) == 0
        halo = jnp.where(first, upre_ref[...], uprev_ref[...]).astype(F32)
    else:
        halo = jnp.zeros((HALO_ROWS, u_ref.shape[1]), F32)
    ext_ref[0:HALO_ROWS, :] = halo
    ext_ref[HALO_ROWS:HALO_ROWS + tm, :] = u_ref[...].astype(F32)

    for g, w in enumerate(POOL_WINDOWS):
        cols = slice(g * gdim, (g + 1) * gdim)
        cur = ext_ref[HALO_ROWS:HALO_ROWS + tm, cols]
        wsum = cur
        for j in range(1, w):
            wsum = wsum + ext_ref[HALO_ROWS - j:HALO_ROWS - j + tm, cols]
        if pos_offset + 1 >= w:
            mean = wsum * (1.0 / w)
        else:
            pos = lax.broadcasted_iota(jnp.int32, (tm, 1), 0) + pos_offset
            mean = wsum / jnp.minimum(pos + 1, w).astype(F32)
        d = (mean - cur).astype(BF16)
        y = jnp.dot(d, pw_ref[g], preferred_element_type=F32) * ps_ref[:, cols]
        o_ref[:, cols] = y.astype(o_ref.dtype)


def _pool(z, pool_w, pool_scale, prefix_z, *, batches, ublock, name):
    groups, gdim, _ = pool_w.shape
    dp = groups * gdim
    rows = z.shape[0]
    seq = rows // batches
    has_halo = prefix_z is not None
    tm = _pick_tile(seq, 512)
    tiles_per_batch = seq // tm
    in_specs = [pl.BlockSpec((tm, dp), lambda i: (i, ublock))]
    args = [z]
    if has_halo:
        halo_blocks = tm // HALO_ROWS
        in_specs += [
            pl.BlockSpec((HALO_ROWS, dp), lambda i: (jnp.maximum(i * halo_blocks - 1, 0), ublock)),
            pl.BlockSpec((HALO_ROWS, dp), lambda i: (prefix_z.shape[0] // HALO_ROWS - 1, ublock)),
        ]
        args += [z, prefix_z]
        pos_offset = prefix_z.shape[0]
    else:
        pos_offset = 0
    in_specs += [
        pl.BlockSpec((groups, gdim, gdim), lambda i: (0, 0, 0)),
        pl.BlockSpec((1, dp), lambda i: (0, 0)),
    ]
    args += [pool_w, pool_scale]
    return pl.pallas_call(
        functools.partial(_pool_body, gdim=gdim, pos_offset=pos_offset,
                          tiles_per_batch=tiles_per_batch, has_halo=has_halo),
        grid=(rows // tm,),
        in_specs=in_specs,
        out_specs=pl.BlockSpec((tm, dp), lambda i: (i, 0)),
        out_shape=jax.ShapeDtypeStruct((rows, dp), BF16),
        scratch_shapes=[pltpu.VMEM((tm + HALO_ROWS, dp), F32)],
        compiler_params=_params(("arbitrary",)),
        name=name,
    )(*args)


def _mix_body(a_ref, p_ref, ga_ref, gp_ref, x_ref, wa_ref, wp_ref, wo_ref, g_ref, o_ref):
    ya = jnp.dot(a_ref[...], wa_ref[...], preferred_element_type=F32)
    yp = jnp.dot(p_ref[...], wp_ref[...], preferred_element_type=F32)
    m = (jax.nn.sigmoid(ga_ref[...].astype(F32)) * ya
         + jax.nn.sigmoid(gp_ref[...].astype(F32)) * yp)
    mo = jnp.dot(m.astype(BF16), wo_ref[...], preferred_element_type=F32)
    o_ref[...] = x_ref[...] + _rms(mo, g_ref[...])


def _mix(attn, pool, z, x, w_attn_o, w_pool_o, w_out, g_post, *, gate_block, name):
    rows, d = x.shape
    da = attn.shape[1]
    dp = pool.shape[1]
    tm = _pick_tile(rows, 256)
    resident = dict(pipeline_mode=pl.Buffered(1))
    return pl.pallas_call(
        _mix_body,
        grid=(rows // tm,),
        in_specs=[
            pl.BlockSpec((tm, da), lambda i: (i, 0)),
            pl.BlockSpec((tm, dp), lambda i: (i, 0)),
            pl.BlockSpec((tm, d), lambda i: (i, gate_block)),
            pl.BlockSpec((tm, d), lambda i: (i, gate_block + 1)),
            pl.BlockSpec((tm, d), lambda i: (i, 0)),
            pl.BlockSpec((da, d), lambda i: (0, 0), **resident),
            pl.BlockSpec((dp, d), lambda i: (0, 0), **resident),
            pl.BlockSpec((d, d), lambda i: (0, 0), **resident),
            pl.BlockSpec((1, d), lambda i: (0, 0)),
        ],
        out_specs=pl.BlockSpec((tm, d), lambda i: (i, 0)),
        out_shape=jax.ShapeDtypeStruct((rows, d), F32),
        compiler_params=_params(("arbitrary",)),
        name=name,
    )(attn, pool, z, z, x, w_attn_o, w_pool_o, w_out, g_post)


def _gelu_tanh(x):
    c = math.sqrt(2.0 / math.pi)
    v = (x * x).astype(F32) * (0.044715 * c) + c
    return (0.5 * x) * (1.0 + jnp.tanh(x * v.astype(x.dtype)))


def _ffn_body(ru_ref, rprev_ref, rpre_ref, rd_ref, gpre_ref, wg_ref, wv_ref, cwg_ref, cwv_ref,
              cbg_ref, cbv_ref, wd_ref, gpost_ref, o_ref, h_ref, gate_ref, act0_ref, act1_ref, *, nf,
              n_steps, tiles_per_batch):
    t = pl.program_id(0)
    tu = jnp.minimum(t, n_steps - 1)
    fu = tu % nf
    iu = tu // nf
    fd = jnp.maximum(t - 1, 0) % nf
    tm = o_ref.shape[0]

    @pl.when(t == 0)
    def _():
        act1_ref[...] = jnp.zeros(act1_ref.shape, act1_ref.dtype)

    @pl.when(jnp.logical_and(fu == 0, t < n_steps))
    def _():
        first = (iu % tiles_per_batch) == 0
        halo = jnp.where(first, rpre_ref[...], rprev_ref[...])
        h_ref[0:HALO_ROWS, :] = _rms(halo, gpre_ref[...]).astype(BF16)
        _rms_rows_to(h_ref, HALO_ROWS, ru_ref, gpre_ref, 128)

    @pl.when(fd == 0)
    def _():
        o_ref[...] = jnp.zeros(o_ref.shape, o_ref.dtype)

    def up_conv(w_ref, cw_ref, cb_ref):
        a = jnp.dot(h_ref[...], w_ref[...], preferred_element_type=F32)
        a0, a1, a2 = [(a if k == 0 else pltpu.roll(a, k, axis=0))[HALO_ROWS:, :].astype(BF16)
                      for k in range(3)]
        cw = cw_ref[...].astype(BF16)
        out = cb_ref[...].astype(BF16) + a2 * cw[0:1, :]
        out = out + a1 * cw[1:2, :]
        out = out + a0 * cw[2:3, :]
        return out

    def step(act_u_ref, act_d_ref):
        gate_ref[...] = up_conv(wg_ref, cwg_ref, cbg_ref)
        val = up_conv(wv_ref, cwv_ref, cbv_ref)
        act_u_ref[...] = _gelu_tanh(gate_ref[...]) * val
        o_ref[...] += jnp.dot(act_d_ref[...], wd_ref[...], preferred_element_type=F32)

    @pl.when(t % 2 == 0)
    def _():
        step(act0_ref, act1_ref)

    @pl.when(t % 2 == 1)
    def _():
        step(act1_ref, act0_ref)

    @pl.when(jnp.logical_and(fd == nf - 1, t > 0))
    def _():
        chunk = min(128, tm)

        def body(c, carry):
            r0 = pl.multiple_of(c * chunk, chunk)
            rows = pl.ds(r0, chunk)
            o_ref[rows, :] = rd_ref[rows, :] + _rms(o_ref[rows, :], gpost_ref[...])
            return carry

        lax.fori_loop(0, tm // chunk, body, 0)


def _conv_ffn(r, r_prefix, g_pre, w_up, conv_w, conv_b, w_down, g_post, *, batches, name):
    rows, d = r.shape
    dff = w_down.shape[0]
    seq = rows // batches
    tm = _pick_tile(seq, 512)
    tf = _pick_tile(dff, 512)
    nf = dff // tf
    assert nf >= 2
    n_steps = (rows // tm) * nf
    tiles_per_batch = seq // tm
    halo_blocks = tm // HALO_ROWS
    kw = conv_w.shape[0]

    def up_item(t):
        tu = jnp.minimum(t, n_steps - 1)
        return tu // nf, tu % nf

    def down_item(t):
        td = jnp.maximum(t - 1, 0)
        return td // nf, td % nf

    return pl.pallas_call(
        functools.partial(_ffn_body, nf=nf, n_steps=n_steps, tiles_per_batch=tiles_per_batch),
        grid=(n_steps + 1,),
        in_specs=[
            pl.BlockSpec((tm, d), lambda t: (up_item(t)[0], 0)),
            pl.BlockSpec((HALO_ROWS, d), lambda t: (jnp.maximum(up_item(t)[0] * halo_blocks - 1, 0), 0)),
            pl.BlockSpec((HALO_ROWS, d), lambda t: (r_prefix.shape[0] // HALO_ROWS - 1, 0)),
            pl.BlockSpec((tm, d), lambda t: (down_item(t)[0], 0)),
            pl.BlockSpec((1, d), lambda t: (0, 0)),
            pl.BlockSpec((d, tf), lambda t: (0, up_item(t)[1])),
            pl.BlockSpec((d, tf), lambda t: (0, nf + up_item(t)[1])),
            pl.BlockSpec((kw, tf), lambda t: (0, up_item(t)[1])),
            pl.BlockSpec((kw, tf), lambda t: (0, nf + up_item(t)[1])),
            pl.BlockSpec((1, tf), lambda t: (0, up_item(t)[1])),
            pl.BlockSpec((1, tf), lambda t: (0, nf + up_item(t)[1])),
            pl.BlockSpec((tf, d), lambda t: (down_item(t)[1], 0)),
            pl.BlockSpec((1, d), lambda t: (0, 0)),
        ],
        out_specs=pl.BlockSpec((tm, d), lambda t: (down_item(t)[0], 0)),
        out_shape=jax.ShapeDtypeStruct((rows, d), F32),
        scratch_shapes=[
            pltpu.VMEM((HALO_ROWS + tm, d), BF16),
            pltpu.VMEM((tm, tf), BF16),
            pltpu.VMEM((tm, tf), BF16),
            pltpu.VMEM((tm, tf), BF16),
        ],
        compiler_params=_params(("arbitrary",)),
        name=name,
    )(r, r, r_prefix, r, g_pre, w_up, w_up, conv_w, conv_w, conv_b, conv_b, w_down, g_post)


def kernel(x, meta_tokens, mix_pre_g, w_in, b_in, w_attn_o, pool_w, pool_scale, w_pool_o, w_out,
           mix_post_g, ffn_pre_g, w_ffn_up, ffn_conv_w, ffn_conv_b, w_ffn_down, ffn_post_g):
    batches, seq, d = x.shape
    n_meta = meta_tokens.shape[0]
    assert w_in.shape[0] == 1, "single-layer block"
    assert n_meta == HALO_ROWS and n_meta >= max(POOL_WINDOWS) and ffn_conv_w.shape[1] <= HALO_ROWS
    d_attn = w_attn_o.shape[1]
    d_pool = w_pool_o.shape[1]
    heads = d_attn // HEAD_DIM
    assert d_attn == d_pool and 2 * d_attn == d
    assert w_in.shape[2] == 3 * d_attn + heads + d_pool + 2 * d

    f0 = 3 * d_attn
    w = w_in[0]
    b = b_in[0]
    w_main = _stage_in_weights(w, skip_start=f0, skip=heads, name="stage_w_in")
    b_main = jnp.concatenate([b[:f0], b[f0 + heads:]])[None, :]
    w_f = jnp.pad(w[:, f0:f0 + heads], ((0, 0), (0, LANES - heads))).astype(BF16)
    b_f = jnp.pad(b[f0:f0 + heads], (0, LANES - heads))[None, :]
    wa = w_attn_o[0].astype(BF16)
    wp = w_pool_o[0].astype(BF16)
    wo = w_out[0].astype(BF16)
    pw = pool_w[0].astype(BF16)
    w_up = w_ffn_up[0].astype(BF16)
    w_down = w_ffn_down[0].astype(BF16)
    ublock = f0 // d_pool
    gate_block = (f0 + d_pool) // d

    def mixer(rows_in, prefix, nb, tag):
        n, _ = rows_in.shape
        s = n // nb
        z, flog = _norm_matmul(rows_in, mix_pre_g, w_main, b_main, w_f, b_f,
                               tm=_pick_tile(s, 1024), tn=_pick_tile(w_main.shape[1], 1024),
                               name=f"inproj_{tag}")
        if prefix is None:
            flog = jnp.pad(flog, ((0, LANES - n), (0, 0)))
            c_col, c_row = _forget_cumsum(flog, batches=nb, heads=heads, name=f"cumsum_{tag}")
            attn = _attention_prefix(z, c_row[0, :, :n].reshape(heads, 1, n), heads=heads,
                                     name=f"attention_{tag}")
        else:
            c_col, _ = _forget_cumsum(flog, batches=nb, heads=heads, name=f"cumsum_{tag}")
            attn = _attention_main(z, c_col, prefix[0], prefix[1], batches=nb, heads=heads,
                                   name=f"attention_{tag}")
        pool = _pool(z, pw, pool_scale, None if prefix is None else prefix[0],
                     batches=nb, ublock=ublock, name=f"pool_{tag}")
        r = _mix(attn, pool, z, rows_in, wa, wp, wo, mix_post_g,
                 gate_block=gate_block, name=f"mix_{tag}")
        return r, z, c_col

    r_meta, z_meta, c_meta = mixer(meta_tokens.astype(x.dtype), None, 1, "prefix")
    r, _, _ = mixer(x.reshape(batches * seq, d), (z_meta, c_meta), batches, "main")
    out = _conv_ffn(r, r_meta, ffn_pre_g, w_up, ffn_conv_w[0], ffn_conv_b, w_down, ffn_post_g,
                    batches=batches, name="conv_ffn")
    return out.reshape(batches, seq, d)
```

```python
import functools
import math

import jax
import jax.numpy as jnp
from jax import lax
from jax.experimental import pallas as pl
from jax.experimental.pallas import tpu as pltpu

HEAD_DIM = 128
POOL_WINDOWS = (2, 4, 8, 16)
HALO_ROWS = 16
EPS = 1e-6
MASK_VALUE = -1e30
F32 = jnp.float32
BF16 = jnp.bfloat16
LANES = 128
LOG2E = math.log2(math.e)
N_SPLIT = 3
SUM_ROWS = 16
VMEM_LIMIT_BYTES = 56 * 1024 * 1024


def _pick_tile(n, preferred):
    t = min(preferred, n)
    while n % t:
        t //= 2
    return t


def _params(semantics, vmem=VMEM_LIMIT_BYTES):
    return pltpu.CompilerParams(dimension_semantics=semantics, vmem_limit_bytes=vmem)


def _rms(x, g):
    ms = jnp.mean(x * x, axis=-1, keepdims=True)
    return x * lax.rsqrt(ms + EPS) * g


def _rms_rows_to(dst_ref, dst_row0, src_ref, g_ref, chunk):
    rows = src_ref.shape[0]
    chunk = min(chunk, rows)

    def body(c, carry):
        r0 = pl.multiple_of(c * chunk, chunk)
        x = src_ref[pl.ds(r0, chunk), :]
        dst_ref[pl.ds(dst_row0 + r0, chunk), :] = _rms(x, g_ref[...]).astype(dst_ref.dtype)
        return carry

    lax.fori_loop(0, rows // chunk, body, 0)


def _norm_matmul_body(x_ref, g_ref, w_ref, b_ref, ws_ref, bs_ref, o_ref, os_ref, h_ref):
    @pl.when(pl.program_id(1) == 0)
    def _():
        _rms_rows_to(h_ref, 0, x_ref, g_ref, 128)
        os_ref[...] = jnp.dot(h_ref[...], ws_ref[...], preferred_element_type=F32) + bs_ref[...]

    acc = jnp.dot(h_ref[...], w_ref[...], preferred_element_type=F32)
    o_ref[...] = (acc + b_ref[...]).astype(o_ref.dtype)


def _norm_matmul(x, g, w, b, w_side, b_side, *, tm, tn, name):
    m, d = x.shape
    n = w.shape[1]
    ns = w_side.shape[1]
    return pl.pallas_call(
        _norm_matmul_body,
        grid=(m // tm, n // tn),
        in_specs=[
            pl.BlockSpec((tm, d), lambda i, j: (i, 0)),
            pl.BlockSpec((1, d), lambda i, j: (0, 0)),
            pl.BlockSpec((d, tn), lambda i, j: (0, j)),
            pl.BlockSpec((1, tn), lambda i, j: (0, j)),
            pl.BlockSpec((d, ns), lambda i, j: (0, 0)),
            pl.BlockSpec((1, ns), lambda i, j: (0, 0)),
        ],
        out_specs=[
            pl.BlockSpec((tm, tn), lambda i, j: (i, j)),
            pl.BlockSpec((tm, ns), lambda i, j: (i, 0)),
        ],
        out_shape=[
            jax.ShapeDtypeStruct((m, n), BF16),
            jax.ShapeDtypeStruct((m, ns), F32),
        ],
        scratch_shapes=[pltpu.VMEM((tm, d), BF16)],
        compiler_params=_params(("arbitrary", "arbitrary")),
        name=name,
    )(x, g, w, b, w_side, b_side)


def _stage_body(a_ref, b_ref, o_ref, *, first_shifted, shift):
    j = pl.program_id(1)

    @pl.when(j < first_shifted)
    def _():
        o_ref[...] = a_ref[...].astype(o_ref.dtype)

    @pl.when(j >= first_shifted)
    def _():
        both = jnp.concatenate([a_ref[...], b_ref[...]], axis=1).astype(F32)
        width = both.shape[1]
        o_ref[...] = pltpu.roll(both, width - shift, axis=1)[:, :o_ref.shape[1]].astype(o_ref.dtype)


def _stage_in_weights(w, *, skip_start, skip, name):
    d, n = w.shape
    tn = _pick_tile(skip_start, 1024)
    n_out = n - skip
    assert skip < LANES and n_out % tn == 0 and skip_start % tn == 0
    tr = _pick_tile(d, 1024)
    edge = tn // LANES
    return pl.pallas_call(
        functools.partial(_stage_body, first_shifted=skip_start // tn, shift=skip),
        grid=(d // tr, n_out // tn),
        in_specs=[
            pl.BlockSpec((tr, tn), lambda i, j: (i, j)),
            pl.BlockSpec((tr, LANES), lambda i, j: (i, (j + 1) * edge)),
        ],
        out_specs=pl.BlockSpec((tr, tn), lambda i, j: (i, j)),
        out_shape=jax.ShapeDtypeStruct((d, n_out), BF16),
        compiler_params=_params(("arbitrary", "arbitrary")),
        name=name,
    )(w, w)


def _cum_body(f_ref, col_ref, row_ref, *, heads):
    x = f_ref[...]
    rows = x.shape[0]
    c = jnp.minimum(x, 0.0) - jnp.log1p(jnp.exp(-jnp.abs(x)))
    t = lax.broadcasted_iota(jnp.int32, c.shape, 0)
    shift = 1
    while shift < rows:
        c = c + jnp.where(t >= shift, pltpu.roll(c, shift, axis=0), 0.0)
        shift *= 2
    col_ref[...] = c
    row_ref[...] = c.T[:heads, :]


def _forget_cumsum(flog, *, batches, heads, name):
    rows = flog.shape[0] // batches
    return pl.pallas_call(
        functools.partial(_cum_body, heads=heads),
        grid=(batches,),
        in_specs=[pl.BlockSpec((rows, flog.shape[1]), lambda b: (b, 0))],
        out_specs=[
            pl.BlockSpec((rows, flog.shape[1]), lambda b: (b, 0)),
            pl.BlockSpec((None, heads, rows), lambda b: (b, 0, 0)),
        ],
        out_shape=[
            jax.ShapeDtypeStruct(flog.shape, F32),
            jax.ShapeDtypeStruct((batches, heads, rows), F32),
        ],
        compiler_params=_params(("arbitrary",)),
        name=name,
    )(flog)


def _attn_prefix_body(q_ref, k_ref, v_ref, c_ref, o_ref, *, scale):
    n = q_ref.shape[0]
    q = (q_ref[...].astype(F32) * scale).astype(BF16)
    s = lax.dot_general(q, k_ref[...], (((1,), (1,)), ((), ())), preferred_element_type=F32)
    s = s - c_ref[...]
    row = lax.broadcasted_iota(jnp.int32, (n, n), 0)
    col = lax.broadcasted_iota(jnp.int32, (n, n), 1)
    s = jnp.where(row >= col, s, MASK_VALUE)
    p = jnp.exp(s - jnp.max(s, axis=-1, keepdims=True))
    l = jnp.sum(p, axis=-1, keepdims=True)
    acc = jnp.dot(p.astype(BF16), v_ref[...], preferred_element_type=F32)
    o_ref[...] = (acc / l).astype(o_ref.dtype)


def _attention_prefix(z, c_row, *, heads, name):
    n = z.shape[0]
    return pl.pallas_call(
        functools.partial(_attn_prefix_body, scale=HEAD_DIM ** -0.5),
        grid=(heads,),
        in_specs=[
            pl.BlockSpec((n, HEAD_DIM), lambda h: (0, h)),
            pl.BlockSpec((n, HEAD_DIM), lambda h: (0, heads + h)),
            pl.BlockSpec((n, HEAD_DIM), lambda h: (0, 2 * heads + h)),
            pl.BlockSpec((None, 1, n), lambda h: (h, 0, 0)),
        ],
        out_specs=pl.BlockSpec((n, HEAD_DIM), lambda h: (0, h)),
        out_shape=jax.ShapeDtypeStruct((n, heads * HEAD_DIM), BF16),
        compiler_params=_params(("arbitrary",)),
        name=name,
    )(z, z, z, c_row)


def _split3(x):
    p1 = x.astype(BF16)
    r1 = x - p1.astype(F32)
    p2 = r1.astype(BF16)
    p3 = (r1 - p2.astype(F32)).astype(BF16)
    return p1, p2, p3


def _bias_columns(pieces, head):
    r = lax.broadcasted_iota(jnp.int32, (N_SPLIT * LANES, LANES), 0)
    c = lax.broadcasted_iota(jnp.int32, (N_SPLIT * LANES, LANES), 1)
    sel = jnp.where(r == c * LANES + head, 1.0, 0.0).astype(BF16)
    return jnp.dot(pieces, sel, preferred_element_type=F32).astype(BF16)


def _attn_main_body(q_ref, k_ref, v_ref, c_ref, kp_ref, vp_ref, cp_ref, o_ref,
                    kaug_ref, vt_ref, kpaug_ref, vpt_ref, qaug_ref, m_ref, acc_ref, alpha_ref, p_ref,
                    *, tq, hp, scale):
    seq = q_ref.shape[0]
    npre = kp_ref.shape[0]
    group = pl.program_id(1)

    pieces = jnp.concatenate(_split3(c_ref[...] * (-LOG2E)), axis=1)
    cp = cp_ref[0:npre, :]
    pieces_p = jnp.concatenate(_split3((cp - cp[npre - 1:npre, :]) * (-LOG2E)), axis=1)

    def sum_rows(n):
        r = lax.broadcasted_iota(jnp.int32, (SUM_ROWS, n), 0)
        return jnp.where(r == 0, 1.0, 0.0).astype(BF16)

    for hh in range(hp):
        lanes = slice(hh * HEAD_DIM, (hh + 1) * HEAD_DIM)
        head = group * hp + hh
        kaug_ref[hh, :, 0:HEAD_DIM] = k_ref[:, lanes]
        kaug_ref[hh, :, HEAD_DIM:] = _bias_columns(pieces, head)
        vt_ref[hh, 0:HEAD_DIM, :] = v_ref[:, lanes].astype(F32).T.astype(BF16)
        vt_ref[hh, HEAD_DIM:, :] = sum_rows(seq)
        kpaug_ref[hh, :, 0:HEAD_DIM] = kp_ref[:, lanes]
        kpaug_ref[hh, :, HEAD_DIM:] = _bias_columns(pieces_p, head)
        vp = jnp.concatenate([vp_ref[:, lanes].astype(F32),
                              jnp.zeros((LANES - npre, HEAD_DIM), F32)], axis=0)
        vpt_ref[hh, 0:HEAD_DIM, :] = vp.T.astype(BF16)
        vpt_ref[hh, HEAD_DIM:, :] = sum_rows(LANES)

    ones = jnp.where(lax.broadcasted_iota(jnp.int32, (tq, HEAD_DIM), 1) < N_SPLIT, 1.0, 0.0).astype(BF16)
    key = lax.broadcasted_iota(jnp.int32, (tq, tq), 0)
    qry = lax.broadcasted_iota(jnp.int32, (tq, tq), 1)
    causal = key <= qry
    heads = range(hp)

    def scores_t(kaug, hh):
        return lax.dot_general(kaug, qaug_ref[hh], (((1,), (1,)), ((), ())),
                               preferred_element_type=F32)

    def softmax_update(hh, st):
        m = m_ref[hh]
        m_new = jnp.maximum(m, jnp.max(st, axis=0, keepdims=True))
        alpha = jnp.exp2(m - m_new)
        m_ref[hh] = m_new
        return alpha, jnp.exp2((st - m_new).astype(BF16))

    def value_update(hh, k_pend):
        vt = vt_ref[hh, :, pl.ds(k_pend, tq)]
        acc_ref[hh] = alpha_ref[hh] * acc_ref[hh] + jnp.dot(vt, p_ref[hh], preferred_element_type=F32)

    def q_tile(qi, carry_unused):
        q0 = pl.multiple_of(qi * tq, tq)
        for hh in heads:
            lanes = slice(hh * HEAD_DIM, (hh + 1) * HEAD_DIM)
            q = (q_ref[pl.ds(q0, tq), lanes].astype(F32) * (scale * LOG2E)).astype(BF16)
            qaug_ref[hh] = jnp.concatenate([q, ones], axis=1)

        sts = [scores_t(kaug_ref[hh, pl.ds(q0, tq), :], hh) for hh in heads]
        for hh in heads:
            st = jnp.where(causal, sts[hh], MASK_VALUE)
            m = jnp.max(st, axis=0, keepdims=True)
            m_ref[hh] = m
            p_ref[hh] = jnp.exp2((st - m).astype(BF16))
            alpha_ref[hh] = jnp.ones_like(m)
            acc_ref[hh] = jnp.zeros(acc_ref.shape[1:], F32)

        def kv_tile(j, k_pend):
            k0 = pl.multiple_of(j * tq, tq)
            sts = [scores_t(kaug_ref[hh, pl.ds(k0, tq), :], hh) for hh in heads]
            for hh in heads:
                value_update(hh, pl.multiple_of(k_pend, tq))
            for hh in heads:
                alpha_ref[hh], p_ref[hh] = softmax_update(hh, sts[hh])
            return k0

        k_pend = lax.fori_loop(0, qi, kv_tile, q0)

        sts = [scores_t(kpaug_ref[hh], hh) for hh in heads]
        for hh in heads:
            value_update(hh, pl.multiple_of(k_pend, tq))
        aps = [softmax_update(hh, sts[hh]) for hh in heads]
        for hh in heads:
            lanes = slice(hh * HEAD_DIM, (hh + 1) * HEAD_DIM)
            alpha, p = aps[hh]
            p_pad = jnp.concatenate([p, jnp.zeros((LANES - npre, tq), BF16)], axis=0)
            acc = alpha * acc_ref[hh] + jnp.dot(vpt_ref[hh], p_pad, preferred_element_type=F32)
            out = acc[0:HEAD_DIM, :] / acc[HEAD_DIM:HEAD_DIM + 1, :]
            o_ref[pl.ds(q0, tq), lanes] = out.T.astype(o_ref.dtype)
        return carry_unused

    lax.fori_loop(0, seq // tq, q_tile, 0)


def _attention_main(z, c_col, z_prefix, c_col_prefix, *, batches, heads, name):
    seq = z.shape[0] // batches
    npre = z_prefix.shape[0]
    tq = _pick_tile(seq, 256)
    hp = _pick_tile(heads, 8)
    groups = heads // hp
    width = hp * HEAD_DIM
    return pl.pallas_call(
        functools.partial(_attn_main_body, tq=tq, hp=hp, scale=HEAD_DIM ** -0.5),
        grid=(batches, groups),
        in_specs=[
            pl.BlockSpec((seq, width), lambda b, g: (b, g)),
            pl.BlockSpec((seq, width), lambda b, g: (b, groups + g)),
            pl.BlockSpec((seq, width), lambda b, g: (b, 2 * groups + g)),
            pl.BlockSpec((seq, LANES), lambda b, g: (b, 0)),
            pl.BlockSpec((npre, width), lambda b, g: (0, groups + g)),
            pl.BlockSpec((npre, width), lambda b, g: (0, 2 * groups + g)),
            pl.BlockSpec(c_col_prefix.shape, lambda b, g: (0, 0)),
        ],
        out_specs=pl.BlockSpec((seq, width), lambda b, g: (b, g)),
        out_shape=jax.ShapeDtypeStruct((batches * seq, heads * HEAD_DIM), BF16),
        scratch_shapes=[
            pltpu.VMEM((hp, seq, 2 * HEAD_DIM), BF16),
            pltpu.VMEM((hp, HEAD_DIM + SUM_ROWS, seq), BF16),
            pltpu.VMEM((hp, npre, 2 * HEAD_DIM), BF16),
            pltpu.VMEM((hp, HEAD_DIM + SUM_ROWS, LANES), BF16),
            pltpu.VMEM((hp, tq, 2 * HEAD_DIM), BF16),
            pltpu.VMEM((hp, 1, tq), F32),
            pltpu.VMEM((hp, HEAD_DIM + SUM_ROWS, tq), F32),
            pltpu.VMEM((hp, 1, tq), F32),
            pltpu.VMEM((hp, tq, tq), BF16),
        ],
        compiler_params=_params(("arbitrary", "arbitrary")),
        name=name,
    )(z, z, z, c_col, z_prefix, z_prefix, c_col_prefix)


def _pool_body(u_ref, *rest, gdim, pos_offset, tiles_per_batch, has_halo):
    if has_halo:
        uprev_ref, upre_ref, pw_ref, ps_ref, o_ref, ext_ref = rest
    else:
        pw_ref, ps_ref, o_ref, ext_ref = rest
    tm = u_ref.shape[0]
    if has_halo:
        first = (pl.program_id(0) % tiles_per_batch) == 0
        halo = jnp.where(first, upre_ref[...], uprev_ref[...]).astype(F32)
    else:
        halo = jnp.zeros((HALO_ROWS, u_ref.shape[1]), F32)
    ext_ref[0:HALO_ROWS, :] = halo
    ext_ref[HALO_ROWS:HALO_ROWS + tm, :] = u_ref[...].astype(F32)

    for g, w in enumerate(POOL_WINDOWS):
        cols = slice(g * gdim, (g + 1) * gdim)
        cur = ext_ref[HALO_ROWS:HALO_ROWS + tm, cols]
        wsum = cur
        for j in range(1, w):
            wsum = wsum + ext_ref[HALO_ROWS - j:HALO_ROWS - j + tm, cols]
        if pos_offset + 1 >= w:
            mean = wsum * (1.0 / w)
        else:
            pos = lax.broadcasted_iota(jnp.int32, (tm, 1), 0) + pos_offset
            mean = wsum / jnp.minimum(pos + 1, w).astype(F32)
        d = (mean - cur).astype(BF16)
        y = jnp.dot(d, pw_ref[g], preferred_element_type=F32) * ps_ref[:, cols]
        o_ref[:, cols] = y.astype(o_ref.dtype)


def _pool(z, pool_w, pool_scale, prefix_z, *, batches, ublock, name):
    groups, gdim, _ = pool_w.shape
    dp = groups * gdim
    rows = z.shape[0]
    seq = rows // batches
    has_halo = prefix_z is not None
    tm = _pick_tile(seq, 512)
    tiles_per_batch = seq // tm
    in_specs = [pl.BlockSpec((tm, dp), lambda i: (i, ublock))]
    args = [z]
    if has_halo:
        halo_blocks = tm // HALO_ROWS
        in_specs += [
            pl.BlockSpec((HALO_ROWS, dp), lambda i: (jnp.maximum(i * halo_blocks - 1, 0), ublock)),
            pl.BlockSpec((HALO_ROWS, dp), lambda i: (prefix_z.shape[0] // HALO_ROWS - 1, ublock)),
        ]
        args += [z, prefix_z]
        pos_offset = prefix_z.shape[0]
    else:
        pos_offset = 0
    in_specs += [
        pl.BlockSpec((groups, gdim, gdim), lambda i: (0, 0, 0)),
        pl.BlockSpec((1, dp), lambda i: (0, 0)),
    ]
    args += [pool_w, pool_scale]
    return pl.pallas_call(
        functools.partial(_pool_body, gdim=gdim, pos_offset=pos_offset,
                          tiles_per_batch=tiles_per_batch, has_halo=has_halo),
        grid=(rows // tm,),
        in_specs=in_specs,
        out_specs=pl.BlockSpec((tm, dp), lambda i: (i, 0)),
        out_shape=jax.ShapeDtypeStruct((rows, dp), BF16),
        scratch_shapes=[pltpu.VMEM((tm + HALO_ROWS, dp), F32)],
        compiler_params=_params(("arbitrary",)),
        name=name,
    )(*args)


def _mix_body(a_ref, p_ref, ga_ref, gp_ref, x_ref, wa_ref, wp_ref, wo_ref, g_ref, o_ref):
    ya = jnp.dot(a_ref[...], wa_ref[...], preferred_element_type=F32)
    yp = jnp.dot(p_ref[...], wp_ref[...], preferred_element_type=F32)
    m = (jax.nn.sigmoid(ga_ref[...].astype(F32)) * ya
         + jax.nn.sigmoid(gp_ref[...].astype(F32)) * yp)
    mo = jnp.dot(m.astype(BF16), wo_ref[...], preferred_element_type=F32)
    o_ref[...] = x_ref[...] + _rms(mo, g_ref[...])


def _mix(attn, pool, z, x, w_attn_o, w_pool_o, w_out, g_post, *, gate_block, name):
    rows, d = x.shape
    da = attn.shape[1]
    dp = pool.shape[1]
    tm = _pick_tile(rows, 256)
    resident = dict(pipeline_mode=pl.Buffered(1))
    return pl.pallas_call(
        _mix_body,
        grid=(rows // tm,),
        in_specs=[
            pl.BlockSpec((tm, da), lambda i: (i, 0)),
            pl.BlockSpec((tm, dp), lambda i: (i, 0)),
            pl.BlockSpec((tm, d), lambda i: (i, gate_block)),
            pl.BlockSpec((tm, d), lambda i: (i, gate_block + 1)),
            pl.BlockSpec((tm, d), lambda i: (i, 0)),
            pl.BlockSpec((da, d), lambda i: (0, 0), **resident),
            pl.BlockSpec((dp, d), lambda i: (0, 0), **resident),
            pl.BlockSpec((d, d), lambda i: (0, 0), **resident),
            pl.BlockSpec((1, d), lambda i: (0, 0)),
        ],
        out_specs=pl.BlockSpec((tm, d), lambda i: (i, 0)),
        out_shape=jax.ShapeDtypeStruct((rows, d), F32),
        compiler_params=_params(("arbitrary",)),
        name=name,
    )(attn, pool, z, z, x, w_attn_o, w_pool_o, w_out, g_post)


def _gelu_tanh(x):
    c = math.sqrt(2.0 / math.pi)
    v = (x * x).astype(F32) * (0.044715 * c) + c
    return (0.5 * x) * (1.0 + jnp.tanh(x * v.astype(x.dtype)))


def _ffn_body(ru_ref, rprev_ref, rpre_ref, rd_ref, gpre_ref, wg_ref, wv_ref, cwg_ref, cwv_ref,
              cbg_ref, cbv_ref, wd_ref, gpost_ref, o_ref, h_ref, gate_ref, act0_ref, act1_ref, *, nf,
              n_steps, tiles_per_batch):
    t = pl.program_id(0)
    tu = jnp.minimum(t, n_steps - 1)
    fu = tu % nf
    iu = tu // nf
    fd = jnp.maximum(t - 1, 0) % nf
    tm = o_ref.shape[0]

    @pl.when(t == 0)
    def _():
        act1_ref[...] = jnp.zeros(act1_ref.shape, act1_ref.dtype)

    @pl.when(jnp.logical_and(fu == 0, t < n_steps))
    def _():
        first = (iu % tiles_per_batch) == 0
        halo = jnp.where(first, rpre_ref[...], rprev_ref[...])
        h_ref[0:HALO_ROWS, :] = _rms(halo, gpre_ref[...]).astype(BF16)
        _rms_rows_to(h_ref, HALO_ROWS, ru_ref, gpre_ref, 128)

    @pl.when(fd == 0)
    def _():
        o_ref[...] = jnp.zeros(o_ref.shape, o_ref.dtype)

    def up_conv(w_ref, cw_ref, cb_ref):
        a = jnp.dot(h_ref[...], w_ref[...], preferred_element_type=F32)
        a0, a1, a2 = [(a if k == 0 else pltpu.roll(a, k, axis=0))[HALO_ROWS:, :].astype(BF16)
                      for k in range(3)]
        cw = cw_ref[...].astype(BF16)
        out = cb_ref[...].astype(BF16) + a2 * cw[0:1, :]
        out = out + a1 * cw[1:2, :]
        out = out + a0 * cw[2:3, :]
        return out

    def step(act_u_ref, act_d_ref):
        gate_ref[...] = up_conv(wg_ref, cwg_ref, cbg_ref)
        val = up_conv(wv_ref, cwv_ref, cbv_ref)
        act_u_ref[...] = _gelu_tanh(gate_ref[...]) * val
        o_ref[...] += jnp.dot(act_d_ref[...], wd_ref[...], preferred_element_type=F32)

    @pl.when(t % 2 == 0)
    def _():
        step(act0_ref, act1_ref)

    @pl.when(t % 2 == 1)
    def _():
        step(act1_ref, act0_ref)

    @pl.when(jnp.logical_and(fd == nf - 1, t > 0))
    def _():
        chunk = min(128, tm)

        def body(c, carry):
            r0 = pl.multiple_of(c * chunk, chunk)
            rows = pl.ds(r0, chunk)
            o_ref[rows, :] = rd_ref[rows, :] + _rms(o_ref[rows, :], gpost_ref[...])
            return carry

        lax.fori_loop(0, tm // chunk, body, 0)


def _conv_ffn(r, r_prefix, g_pre, w_up, conv_w, conv_b, w_down, g_post, *, batches, name):
    rows, d = r.shape
    dff = w_down.shape[0]
    seq = rows // batches
    tm = _pick_tile(seq, 512)
    tf = _pick_tile(dff, 512)
    nf = dff // tf
    assert nf >= 2
    n_steps = (rows // tm) * nf
    tiles_per_batch = seq // tm
    halo_blocks = tm // HALO_ROWS
    kw = conv_w.shape[0]

    def up_item(t):
        tu = jnp.minimum(t, n_steps - 1)
        return tu // nf, tu % nf

    def down_item(t):
        td = jnp.maximum(t - 1, 0)
        return td // nf, td % nf

    return pl.pallas_call(
        functools.partial(_ffn_body, nf=nf, n_steps=n_steps, tiles_per_batch=tiles_per_batch),
        grid=(n_steps + 1,),
        in_specs=[
            pl.BlockSpec((tm, d), lambda t: (up_item(t)[0], 0)),
            pl.BlockSpec((HALO_ROWS, d), lambda t: (jnp.maximum(up_item(t)[0] * halo_blocks - 1, 0), 0)),
            pl.BlockSpec((HALO_ROWS, d), lambda t: (r_prefix.shape[0] // HALO_ROWS - 1, 0)),
            pl.BlockSpec((tm, d), lambda t: (down_item(t)[0], 0)),
            pl.BlockSpec((1, d), lambda t: (0, 0)),
            pl.BlockSpec((d, tf), lambda t: (0, up_item(t)[1])),
            pl.BlockSpec((d, tf), lambda t: (0, nf + up_item(t)[1])),
            pl.BlockSpec((kw, tf), lambda t: (0, up_item(t)[1])),
            pl.BlockSpec((kw, tf), lambda t: (0, nf + up_item(t)[1])),
            pl.BlockSpec((1, tf), lambda t: (0, up_item(t)[1])),
            pl.BlockSpec((1, tf), lambda t: (0, nf + up_item(t)[1])),
            pl.BlockSpec((tf, d), lambda t: (down_item(t)[1], 0)),
            pl.BlockSpec((1, d), lambda t: (0, 0)),
        ],
        out_specs=pl.BlockSpec((tm, d), lambda t: (down_item(t)[0], 0)),
        out_shape=jax.ShapeDtypeStruct((rows, d), F32),
        scratch_shapes=[
            pltpu.VMEM((HALO_ROWS + tm, d), BF16),
            pltpu.VMEM((tm, tf), BF16),
            pltpu.VMEM((tm, tf), BF16),
            pltpu.VMEM((tm, tf), BF16),
        ],
        compiler_params=_params(("arbitrary",)),
        name=name,
    )(r, r, r_prefix, r, g_pre, w_up, w_up, conv_w, conv_w, conv_b, conv_b, w_down, g_post)


def kernel(x, meta_tokens, mix_pre_g, w_in, b_in, w_attn_o, pool_w, pool_scale, w_pool_o, w_out,
           mix_post_g, ffn_pre_g, w_ffn_up, ffn_conv_w, ffn_conv_b, w_ffn_down, ffn_post_g):
    batches, seq, d = x.shape
    n_meta = meta_tokens.shape[0]
    assert w_in.shape[0] == 1, "single-layer block"
    assert n_meta == HALO_ROWS and n_meta >= max(POOL_WINDOWS) and ffn_conv_w.shape[1] <= HALO_ROWS
    d_attn = w_attn_o.shape[1]
    d_pool = w_pool_o.shape[1]
    heads = d_attn // HEAD_DIM
    assert d_attn == d_pool and 2 * d_attn == d
    assert w_in.shape[2] == 3 * d_attn + heads + d_pool + 2 * d

    f0 = 3 * d_attn
    w = w_in[0]
    b = b_in[0]
    w_main = _stage_in_weights(w.astype(BF16), skip_start=f0, skip=heads, name="stage_w_in")
    b_main = jnp.concatenate([b[:f0], b[f0 + heads:]])[None, :]
    w_f = jnp.pad(w[:, f0:f0 + heads], ((0, 0), (0, LANES - heads))).astype(BF16)
    b_f = jnp.pad(b[f0:f0 + heads], (0, LANES - heads))[None, :]
    wa = w_attn_o[0].astype(BF16)
    wp = w_pool_o[0].astype(BF16)
    wo = w_out[0].astype(BF16)
    pw = pool_w[0].astype(BF16)
    w_up = w_ffn_up[0].astype(BF16)
    w_down = w_ffn_down[0].astype(BF16)
    ublock = f0 // d_pool
    gate_block = (f0 + d_pool) // d

    def mixer(rows_in, prefix, nb, tag):
        n, _ = rows_in.shape
        s = n // nb
        z, flog = _norm_matmul(rows_in, mix_pre_g, w_main, b_main, w_f, b_f,
                               tm=_pick_tile(s, 1024), tn=_pick_tile(w_main.shape[1], 1024),
                               name=f"inproj_{tag}")
        if prefix is None:
            flog = jnp.pad(flog, ((0, LANES - n), (0, 0)))
            c_col, c_row = _forget_cumsum(flog, batches=nb, heads=heads, name=f"cumsum_{tag}")
            attn = _attention_prefix(z, c_row[0, :, :n].reshape(heads, 1, n), heads=heads,
                                     name=f"attention_{tag}")
        else:
            c_col, _ = _forget_cumsum(flog, batches=nb, heads=heads, name=f"cumsum_{tag}")
            attn = _attention_main(z, c_col, prefix[0], prefix[1], batches=nb, heads=heads,
                                   name=f"attention_{tag}")
        pool = _pool(z, pw, pool_scale, None if prefix is None else prefix[0],
                     batches=nb, ublock=ublock, name=f"pool_{tag}")
        r = _mix(attn, pool, z, rows_in, wa, wp, wo, mix_post_g,
                 gate_block=gate_block, name=f"mix_{tag}")
        return r, z, c_col

    r_meta, z_meta, c_meta = mixer(meta_tokens.astype(x.dtype), None, 1, "prefix")
    r, _, _ = mixer(x.reshape(batches * seq, d), (z_meta, c_meta), batches, "main")
    out = _conv_ffn(r, r_meta, ffn_pre_g, w_up, ffn_conv_w[0], ffn_conv_b, w_down, ffn_post_g,
                    batches=batches, name="conv_ffn")
    return out.reshape(batches, seq, d)
```

```python
import functools
import math

import jax
import jax.numpy as jnp
from jax import lax
from jax.experimental import pallas as pl
from jax.experimental.pallas import tpu as pltpu

HEAD_DIM = 128
POOL_WINDOWS = (2, 4, 8, 16)
HALO_ROWS = 16
EPS = 1e-6
MASK_VALUE = -1e30
F32 = jnp.float32
BF16 = jnp.bfloat16
LANES = 128
LOG2E = math.log2(math.e)
N_SPLIT = 3
SUM_ROWS = 16
VMEM_LIMIT_BYTES = 56 * 1024 * 1024


def _pick_tile(n, preferred):
    t = min(preferred, n)
    while n % t:
        t //= 2
    return t


def _params(semantics, vmem=VMEM_LIMIT_BYTES):
    return pltpu.CompilerParams(dimension_semantics=semantics, vmem_limit_bytes=vmem)


def _rms(x, g):
    ms = jnp.mean(x * x, axis=-1, keepdims=True)
    return x * lax.rsqrt(ms + EPS) * g


def _rms_rows_to(dst_ref, dst_row0, src_ref, g_ref, chunk):
    rows = src_ref.shape[0]
    chunk = min(chunk, rows)

    def body(c, carry):
        r0 = pl.multiple_of(c * chunk, chunk)
        x = src_ref[pl.ds(r0, chunk), :]
        dst_ref[pl.ds(dst_row0 + r0, chunk), :] = _rms(x, g_ref[...]).astype(dst_ref.dtype)
        return carry

    lax.fori_loop(0, rows // chunk, body, 0)


def _norm_matmul_body(x_ref, g_ref, w_ref, b_ref, ws_ref, bs_ref, o_ref, os_ref, h_ref):
    @pl.when(pl.program_id(1) == 0)
    def _():
        _rms_rows_to(h_ref, 0, x_ref, g_ref, 128)
        os_ref[...] = jnp.dot(h_ref[...], ws_ref[...], preferred_element_type=F32) + bs_ref[...]

    acc = jnp.dot(h_ref[...], w_ref[...], preferred_element_type=F32)
    o_ref[...] = (acc + b_ref[...]).astype(o_ref.dtype)


def _norm_matmul(x, g, w, b, w_side, b_side, *, tm, tn, name):
    m, d = x.shape
    n = w.shape[1]
    ns = w_side.shape[1]
    return pl.pallas_call(
        _norm_matmul_body,
        grid=(m // tm, n // tn),
        in_specs=[
            pl.BlockSpec((tm, d), lambda i, j: (i, 0)),
            pl.BlockSpec((1, d), lambda i, j: (0, 0)),
            pl.BlockSpec((d, tn), lambda i, j: (0, j)),
            pl.BlockSpec((1, tn), lambda i, j: (0, j)),
            pl.BlockSpec((d, ns), lambda i, j: (0, 0)),
            pl.BlockSpec((1, ns), lambda i, j: (0, 0)),
        ],
        out_specs=[
            pl.BlockSpec((tm, tn), lambda i, j: (i, j)),
            pl.BlockSpec((tm, ns), lambda i, j: (i, 0)),
        ],
        out_shape=[
            jax.ShapeDtypeStruct((m, n), BF16),
            jax.ShapeDtypeStruct((m, ns), F32),
        ],
        scratch_shapes=[pltpu.VMEM((tm, d), BF16)],
        compiler_params=_params(("arbitrary", "arbitrary")),
        name=name,
    )(x, g, w, b, w_side, b_side)


def _stage_body(a_ref, b_ref, o_ref, *, first_shifted, shift):
    j = pl.program_id(1)

    @pl.when(j < first_shifted)
    def _():
        o_ref[...] = a_ref[...].astype(o_ref.dtype)

    @pl.when(j >= first_shifted)
    def _():
        both = jnp.concatenate([a_ref[...], b_ref[...]], axis=1).astype(F32)
        width = both.shape[1]
        o_ref[...] = pltpu.roll(both, width - shift, axis=1)[:, :o_ref.shape[1]].astype(o_ref.dtype)


def _stage_in_weights(w, *, skip_start, skip, name):
    d, n = w.shape
    tn = _pick_tile(skip_start, 1024)
    n_out = n - skip
    assert skip < LANES and n_out % tn == 0 and skip_start % tn == 0
    tr = _pick_tile(d, 1024)
    edge = tn // LANES
    return pl.pallas_call(
        functools.partial(_stage_body, first_shifted=skip_start // tn, shift=skip),
        grid=(d // tr, n_out // tn),
        in_specs=[
            pl.BlockSpec((tr, tn), lambda i, j: (i, j)),
            pl.BlockSpec((tr, LANES), lambda i, j: (i, (j + 1) * edge)),
        ],
        out_specs=pl.BlockSpec((tr, tn), lambda i, j: (i, j)),
        out_shape=jax.ShapeDtypeStruct((d, n_out), BF16),
        compiler_params=_params(("arbitrary", "arbitrary")),
        name=name,
    )(w, w)


def _cum_body(f_ref, col_ref, row_ref, *, heads):
    x = f_ref[...]
    rows = x.shape[0]
    c = jnp.minimum(x, 0.0) - jnp.log1p(jnp.exp(-jnp.abs(x)))
    t = lax.broadcasted_iota(jnp.int32, c.shape, 0)
    shift = 1
    while shift < rows:
        c = c + jnp.where(t >= shift, pltpu.roll(c, shift, axis=0), 0.0)
        shift *= 2
    col_ref[...] = c
    row_ref[...] = c.T[:heads, :]


def _forget_cumsum(flog, *, batches, heads, name):
    rows = flog.shape[0] // batches
    return pl.pallas_call(
        functools.partial(_cum_body, heads=heads),
        grid=(batches,),
        in_specs=[pl.BlockSpec((rows, flog.shape[1]), lambda b: (b, 0))],
        out_specs=[
            pl.BlockSpec((rows, flog.shape[1]), lambda b: (b, 0)),
            pl.BlockSpec((None, heads, rows), lambda b: (b, 0, 0)),
        ],
        out_shape=[
            jax.ShapeDtypeStruct(flog.shape, F32),
            jax.ShapeDtypeStruct((batches, heads, rows), F32),
        ],
        compiler_params=_params(("arbitrary",)),
        name=name,
    )(flog)


def _attn_prefix_body(q_ref, k_ref, v_ref, c_ref, o_ref, *, scale):
    n = q_ref.shape[0]
    q = (q_ref[...].astype(F32) * scale).astype(BF16)
    s = lax.dot_general(q, k_ref[...], (((1,), (1,)), ((), ())), preferred_element_type=F32)
    s = s - c_ref[...]
    row = lax.broadcasted_iota(jnp.int32, (n, n), 0)
    col = lax.broadcasted_iota(jnp.int32, (n, n), 1)
    s = jnp.where(row >= col, s, MASK_VALUE)
    p = jnp.exp(s - jnp.max(s, axis=-1, keepdims=True))
    l = jnp.sum(p, axis=-1, keepdims=True)
    acc = jnp.dot(p.astype(BF16), v_ref[...], preferred_element_type=F32)
    o_ref[...] = (acc / l).astype(o_ref.dtype)


def _attention_prefix(z, c_row, *, heads, name):
    n = z.shape[0]
    return pl.pallas_call(
        functools.partial(_attn_prefix_body, scale=HEAD_DIM ** -0.5),
        grid=(heads,),
        in_specs=[
            pl.BlockSpec((n, HEAD_DIM), lambda h: (0, h)),
            pl.BlockSpec((n, HEAD_DIM), lambda h: (0, heads + h)),
            pl.BlockSpec((n, HEAD_DIM), lambda h: (0, 2 * heads + h)),
            pl.BlockSpec((None, 1, n), lambda h: (h, 0, 0)),
        ],
        out_specs=pl.BlockSpec((n, HEAD_DIM), lambda h: (0, h)),
        out_shape=jax.ShapeDtypeStruct((n, heads * HEAD_DIM), BF16),
        compiler_params=_params(("arbitrary",)),
        name=name,
    )(z, z, z, c_row)


def _split3(x):
    p1 = x.astype(BF16)
    r1 = x - p1.astype(F32)
    p2 = r1.astype(BF16)
    p3 = (r1 - p2.astype(F32)).astype(BF16)
    return p1, p2, p3


def _bias_block(x, n_heads):
    lane = lax.broadcasted_iota(jnp.int32, x.shape, 1)
    pieces = _split3(jnp.where(lane < n_heads, x, 0.0))
    block = pieces[0].astype(F32)
    for k in range(1, N_SPLIT):
        block = block + pltpu.roll(pieces[k].astype(F32), k * n_heads, axis=1)
    return block.astype(BF16)


def _attn_main_body(q_ref, k_ref, v_ref, c_ref, kp_ref, vp_ref, cp_ref, o_ref,
                    e_ref, vt_ref, ep_ref, vpt_ref, qaug_ref, m_ref, acc_ref, alpha_ref, p_ref,
                    *, tq, hp, n_heads, scale):
    seq = q_ref.shape[0]
    npre = kp_ref.shape[0]
    group = pl.program_id(1)

    e_ref[...] = _bias_block(c_ref[...] * (-LOG2E), n_heads)
    cp = cp_ref[0:npre, :]
    ep_ref[...] = _bias_block((cp - cp[npre - 1:npre, :]) * (-LOG2E), n_heads)

    def sum_rows(n):
        r = lax.broadcasted_iota(jnp.int32, (SUM_ROWS, n), 0)
        return jnp.where(r == 0, 1.0, 0.0).astype(BF16)

    for hh in range(hp):
        lanes = slice(hh * HEAD_DIM, (hh + 1) * HEAD_DIM)
        vt_ref[hh, 0:HEAD_DIM, :] = v_ref[:, lanes].astype(F32).T.astype(BF16)
        vt_ref[hh, HEAD_DIM:, :] = sum_rows(seq)
        vp = jnp.concatenate([vp_ref[:, lanes].astype(F32),
                              jnp.zeros((LANES - npre, HEAD_DIM), F32)], axis=0)
        vpt_ref[hh, 0:HEAD_DIM, :] = vp.T.astype(BF16)
        vpt_ref[hh, HEAD_DIM:, :] = sum_rows(LANES)

    lane = lax.broadcasted_iota(jnp.int32, (tq, HEAD_DIM), 1)
    key = lax.broadcasted_iota(jnp.int32, (tq, tq), 0)
    qry = lax.broadcasted_iota(jnp.int32, (tq, tq), 1)
    causal = key <= qry
    heads = range(hp)

    def k_aug(hh, k0):
        lanes = slice(hh * HEAD_DIM, (hh + 1) * HEAD_DIM)
        return jnp.concatenate([k_ref[pl.ds(k0, tq), lanes], e_ref[pl.ds(k0, tq), :]], axis=1)

    def scores_t(kaug, hh):
        return lax.dot_general(kaug, qaug_ref[hh], (((1,), (1,)), ((), ())),
                               preferred_element_type=F32)

    def softmax_update(hh, st):
        m = m_ref[hh]
        m_new = jnp.maximum(m, jnp.max(st, axis=0, keepdims=True))
        alpha = jnp.exp2(m - m_new)
        m_ref[hh] = m_new
        return alpha, jnp.exp2((st - m_new).astype(BF16))

    def value_update(hh, k_pend):
        vt = vt_ref[hh, :, pl.ds(k_pend, tq)]
        acc_ref[hh] = alpha_ref[hh] * acc_ref[hh] + jnp.dot(vt, p_ref[hh], preferred_element_type=F32)

    def q_tile(qi, carry_unused):
        q0 = pl.multiple_of(qi * tq, tq)
        for hh in heads:
            lanes = slice(hh * HEAD_DIM, (hh + 1) * HEAD_DIM)
            head = group * hp + hh
            q = (q_ref[pl.ds(q0, tq), lanes].astype(F32) * (scale * LOG2E)).astype(BF16)
            picks = (lane == head) | (lane == n_heads + head) | (lane == 2 * n_heads + head)
            qaug_ref[hh] = jnp.concatenate([q, jnp.where(picks, 1.0, 0.0).astype(BF16)], axis=1)

        sts = []
        for hh in heads:
            lanes = slice(hh * HEAD_DIM, (hh + 1) * HEAD_DIM)
            kp_aug = jnp.concatenate([kp_ref[:, lanes], ep_ref[...]], axis=1)
            sts.append(scores_t(jnp.concatenate([kp_aug, k_aug(hh, q0)], axis=0), hh))
        for hh in heads:
            st_pre = sts[hh][0:npre, :]
            st = jnp.where(causal, sts[hh][npre:, :], MASK_VALUE)
            m = jnp.maximum(jnp.max(st, axis=0, keepdims=True), jnp.max(st_pre, axis=0, keepdims=True))
            m_ref[hh] = m
            p_ref[hh] = jnp.exp2((st - m).astype(BF16))
            alpha_ref[hh] = jnp.ones_like(m)
            p_pre = jnp.concatenate([jnp.exp2((st_pre - m).astype(BF16)),
                                     jnp.zeros((LANES - npre, tq), BF16)], axis=0)
            acc_ref[hh] = jnp.dot(vpt_ref[hh], p_pre, preferred_element_type=F32)

        def kv_tile(j, k_pend):
            k0 = pl.multiple_of(j * tq, tq)
            sts = [scores_t(k_aug(hh, k0), hh) for hh in heads]
            for hh in heads:
                value_update(hh, pl.multiple_of(k_pend, tq))
            for hh in heads:
                alpha_ref[hh], p_ref[hh] = softmax_update(hh, sts[hh])
            return k0

        k_pend = lax.fori_loop(0, qi, kv_tile, q0)

        for hh in heads:
            value_update(hh, pl.multiple_of(k_pend, tq))
        for hh in heads:
            lanes = slice(hh * HEAD_DIM, (hh + 1) * HEAD_DIM)
            acc = acc_ref[hh]
            out = acc[0:HEAD_DIM, :] / acc[HEAD_DIM:HEAD_DIM + 1, :]
            o_ref[pl.ds(q0, tq), lanes] = out.T.astype(o_ref.dtype)
        return carry_unused

    lax.fori_loop(0, seq // tq, q_tile, 0)


def _attention_main(z, c_col, z_prefix, c_col_prefix, *, batches, heads, name):
    seq = z.shape[0] // batches
    npre = z_prefix.shape[0]
    tq = _pick_tile(seq, 256)
    hp = _pick_tile(heads, 8)
    assert N_SPLIT * heads <= LANES
    groups = heads // hp
    width = hp * HEAD_DIM
    return pl.pallas_call(
        functools.partial(_attn_main_body, tq=tq, hp=hp, n_heads=heads, scale=HEAD_DIM ** -0.5),
        grid=(batches, groups),
        in_specs=[
            pl.BlockSpec((seq, width), lambda b, g: (b, g)),
            pl.BlockSpec((seq, width), lambda b, g: (b, groups + g)),
            pl.BlockSpec((seq, width), lambda b, g: (b, 2 * groups + g)),
            pl.BlockSpec((seq, LANES), lambda b, g: (b, 0)),
            pl.BlockSpec((npre, width), lambda b, g: (0, groups + g)),
            pl.BlockSpec((npre, width), lambda b, g: (0, 2 * groups + g)),
            pl.BlockSpec(c_col_prefix.shape, lambda b, g: (0, 0)),
        ],
        out_specs=pl.BlockSpec((seq, width), lambda b, g: (b, g)),
        out_shape=jax.ShapeDtypeStruct((batches * seq, heads * HEAD_DIM), BF16),
        scratch_shapes=[
            pltpu.VMEM((seq, LANES), BF16),
            pltpu.VMEM((hp, HEAD_DIM + SUM_ROWS, seq), BF16),
            pltpu.VMEM((npre, LANES), BF16),
            pltpu.VMEM((hp, HEAD_DIM + SUM_ROWS, LANES), BF16),
            pltpu.VMEM((hp, tq, 2 * HEAD_DIM), BF16),
            pltpu.VMEM((hp, 1, tq), F32),
            pltpu.VMEM((hp, HEAD_DIM + SUM_ROWS, tq), F32),
            pltpu.VMEM((hp, 1, tq), F32),
            pltpu.VMEM((hp, tq, tq), BF16),
        ],
        compiler_params=_params(("arbitrary", "arbitrary")),
        name=name,
    )(z, z, z, c_col, z_prefix, z_prefix, c_col_prefix)


def _pool_body(u_ref, *rest, gdim, pos_offset, tiles_per_batch, has_halo):
    if has_halo:
        uprev_ref, upre_ref, pw_ref, ps_ref, o_ref, ext_ref = rest
    else:
        pw_ref, ps_ref, o_ref, ext_ref = rest
    tm = u_ref.shape[0]
    if has_halo:
        first = (pl.program_id(0) % tiles_per_batch) == 0
        halo = jnp.where(first, upre_ref[...], uprev_ref[...]).astype(F32)
    else:
        halo = jnp.zeros((HALO_ROWS, u_ref.shape[1]), F32)
    ext_ref[0:HALO_ROWS, :] = halo
    ext_ref[HALO_ROWS:HALO_ROWS + tm, :] = u_ref[...].astype(F32)

    for g, w in enumerate(POOL_WINDOWS):
        cols = slice(g * gdim, (g + 1) * gdim)
        cur = ext_ref[HALO_ROWS:HALO_ROWS + tm, cols]
        wsum = cur
        for j in range(1, w):
            wsum = wsum + ext_ref[HALO_ROWS - j:HALO_ROWS - j + tm, cols]
        if pos_offset + 1 >= w:
            mean = wsum * (1.0 / w)
        else:
            pos = lax.broadcasted_iota(jnp.int32, (tm, 1), 0) + pos_offset
            mean = wsum / jnp.minimum(pos + 1, w).astype(F32)
        d = (mean - cur).astype(BF16)
        y = jnp.dot(d, pw_ref[g], preferred_element_type=F32) * ps_ref[:, cols]
        o_ref[:, cols] = y.astype(o_ref.dtype)


def _pool(z, pool_w, pool_scale, prefix_z, *, batches, ublock, name):
    groups, gdim, _ = pool_w.shape
    dp = groups * gdim
    rows = z.shape[0]
    seq = rows // batches
    has_halo = prefix_z is not None
    tm = _pick_tile(seq, 512)
    tiles_per_batch = seq // tm
    in_specs = [pl.BlockSpec((tm, dp), lambda i: (i, ublock))]
    args = [z]
    if has_halo:
        halo_blocks = tm // HALO_ROWS
        in_specs += [
            pl.BlockSpec((HALO_ROWS, dp), lambda i: (jnp.maximum(i * halo_blocks - 1, 0), ublock)),
            pl.BlockSpec((HALO_ROWS, dp), lambda i: (prefix_z.shape[0] // HALO_ROWS - 1, ublock)),
        ]
        args += [z, prefix_z]
        pos_offset = prefix_z.shape[0]
    else:
        pos_offset = 0
    in_specs += [
        pl.BlockSpec((groups, gdim, gdim), lambda i: (0, 0, 0)),
        pl.BlockSpec((1, dp), lambda i: (0, 0)),
    ]
    args += [pool_w, pool_scale]
    return pl.pallas_call(
        functools.partial(_pool_body, gdim=gdim, pos_offset=pos_offset,
                          tiles_per_batch=tiles_per_batch, has_halo=has_halo),
        grid=(rows // tm,),
        in_specs=in_specs,
        out_specs=pl.BlockSpec((tm, dp), lambda i: (i, 0)),
        out_shape=jax.ShapeDtypeStruct((rows, dp), BF16),
        scratch_shapes=[pltpu.VMEM((tm + HALO_ROWS, dp), F32)],
        compiler_params=_params(("arbitrary",)),
        name=name,
    )(*args)


def _mix_body(a_ref, p_ref, ga_ref, gp_ref, x_ref, wa_ref, wp_ref, wo_ref, g_ref, o_ref):
    ya = jnp.dot(a_ref[...], wa_ref[...], preferred_element_type=F32)
    yp = jnp.dot(p_ref[...], wp_ref[...], preferred_element_type=F32)
    m = (jax.nn.sigmoid(ga_ref[...].astype(F32)) * ya
         + jax.nn.sigmoid(gp_ref[...].astype(F32)) * yp)
    mo = jnp.dot(m.astype(BF16), wo_ref[...], preferred_element_type=F32)
    o_ref[...] = x_ref[...] + _rms(mo, g_ref[...])


def _mix(attn, pool, z, x, w_attn_o, w_pool_o, w_out, g_post, *, gate_block, name):
    rows, d = x.shape
    da = attn.shape[1]
    dp = pool.shape[1]
    tm = _pick_tile(rows, 256)
    resident = dict(pipeline_mode=pl.Buffered(1))
    return pl.pallas_call(
        _mix_body,
        grid=(rows // tm,),
        in_specs=[
            pl.BlockSpec((tm, da), lambda i: (i, 0)),
            pl.BlockSpec((tm, dp), lambda i: (i, 0)),
            pl.BlockSpec((tm, d), lambda i: (i, gate_block)),
            pl.BlockSpec((tm, d), lambda i: (i, gate_block + 1)),
            pl.BlockSpec((tm, d), lambda i: (i, 0)),
            pl.BlockSpec((da, d), lambda i: (0, 0), **resident),
            pl.BlockSpec((dp, d), lambda i: (0, 0), **resident),
            pl.BlockSpec((d, d), lambda i: (0, 0), **resident),
            pl.BlockSpec((1, d), lambda i: (0, 0)),
        ],
        out_specs=pl.BlockSpec((tm, d), lambda i: (i, 0)),
        out_shape=jax.ShapeDtypeStruct((rows, d), F32),
        compiler_params=_params(("arbitrary",)),
        name=name,
    )(attn, pool, z, z, x, w_attn_o, w_pool_o, w_out, g_post)


def _gelu_tanh(x):
    c = math.sqrt(2.0 / math.pi)
    v = (x * x).astype(F32) * (0.044715 * c) + c
    return (0.5 * x) * (1.0 + jnp.tanh(x * v.astype(x.dtype)))


def _ffn_body(ru_ref, rprev_ref, rpre_ref, rd_ref, gpre_ref, wg_ref, wv_ref, cwg_ref, cwv_ref,
              cbg_ref, cbv_ref, wd_ref, gpost_ref, o_ref, h_ref, gate_ref, act0_ref, act1_ref, *, nf,
              n_steps, tiles_per_batch):
    t = pl.program_id(0)
    tu = jnp.minimum(t, n_steps - 1)
    fu = tu % nf
    iu = tu // nf
    fd = jnp.maximum(t - 1, 0) % nf
    tm = o_ref.shape[0]

    @pl.when(t == 0)
    def _():
        act1_ref[...] = jnp.zeros(act1_ref.shape, act1_ref.dtype)

    @pl.when(jnp.logical_and(fu == 0, t < n_steps))
    def _():
        first = (iu % tiles_per_batch) == 0
        halo = jnp.where(first, rpre_ref[...], rprev_ref[...])
        h_ref[0:HALO_ROWS, :] = _rms(halo, gpre_ref[...]).astype(BF16)
        _rms_rows_to(h_ref, HALO_ROWS, ru_ref, gpre_ref, 128)

    @pl.when(fd == 0)
    def _():
        o_ref[...] = jnp.zeros(o_ref.shape, o_ref.dtype)

    def up_conv(w_ref, cw_ref, cb_ref):
        a = jnp.dot(h_ref[...], w_ref[...], preferred_element_type=F32)
        a0, a1, a2 = [(a if k == 0 else pltpu.roll(a, k, axis=0))[HALO_ROWS:, :].astype(BF16)
                      for k in range(3)]
        cw = cw_ref[...].astype(BF16)
        out = cb_ref[...].astype(BF16) + a2 * cw[0:1, :]
        out = out + a1 * cw[1:2, :]
        out = out + a0 * cw[2:3, :]
        return out

    def step(act_u_ref, act_d_ref):
        gate_ref[...] = up_conv(wg_ref, cwg_ref, cbg_ref)
        val = up_conv(wv_ref, cwv_ref, cbv_ref)
        act_u_ref[...] = _gelu_tanh(gate_ref[...]) * val
        o_ref[...] += jnp.dot(act_d_ref[...], wd_ref[...], preferred_element_type=F32)

    @pl.when(t % 2 == 0)
    def _():
        step(act0_ref, act1_ref)

    @pl.when(t % 2 == 1)
    def _():
        step(act1_ref, act0_ref)

    @pl.when(jnp.logical_and(fd == nf - 1, t > 0))
    def _():
        chunk = min(128, tm)

        def body(c, carry):
            r0 = pl.multiple_of(c * chunk, chunk)
            rows = pl.ds(r0, chunk)
            o_ref[rows, :] = rd_ref[rows, :] + _rms(o_ref[rows, :], gpost_ref[...])
            return carry

        lax.fori_loop(0, tm // chunk, body, 0)


def _conv_ffn(r, r_prefix, g_pre, w_up, conv_w, conv_b, w_down, g_post, *, batches, name):
    rows, d = r.shape
    dff = w_down.shape[0]
    seq = rows // batches
    tm = _pick_tile(seq, 512)
    tf = _pick_tile(dff, 512)
    nf = dff // tf
    assert nf >= 2
    n_steps = (rows // tm) * nf
    tiles_per_batch = seq // tm
    halo_blocks = tm // HALO_ROWS
    kw = conv_w.shape[0]

    def up_item(t):
        tu = jnp.minimum(t, n_steps - 1)
        return tu // nf, tu % nf

    def down_item(t):
        td = jnp.maximum(t - 1, 0)
        return td // nf, td % nf

    return pl.pallas_call(
        functools.partial(_ffn_body, nf=nf, n_steps=n_steps, tiles_per_batch=tiles_per_batch),
        grid=(n_steps + 1,),
        in_specs=[
            pl.BlockSpec((tm, d), lambda t: (up_item(t)[0], 0)),
            pl.BlockSpec((HALO_ROWS, d), lambda t: (jnp.maximum(up_item(t)[0] * halo_blocks - 1, 0), 0)),
            pl.BlockSpec((HALO_ROWS, d), lambda t: (r_prefix.shape[0] // HALO_ROWS - 1, 0)),
            pl.BlockSpec((tm, d), lambda t: (down_item(t)[0], 0)),
            pl.BlockSpec((1, d), lambda t: (0, 0)),
            pl.BlockSpec((d, tf), lambda t: (0, up_item(t)[1])),
            pl.BlockSpec((d, tf), lambda t: (0, nf + up_item(t)[1])),
            pl.BlockSpec((kw, tf), lambda t: (0, up_item(t)[1])),
            pl.BlockSpec((kw, tf), lambda t: (0, nf + up_item(t)[1])),
            pl.BlockSpec((1, tf), lambda t: (0, up_item(t)[1])),
            pl.BlockSpec((1, tf), lambda t: (0, nf + up_item(t)[1])),
            pl.BlockSpec((tf, d), lambda t: (down_item(t)[1], 0)),
            pl.BlockSpec((1, d), lambda t: (0, 0)),
        ],
        out_specs=pl.BlockSpec((tm, d), lambda t: (down_item(t)[0], 0)),
        out_shape=jax.ShapeDtypeStruct((rows, d), F32),
        scratch_shapes=[
            pltpu.VMEM((HALO_ROWS + tm, d), BF16),
            pltpu.VMEM((tm, tf), BF16),
            pltpu.VMEM((tm, tf), BF16),
            pltpu.VMEM((tm, tf), BF16),
        ],
        compiler_params=_params(("arbitrary",)),
        name=name,
    )(r, r, r_prefix, r, g_pre, w_up, w_up, conv_w, conv_w, conv_b, conv_b, w_down, g_post)


def kernel(x, meta_tokens, mix_pre_g, w_in, b_in, w_attn_o, pool_w, pool_scale, w_pool_o, w_out,
           mix_post_g, ffn_pre_g, w_ffn_up, ffn_conv_w, ffn_conv_b, w_ffn_down, ffn_post_g):
    batches, seq, d = x.shape
    n_meta = meta_tokens.shape[0]
    assert w_in.shape[0] == 1, "single-layer block"
    assert n_meta == HALO_ROWS and n_meta >= max(POOL_WINDOWS) and ffn_conv_w.shape[1] <= HALO_ROWS
    d_attn = w_attn_o.shape[1]
    d_pool = w_pool_o.shape[1]
    heads = d_attn // HEAD_DIM
    assert d_attn == d_pool and 2 * d_attn == d
    assert w_in.shape[2] == 3 * d_attn + heads + d_pool + 2 * d

    f0 = 3 * d_attn
    w = w_in[0]
    b = b_in[0]
    w_main = _stage_in_weights(w.astype(BF16), skip_start=f0, skip=heads, name="stage_w_in")
    b_main = jnp.concatenate([b[:f0], b[f0 + heads:]])[None, :]
    w_f = jnp.pad(w[:, f0:f0 + heads], ((0, 0), (0, LANES - heads))).astype(BF16)
    b_f = jnp.pad(b[f0:f0 + heads], (0, LANES - heads))[None, :]
    wa = w_attn_o[0].astype(BF16)
    wp = w_pool_o[0].astype(BF16)
    wo = w_out[0].astype(BF16)
    pw = pool_w[0].astype(BF16)
    w_up = w_ffn_up[0].astype(BF16)
    w_down = w_ffn_down[0].astype(BF16)
    ublock = f0 // d_pool
    gate_block = (f0 + d_pool) // d

    def mixer(rows_in, prefix, nb, tag):
        n, _ = rows_in.shape
        s = n // nb
        z, flog = _norm_matmul(rows_in, mix_pre_g, w_main, b_main, w_f, b_f,
                               tm=_pick_tile(s, 1024), tn=_pick_tile(w_main.shape[1], 1024),
                               name=f"inproj_{tag}")
        if prefix is None:
            flog = jnp.pad(flog, ((0, LANES - n), (0, 0)))
            c_col, c_row = _forget_cumsum(flog, batches=nb, heads=heads, name=f"cumsum_{tag}")
            attn = _attention_prefix(z, c_row[0, :, :n].reshape(heads, 1, n), heads=heads,
                                     name=f"attention_{tag}")
        else:
            c_col, _ = _forget_cumsum(flog, batches=nb, heads=heads, name=f"cumsum_{tag}")
            attn = _attention_main(z, c_col, prefix[0], prefix[1], batches=nb, heads=heads,
                                   name=f"attention_{tag}")
        pool = _pool(z, pw, pool_scale, None if prefix is None else prefix[0],
                     batches=nb, ublock=ublock, name=f"pool_{tag}")
        r = _mix(attn, pool, z, rows_in, wa, wp, wo, mix_post_g,
                 gate_block=gate_block, name=f"mix_{tag}")
        return r, z, c_col

    r_meta, z_meta, c_meta = mixer(meta_tokens.astype(x.dtype), None, 1, "prefix")
    r, _, _ = mixer(x.reshape(batches * seq, d), (z_meta, c_meta), batches, "main")
    out = _conv_ffn(r, r_meta, ffn_pre_g, w_up, ffn_conv_w[0], ffn_conv_b, w_down, ffn_post_g,
                    batches=batches, name="conv_ffn")
    return out.reshape(batches, seq, d)
```

```python
import functools
import math

import jax
import jax.numpy as jnp
from jax import lax
from jax.experimental import pallas as pl
from jax.experimental.pallas import tpu as pltpu

HEAD_DIM = 128
POOL_WINDOWS = (2, 4, 8, 16)
HALO_ROWS = 16
EPS = 1e-6
MASK_VALUE = -1e30
F32 = jnp.float32
BF16 = jnp.bfloat16
LANES = 128
LOG2E = math.log2(math.e)
N_SPLIT = 3
SUM_ROWS = 16
VMEM_LIMIT_BYTES = 56 * 1024 * 1024


def _pick_tile(n, preferred):
    t = min(preferred, n)
    while n % t:
        t //= 2
    return t


def _params(semantics, vmem=VMEM_LIMIT_BYTES):
    return pltpu.CompilerParams(dimension_semantics=semantics, vmem_limit_bytes=vmem)


def _rms(x, g):
    ms = jnp.mean(x * x, axis=-1, keepdims=True)
    return x * lax.rsqrt(ms + EPS) * g


def _rms_rows_to(dst_ref, dst_row0, src_ref, g_ref, chunk):
    rows = src_ref.shape[0]
    chunk = min(chunk, rows)

    def body(c, carry):
        r0 = pl.multiple_of(c * chunk, chunk)
        x = src_ref[pl.ds(r0, chunk), :]
        dst_ref[pl.ds(dst_row0 + r0, chunk), :] = _rms(x, g_ref[...]).astype(dst_ref.dtype)
        return carry

    lax.fori_loop(0, rows // chunk, body, 0)


def _norm_matmul_body(x_ref, g_ref, w_ref, b_ref, ws_ref, bs_ref, *rest, n_casts):
    cast_in = rest[:n_casts]
    o_ref, os_ref = rest[n_casts:n_casts + 2]
    cast_out = rest[n_casts + 2:2 * n_casts + 2]
    h_ref = rest[-1]

    @pl.when(pl.program_id(1) == 0)
    def _():
        _rms_rows_to(h_ref, 0, x_ref, g_ref, 128)
        os_ref[...] = jnp.dot(h_ref[...], ws_ref[...], preferred_element_type=F32) + bs_ref[...]

    acc = jnp.dot(h_ref[...], w_ref[...], preferred_element_type=F32)
    o_ref[...] = (acc + b_ref[...]).astype(o_ref.dtype)

    for src, dst in zip(cast_in, cast_out):
        dst[...] = src[...].astype(dst.dtype)


def _norm_matmul(x, g, w, b, w_side, b_side, casts=(), *, tm, tn, name):
    m, d = x.shape
    n = w.shape[1]
    ns = w_side.shape[1]
    grid = (m // tm, n // tn)
    n_steps = grid[0] * grid[1]
    cast_specs = []
    for a in casts:
        chunk = next(c for c in range(16, a.shape[0] + 1, 16)
                     if a.shape[0] % c == 0 and a.shape[0] // c <= n_steps)
        last = a.shape[0] // chunk - 1
        cast_specs.append(pl.BlockSpec(
            (chunk, a.shape[1]), lambda i, j, last=last: (jnp.minimum(i * grid[1] + j, last), 0)))
    outs = pl.pallas_call(
        functools.partial(_norm_matmul_body, n_casts=len(casts)),
        grid=grid,
        in_specs=[
            pl.BlockSpec((tm, d), lambda i, j: (i, 0)),
            pl.BlockSpec((1, d), lambda i, j: (0, 0)),
            pl.BlockSpec((d, tn), lambda i, j: (0, j)),
            pl.BlockSpec((1, tn), lambda i, j: (0, j)),
            pl.BlockSpec((d, ns), lambda i, j: (0, 0)),
            pl.BlockSpec((1, ns), lambda i, j: (0, 0)),
        ] + cast_specs,
        out_specs=[
            pl.BlockSpec((tm, tn), lambda i, j: (i, j)),
            pl.BlockSpec((tm, ns), lambda i, j: (i, 0)),
        ] + cast_specs,
        out_shape=[
            jax.ShapeDtypeStruct((m, n), BF16),
            jax.ShapeDtypeStruct((m, ns), F32),
        ] + [jax.ShapeDtypeStruct(a.shape, BF16) for a in casts],
        scratch_shapes=[pltpu.VMEM((tm, d), BF16)],
        compiler_params=_params(("arbitrary", "arbitrary")),
        name=name,
    )(x, g, w, b, w_side, b_side, *casts)
    return outs[0], outs[1], outs[2:]


def _stage_body(a_ref, b_ref, o_ref, *, first_shifted, shift):
    j = pl.program_id(1)

    @pl.when(j < first_shifted)
    def _():
        o_ref[...] = a_ref[...].astype(o_ref.dtype)

    @pl.when(j >= first_shifted)
    def _():
        both = jnp.concatenate([a_ref[...], b_ref[...]], axis=1).astype(F32)
        width = both.shape[1]
        o_ref[...] = pltpu.roll(both, width - shift, axis=1)[:, :o_ref.shape[1]].astype(o_ref.dtype)


def _stage_in_weights(w, *, skip_start, skip, name):
    d, n = w.shape
    tn = _pick_tile(skip_start, 1024)
    n_out = n - skip
    assert skip < LANES and n_out % tn == 0 and skip_start % tn == 0
    tr = _pick_tile(d, 1024)
    edge = tn // LANES
    return pl.pallas_call(
        functools.partial(_stage_body, first_shifted=skip_start // tn, shift=skip),
        grid=(d // tr, n_out // tn),
        in_specs=[
            pl.BlockSpec((tr, tn), lambda i, j: (i, j)),
            pl.BlockSpec((tr, LANES), lambda i, j: (i, (j + 1) * edge)),
        ],
        out_specs=pl.BlockSpec((tr, tn), lambda i, j: (i, j)),
        out_shape=jax.ShapeDtypeStruct((d, n_out), BF16),
        compiler_params=_params(("arbitrary", "arbitrary")),
        name=name,
    )(w, w)


def _cum_body(f_ref, col_ref, row_ref, *, heads):
    x = f_ref[...]
    rows = x.shape[0]
    c = jnp.minimum(x, 0.0) - jnp.log1p(jnp.exp(-jnp.abs(x)))
    t = lax.broadcasted_iota(jnp.int32, c.shape, 0)
    shift = 1
    while shift < rows:
        c = c + jnp.where(t >= shift, pltpu.roll(c, shift, axis=0), 0.0)
        shift *= 2
    col_ref[...] = c
    row_ref[...] = c.T[:heads, :]


def _forget_cumsum(flog, *, batches, heads, name):
    rows = flog.shape[0] // batches
    return pl.pallas_call(
        functools.partial(_cum_body, heads=heads),
        grid=(batches,),
        in_specs=[pl.BlockSpec((rows, flog.shape[1]), lambda b: (b, 0))],
        out_specs=[
            pl.BlockSpec((rows, flog.shape[1]), lambda b: (b, 0)),
            pl.BlockSpec((None, heads, rows), lambda b: (b, 0, 0)),
        ],
        out_shape=[
            jax.ShapeDtypeStruct(flog.shape, F32),
            jax.ShapeDtypeStruct((batches, heads, rows), F32),
        ],
        compiler_params=_params(("arbitrary",)),
        name=name,
    )(flog)


def _attn_prefix_body(q_ref, k_ref, v_ref, c_ref, o_ref, *, scale):
    n = q_ref.shape[0]
    q = (q_ref[...].astype(F32) * scale).astype(BF16)
    s = lax.dot_general(q, k_ref[...], (((1,), (1,)), ((), ())), preferred_element_type=F32)
    s = s - c_ref[...]
    row = lax.broadcasted_iota(jnp.int32, (n, n), 0)
    col = lax.broadcasted_iota(jnp.int32, (n, n), 1)
    s = jnp.where(row >= col, s, MASK_VALUE)
    p = jnp.exp(s - jnp.max(s, axis=-1, keepdims=True))
    l = jnp.sum(p, axis=-1, keepdims=True)
    acc = jnp.dot(p.astype(BF16), v_ref[...], preferred_element_type=F32)
    o_ref[...] = (acc / l).astype(o_ref.dtype)


def _attention_prefix(z, c_row, *, heads, name):
    n = z.shape[0]
    return pl.pallas_call(
        functools.partial(_attn_prefix_body, scale=HEAD_DIM ** -0.5),
        grid=(heads,),
        in_specs=[
            pl.BlockSpec((n, HEAD_DIM), lambda h: (0, h)),
            pl.BlockSpec((n, HEAD_DIM), lambda h: (0, heads + h)),
            pl.BlockSpec((n, HEAD_DIM), lambda h: (0, 2 * heads + h)),
            pl.BlockSpec((None, 1, n), lambda h: (h, 0, 0)),
        ],
        out_specs=pl.BlockSpec((n, HEAD_DIM), lambda h: (0, h)),
        out_shape=jax.ShapeDtypeStruct((n, heads * HEAD_DIM), BF16),
        compiler_params=_params(("arbitrary",)),
        name=name,
    )(z, z, z, c_row)


def _split3(x):
    p1 = x.astype(BF16)
    r1 = x - p1.astype(F32)
    p2 = r1.astype(BF16)
    p3 = (r1 - p2.astype(F32)).astype(BF16)
    return p1, p2, p3


def _bias_block(x, n_heads):
    lane = lax.broadcasted_iota(jnp.int32, x.shape, 1)
    pieces = _split3(jnp.where(lane < n_heads, x, 0.0))
    block = pieces[0].astype(F32)
    for k in range(1, N_SPLIT):
        block = block + pltpu.roll(pieces[k].astype(F32), k * n_heads, axis=1)
    return block.astype(BF16)


def _attn_main_body(q_ref, k_ref, v_ref, c_ref, kp_ref, vp_ref, cp_ref, o_ref,
                    e_ref, vt_ref, ep_ref, vpt_ref, qaug_ref, m_ref, acc_ref, alpha_ref, p_ref,
                    *, tq, hp, n_heads, scale):
    seq = q_ref.shape[0]
    npre = kp_ref.shape[0]
    group = pl.program_id(1)

    e_ref[...] = _bias_block(c_ref[...] * (-LOG2E), n_heads)
    cp = cp_ref[0:npre, :]
    ep_ref[...] = _bias_block((cp - cp[npre - 1:npre, :]) * (-LOG2E), n_heads)

    def sum_rows(n):
        r = lax.broadcasted_iota(jnp.int32, (SUM_ROWS, n), 0)
        return jnp.where(r == 0, 1.0, 0.0).astype(BF16)

    for hh in range(hp):
        lanes = slice(hh * HEAD_DIM, (hh + 1) * HEAD_DIM)
        vt_ref[hh, 0:HEAD_DIM, :] = v_ref[:, lanes].astype(F32).T.astype(BF16)
        vt_ref[hh, HEAD_DIM:, :] = sum_rows(seq)
        vp = jnp.concatenate([vp_ref[:, lanes].astype(F32),
                              jnp.zeros((LANES - npre, HEAD_DIM), F32)], axis=0)
        vpt_ref[hh, 0:HEAD_DIM, :] = vp.T.astype(BF16)
        vpt_ref[hh, HEAD_DIM:, :] = sum_rows(LANES)

    lane = lax.broadcasted_iota(jnp.int32, (tq, HEAD_DIM), 1)
    key = lax.broadcasted_iota(jnp.int32, (tq, tq), 0)
    qry = lax.broadcasted_iota(jnp.int32, (tq, tq), 1)
    causal = key <= qry
    heads = range(hp)

    def k_aug(hh, k0):
        lanes = slice(hh * HEAD_DIM, (hh + 1) * HEAD_DIM)
        return jnp.concatenate([k_ref[pl.ds(k0, tq), lanes], e_ref[pl.ds(k0, tq), :]], axis=1)

    def scores_t(kaug, hh):
        return lax.dot_general(kaug, qaug_ref[hh], (((1,), (1,)), ((), ())),
                               preferred_element_type=F32)

    def softmax_update(hh, st):
        m = m_ref[hh]
        m_new = jnp.maximum(m, jnp.max(st, axis=0, keepdims=True))
        alpha = jnp.exp2(m - m_new)
        m_ref[hh] = m_new
        return alpha, jnp.exp2((st - m_new).astype(BF16))

    def value_update(hh, k_pend):
        vt = vt_ref[hh, :, pl.ds(k_pend, tq)]
        acc_ref[hh] = alpha_ref[hh] * acc_ref[hh] + jnp.dot(vt, p_ref[hh], preferred_element_type=F32)

    def q_tile(qi, carry_unused):
        q0 = pl.multiple_of(qi * tq, tq)
        for hh in heads:
            lanes = slice(hh * HEAD_DIM, (hh + 1) * HEAD_DIM)
            head = group * hp + hh
            q = (q_ref[pl.ds(q0, tq), lanes].astype(F32) * (scale * LOG2E)).astype(BF16)
            picks = (lane == head) | (lane == n_heads + head) | (lane == 2 * n_heads + head)
            qaug_ref[hh] = jnp.concatenate([q, jnp.where(picks, 1.0, 0.0).astype(BF16)], axis=1)

        sts = []
        for hh in heads:
            lanes = slice(hh * HEAD_DIM, (hh + 1) * HEAD_DIM)
            kp_aug = jnp.concatenate([kp_ref[:, lanes], ep_ref[...]], axis=1)
            sts.append(scores_t(jnp.concatenate([kp_aug, k_aug(hh, q0)], axis=0), hh))
        for hh in heads:
            st_pre = sts[hh][0:npre, :]
            st = jnp.where(causal, sts[hh][npre:, :], MASK_VALUE)
            m = jnp.maximum(jnp.max(st, axis=0, keepdims=True), jnp.max(st_pre, axis=0, keepdims=True))
            m_ref[hh] = m
            p_ref[hh] = jnp.exp2((st - m).astype(BF16))
            alpha_ref[hh] = jnp.ones_like(m)
            p_pre = jnp.concatenate([jnp.exp2((st_pre - m).astype(BF16)),
                                     jnp.zeros((LANES - npre, tq), BF16)], axis=0)
            acc_ref[hh] = jnp.dot(vpt_ref[hh], p_pre, preferred_element_type=F32)

        def kv_tile(j, k_pend):
            k0 = pl.multiple_of(j * tq, tq)
            sts = [scores_t(k_aug(hh, k0), hh) for hh in heads]
            for hh in heads:
                value_update(hh, pl.multiple_of(k_pend, tq))
            for hh in heads:
                alpha_ref[hh], p_ref[hh] = softmax_update(hh, sts[hh])
            return k0

        k_pend = lax.fori_loop(0, qi, kv_tile, q0)

        for hh in heads:
            value_update(hh, pl.multiple_of(k_pend, tq))
        for hh in heads:
            lanes = slice(hh * HEAD_DIM, (hh + 1) * HEAD_DIM)
            acc = acc_ref[hh]
            out = acc[0:HEAD_DIM, :] / acc[HEAD_DIM:HEAD_DIM + 1, :]
            o_ref[pl.ds(q0, tq), lanes] = out.T.astype(o_ref.dtype)
        return carry_unused

    lax.fori_loop(0, seq // tq, q_tile, 0)


def _attention_main(z, c_col, z_prefix, c_col_prefix, *, batches, heads, name):
    seq = z.shape[0] // batches
    npre = z_prefix.shape[0]
    tq = _pick_tile(seq, 256)
    hp = _pick_tile(heads, 8)
    assert N_SPLIT * heads <= LANES
    groups = heads // hp
    width = hp * HEAD_DIM
    return pl.pallas_call(
        functools.partial(_attn_main_body, tq=tq, hp=hp, n_heads=heads, scale=HEAD_DIM ** -0.5),
        grid=(batches, groups),
        in_specs=[
            pl.BlockSpec((seq, width), lambda b, g: (b, g)),
            pl.BlockSpec((seq, width), lambda b, g: (b, groups + g)),
            pl.BlockSpec((seq, width), lambda b, g: (b, 2 * groups + g)),
            pl.BlockSpec((seq, LANES), lambda b, g: (b, 0)),
            pl.BlockSpec((npre, width), lambda b, g: (0, groups + g)),
            pl.BlockSpec((npre, width), lambda b, g: (0, 2 * groups + g)),
            pl.BlockSpec(c_col_prefix.shape, lambda b, g: (0, 0)),
        ],
        out_specs=pl.BlockSpec((seq, width), lambda b, g: (b, g)),
        out_shape=jax.ShapeDtypeStruct((batches * seq, heads * HEAD_DIM), BF16),
        scratch_shapes=[
            pltpu.VMEM((seq, LANES), BF16),
            pltpu.VMEM((hp, HEAD_DIM + SUM_ROWS, seq), BF16),
            pltpu.VMEM((npre, LANES), BF16),
            pltpu.VMEM((hp, HEAD_DIM + SUM_ROWS, LANES), BF16),
            pltpu.VMEM((hp, tq, 2 * HEAD_DIM), BF16),
            pltpu.VMEM((hp, 1, tq), F32),
            pltpu.VMEM((hp, HEAD_DIM + SUM_ROWS, tq), F32),
            pltpu.VMEM((hp, 1, tq), F32),
            pltpu.VMEM((hp, tq, tq), BF16),
        ],
        compiler_params=_params(("arbitrary", "arbitrary")),
        name=name,
    )(z, z, z, c_col, z_prefix, z_prefix, c_col_prefix)


def _pool_body(u_ref, *rest, gdim, pos_offset, tiles_per_batch, has_halo):
    if has_halo:
        uprev_ref, upre_ref, pw_ref, ps_ref, o_ref, ext_ref = rest
    else:
        pw_ref, ps_ref, o_ref, ext_ref = rest
    tm = u_ref.shape[0]
    if has_halo:
        first = (pl.program_id(0) % tiles_per_batch) == 0
        halo = jnp.where(first, upre_ref[...], uprev_ref[...]).astype(F32)
    else:
        halo = jnp.zeros((HALO_ROWS, u_ref.shape[1]), F32)
    ext_ref[0:HALO_ROWS, :] = halo
    ext_ref[HALO_ROWS:HALO_ROWS + tm, :] = u_ref[...].astype(F32)

    for g, w in enumerate(POOL_WINDOWS):
        cols = slice(g * gdim, (g + 1) * gdim)
        wsum = ext_ref[:, cols]
        shift = 1
        while shift < w:
            wsum = wsum + pltpu.roll(wsum, shift, axis=0)
            shift *= 2
        wsum = wsum[HALO_ROWS:, :]
        cur = ext_ref[HALO_ROWS:HALO_ROWS + tm, cols]
        if pos_offset + 1 >= w:
            mean = wsum * (1.0 / w)
        else:
            pos = lax.broadcasted_iota(jnp.int32, (tm, 1), 0) + pos_offset
            mean = wsum / jnp.minimum(pos + 1, w).astype(F32)
        d = (mean - cur).astype(BF16)
        y = jnp.dot(d, pw_ref[g], preferred_element_type=F32) * ps_ref[:, cols]
        o_ref[:, cols] = y.astype(o_ref.dtype)


def _pool(z, pool_w, pool_scale, prefix_z, *, batches, ublock, name):
    groups, gdim, _ = pool_w.shape
    dp = groups * gdim
    rows = z.shape[0]
    seq = rows // batches
    has_halo = prefix_z is not None
    tm = _pick_tile(seq, 512)
    tiles_per_batch = seq // tm
    in_specs = [pl.BlockSpec((tm, dp), lambda i: (i, ublock))]
    args = [z]
    if has_halo:
        halo_blocks = tm // HALO_ROWS
        in_specs += [
            pl.BlockSpec((HALO_ROWS, dp), lambda i: (jnp.maximum(i * halo_blocks - 1, 0), ublock)),
            pl.BlockSpec((HALO_ROWS, dp), lambda i: (prefix_z.shape[0] // HALO_ROWS - 1, ublock)),
        ]
        args += [z, prefix_z]
        pos_offset = prefix_z.shape[0]
    else:
        pos_offset = 0
    in_specs += [
        pl.BlockSpec((groups, gdim, gdim), lambda i: (0, 0, 0)),
        pl.BlockSpec((1, dp), lambda i: (0, 0)),
    ]
    args += [pool_w, pool_scale]
    return pl.pallas_call(
        functools.partial(_pool_body, gdim=gdim, pos_offset=pos_offset,
                          tiles_per_batch=tiles_per_batch, has_halo=has_halo),
        grid=(rows // tm,),
        in_specs=in_specs,
        out_specs=pl.BlockSpec((tm, dp), lambda i: (i, 0)),
        out_shape=jax.ShapeDtypeStruct((rows, dp), BF16),
        scratch_shapes=[pltpu.VMEM((tm + HALO_ROWS, dp), F32)],
        compiler_params=_params(("arbitrary",)),
        name=name,
    )(*args)


def _mix_body(a_ref, p_ref, ga_ref, gp_ref, x_ref, wa_ref, wp_ref, wo_ref, g_ref, o_ref):
    ya = jnp.dot(a_ref[...], wa_ref[...], preferred_element_type=F32)
    yp = jnp.dot(p_ref[...], wp_ref[...], preferred_element_type=F32)
    m = (jax.nn.sigmoid(ga_ref[...].astype(F32)) * ya
         + jax.nn.sigmoid(gp_ref[...].astype(F32)) * yp)
    mo = jnp.dot(m.astype(BF16), wo_ref[...], preferred_element_type=F32)
    o_ref[...] = x_ref[...] + _rms(mo, g_ref[...])


def _mix(attn, pool, z, x, w_attn_o, w_pool_o, w_out, g_post, *, gate_block, name):
    rows, d = x.shape
    da = attn.shape[1]
    dp = pool.shape[1]
    tm = _pick_tile(rows, 256)
    resident = dict(pipeline_mode=pl.Buffered(1))
    return pl.pallas_call(
        _mix_body,
        grid=(rows // tm,),
        in_specs=[
            pl.BlockSpec((tm, da), lambda i: (i, 0)),
            pl.BlockSpec((tm, dp), lambda i: (i, 0)),
            pl.BlockSpec((tm, d), lambda i: (i, gate_block)),
            pl.BlockSpec((tm, d), lambda i: (i, gate_block + 1)),
            pl.BlockSpec((tm, d), lambda i: (i, 0)),
            pl.BlockSpec((da, d), lambda i: (0, 0), **resident),
            pl.BlockSpec((dp, d), lambda i: (0, 0), **resident),
            pl.BlockSpec((d, d), lambda i: (0, 0), **resident),
            pl.BlockSpec((1, d), lambda i: (0, 0)),
        ],
        out_specs=pl.BlockSpec((tm, d), lambda i: (i, 0)),
        out_shape=jax.ShapeDtypeStruct((rows, d), F32),
        compiler_params=_params(("arbitrary",)),
        name=name,
    )(attn, pool, z, z, x, w_attn_o, w_pool_o, w_out, g_post)


def _gelu_tanh(x):
    c = math.sqrt(2.0 / math.pi)
    v = (x * x).astype(F32) * (0.044715 * c) + c
    return (0.5 * x) * (1.0 + jnp.tanh(x * v.astype(x.dtype)))


def _ffn_body(ru_ref, rprev_ref, rpre_ref, rd_ref, gpre_ref, wg_ref, wv_ref, cwg_ref, cwv_ref,
              cbg_ref, cbv_ref, wd_ref, gpost_ref, o_ref, h_ref, gate_ref, act0_ref, act1_ref, *, nf,
              n_steps, tiles_per_batch):
    t = pl.program_id(0)
    tu = jnp.minimum(t, n_steps - 1)
    fu = tu % nf
    iu = tu // nf
    fd = jnp.maximum(t - 1, 0) % nf
    tm = o_ref.shape[0]

    @pl.when(t == 0)
    def _():
        act1_ref[...] = jnp.zeros(act1_ref.shape, act1_ref.dtype)

    @pl.when(jnp.logical_and(fu == 0, t < n_steps))
    def _():
        first = (iu % tiles_per_batch) == 0
        halo = jnp.where(first, rpre_ref[...], rprev_ref[...])
        h_ref[0:HALO_ROWS, :] = _rms(halo, gpre_ref[...]).astype(BF16)
        _rms_rows_to(h_ref, HALO_ROWS, ru_ref, gpre_ref, 128)

    @pl.when(fd == 0)
    def _():
        o_ref[...] = jnp.zeros(o_ref.shape, o_ref.dtype)

    def up_conv(w_ref, cw_ref, cb_ref):
        a = jnp.dot(h_ref[...], w_ref[...], preferred_element_type=F32)
        a0, a1, a2 = [(a if k == 0 else pltpu.roll(a, k, axis=0))[HALO_ROWS:, :].astype(BF16)
                      for k in range(3)]
        cw = cw_ref[...].astype(BF16)
        out = cb_ref[...].astype(BF16) + a2 * cw[0:1, :]
        out = out + a1 * cw[1:2, :]
        out = out + a0 * cw[2:3, :]
        return out

    def step(act_u_ref, act_d_ref):
        gate_ref[...] = up_conv(wg_ref, cwg_ref, cbg_ref)
        val = up_conv(wv_ref, cwv_ref, cbv_ref)
        act_u_ref[...] = _gelu_tanh(gate_ref[...]) * val
        o_ref[...] += jnp.dot(act_d_ref[...], wd_ref[...], preferred_element_type=F32)

    @pl.when(t % 2 == 0)
    def _():
        step(act0_ref, act1_ref)

    @pl.when(t % 2 == 1)
    def _():
        step(act1_ref, act0_ref)

    @pl.when(jnp.logical_and(fd == nf - 1, t > 0))
    def _():
        chunk = min(128, tm)

        def body(c, carry):
            r0 = pl.multiple_of(c * chunk, chunk)
            rows = pl.ds(r0, chunk)
            o_ref[rows, :] = rd_ref[rows, :] + _rms(o_ref[rows, :], gpost_ref[...])
            return carry

        lax.fori_loop(0, tm // chunk, body, 0)


def _conv_ffn(r, r_prefix, g_pre, w_up, conv_w, conv_b, w_down, g_post, *, batches, name):
    rows, d = r.shape
    dff = w_down.shape[0]
    seq = rows // batches
    tm = _pick_tile(seq, 512)
    tf = _pick_tile(dff, 512)
    nf = dff // tf
    assert nf >= 2
    n_steps = (rows // tm) * nf
    tiles_per_batch = seq // tm
    halo_blocks = tm // HALO_ROWS
    kw = conv_w.shape[0]

    def up_item(t):
        tu = jnp.minimum(t, n_steps - 1)
        return tu // nf, tu % nf

    def down_item(t):
        td = jnp.maximum(t - 1, 0)
        return td // nf, td % nf

    return pl.pallas_call(
        functools.partial(_ffn_body, nf=nf, n_steps=n_steps, tiles_per_batch=tiles_per_batch),
        grid=(n_steps + 1,),
        in_specs=[
            pl.BlockSpec((tm, d), lambda t: (up_item(t)[0], 0)),
            pl.BlockSpec((HALO_ROWS, d), lambda t: (jnp.maximum(up_item(t)[0] * halo_blocks - 1, 0), 0)),
            pl.BlockSpec((HALO_ROWS, d), lambda t: (r_prefix.shape[0] // HALO_ROWS - 1, 0)),
            pl.BlockSpec((tm, d), lambda t: (down_item(t)[0], 0)),
            pl.BlockSpec((1, d), lambda t: (0, 0)),
            pl.BlockSpec((d, tf), lambda t: (0, up_item(t)[1])),
            pl.BlockSpec((d, tf), lambda t: (0, nf + up_item(t)[1])),
            pl.BlockSpec((kw, tf), lambda t: (0, up_item(t)[1])),
            pl.BlockSpec((kw, tf), lambda t: (0, nf + up_item(t)[1])),
            pl.BlockSpec((1, tf), lambda t: (0, up_item(t)[1])),
            pl.BlockSpec((1, tf), lambda t: (0, nf + up_item(t)[1])),
            pl.BlockSpec((tf, d), lambda t: (down_item(t)[1], 0)),
            pl.BlockSpec((1, d), lambda t: (0, 0)),
        ],
        out_specs=pl.BlockSpec((tm, d), lambda t: (down_item(t)[0], 0)),
        out_shape=jax.ShapeDtypeStruct((rows, d), F32),
        scratch_shapes=[
            pltpu.VMEM((HALO_ROWS + tm, d), BF16),
            pltpu.VMEM((tm, tf), BF16),
            pltpu.VMEM((tm, tf), BF16),
            pltpu.VMEM((tm, tf), BF16),
        ],
        compiler_params=_params(("arbitrary",)),
        name=name,
    )(r, r, r_prefix, r, g_pre, w_up, w_up, conv_w, conv_w, conv_b, conv_b, w_down, g_post)


def kernel(x, meta_tokens, mix_pre_g, w_in, b_in, w_attn_o, pool_w, pool_scale, w_pool_o, w_out,
           mix_post_g, ffn_pre_g, w_ffn_up, ffn_conv_w, ffn_conv_b, w_ffn_down, ffn_post_g):
    batches, seq, d = x.shape
    n_meta = meta_tokens.shape[0]
    assert w_in.shape[0] == 1, "single-layer block"
    assert n_meta == HALO_ROWS and n_meta >= max(POOL_WINDOWS) and ffn_conv_w.shape[1] <= HALO_ROWS
    d_attn = w_attn_o.shape[1]
    d_pool = w_pool_o.shape[1]
    heads = d_attn // HEAD_DIM
    assert d_attn == d_pool and 2 * d_attn == d
    assert w_in.shape[2] == 3 * d_attn + heads + d_pool + 2 * d

    f0 = 3 * d_attn
    w = w_in[0]
    b = b_in[0]
    w_main = _stage_in_weights(w.astype(BF16), skip_start=f0, skip=heads, name="stage_w_in")
    b_main = jnp.concatenate([b[:f0], b[f0 + heads:]])[None, :]
    w_f = jnp.pad(w[:, f0:f0 + heads], ((0, 0), (0, LANES - heads))).astype(BF16)
    b_f = jnp.pad(b[f0:f0 + heads], (0, LANES - heads))[None, :]
    pw = pool_w[0].astype(BF16)
    later_weights = (w_attn_o[0], w_pool_o[0], w_out[0], w_ffn_up[0], w_ffn_down[0])
    ublock = f0 // d_pool
    gate_block = (f0 + d_pool) // d

    def in_project(rows_in, nb, tag, casts=()):
        s = rows_in.shape[0] // nb
        return _norm_matmul(rows_in, mix_pre_g, w_main, b_main, w_f, b_f, casts,
                            tm=_pick_tile(s, 1024), tn=_pick_tile(w_main.shape[1], 1024),
                            name=f"inproj_{tag}")

    def mixer(rows_in, projected, prefix, nb, tag):
        n, _ = rows_in.shape
        z, flog = projected
        if prefix is None:
            flog = jnp.pad(flog, ((0, LANES - n), (0, 0)))
            c_col, c_row = _forget_cumsum(flog, batches=nb, heads=heads, name=f"cumsum_{tag}")
            attn = _attention_prefix(z, c_row[0, :, :n].reshape(heads, 1, n), heads=heads,
                                     name=f"attention_{tag}")
        else:
            c_col, _ = _forget_cumsum(flog, batches=nb, heads=heads, name=f"cumsum_{tag}")
            attn = _attention_main(z, c_col, prefix[0], prefix[1], batches=nb, heads=heads,
                                   name=f"attention_{tag}")
        pool = _pool(z, pw, pool_scale, None if prefix is None else prefix[0],
                     batches=nb, ublock=ublock, name=f"pool_{tag}")
        r = _mix(attn, pool, z, rows_in, wa, wp, wo, mix_post_g,
                 gate_block=gate_block, name=f"mix_{tag}")
        return r, z, c_col

    x_rows = x.reshape(batches * seq, d)
    meta_rows = meta_tokens.astype(x.dtype)
    z_x, flog_x, (wa, wp, wo, w_up, w_down) = in_project(x_rows, batches, "main", later_weights)
    z_m, flog_m, _ = in_project(meta_rows, 1, "prefix")
    r_meta, z_meta, c_meta = mixer(meta_rows, (z_m, flog_m), None, 1, "prefix")
    r, _, _ = mixer(x_rows, (z_x, flog_x), (z_meta, c_meta), batches, "main")
    out = _conv_ffn(r, r_meta, ffn_pre_g, w_up, ffn_conv_w[0], ffn_conv_b, w_down, ffn_post_g,
                    batches=batches, name="conv_ffn")
    return out.reshape(batches, seq, d)
```

```python
import functools
import math

import jax
import jax.numpy as jnp
from jax import lax
from jax.experimental import pallas as pl
from jax.experimental.pallas import tpu as pltpu

HEAD_DIM = 128
POOL_WINDOWS = (2, 4, 8, 16)
HALO_ROWS = 16
EPS = 1e-6
MASK_VALUE = -1e30
F32 = jnp.float32
BF16 = jnp.bfloat16
LANES = 128
LOG2E = math.log2(math.e)
N_SPLIT = 3
SUM_ROWS = 16
VMEM_LIMIT_BYTES = 56 * 1024 * 1024


def _pick_tile(n, preferred):
    t = min(preferred, n)
    while n % t:
        t //= 2
    return t


def _params(semantics, vmem=VMEM_LIMIT_BYTES):
    return pltpu.CompilerParams(dimension_semantics=semantics, vmem_limit_bytes=vmem)


def _rms(x, g):
    ms = jnp.mean(x * x, axis=-1, keepdims=True)
    return x * lax.rsqrt(ms + EPS) * g


def _rms_rows_to(dst_ref, dst_row0, src_ref, g_ref, chunk):
    rows = src_ref.shape[0]
    chunk = min(chunk, rows)

    def body(c, carry):
        r0 = pl.multiple_of(c * chunk, chunk)
        x = src_ref[pl.ds(r0, chunk), :]
        dst_ref[pl.ds(dst_row0 + r0, chunk), :] = _rms(x, g_ref[...]).astype(dst_ref.dtype)
        return carry

    lax.fori_loop(0, rows // chunk, body, 0)


def _norm_matmul_body(x_ref, g_ref, w_ref, b_ref, ws_ref, bs_ref, *rest, n_casts):
    cast_in = rest[:n_casts]
    o_ref, os_ref = rest[n_casts:n_casts + 2]
    cast_out = rest[n_casts + 2:2 * n_casts + 2]
    h_ref = rest[-1]

    @pl.when(pl.program_id(1) == 0)
    def _():
        _rms_rows_to(h_ref, 0, x_ref, g_ref, 128)
        os_ref[...] = lax.dot_general(h_ref[...], ws_ref[...], (((1,), (1,)), ((), ())),
                                      preferred_element_type=F32) + bs_ref[...]

    acc = lax.dot_general(h_ref[...], w_ref[...], (((1,), (1,)), ((), ())),
                          preferred_element_type=F32)
    o_ref[...] = (acc + b_ref[...]).astype(o_ref.dtype)

    for src, dst in zip(cast_in, cast_out):
        dst[...] = src[...].astype(dst.dtype)


def _norm_matmul(x, g, w, b, w_side, b_side, casts=(), *, tm, tn, name):
    m, d = x.shape
    n = w.shape[0]
    ns = w_side.shape[0]
    grid = (m // tm, n // tn)
    n_steps = grid[0] * grid[1]
    cast_specs = []
    for a in casts:
        chunk = next(c for c in range(16, a.shape[0] + 1, 16)
                     if a.shape[0] % c == 0 and a.shape[0] // c <= n_steps)
        last = a.shape[0] // chunk - 1
        cast_specs.append(pl.BlockSpec(
            (chunk, a.shape[1]), lambda i, j, last=last: (jnp.minimum(i * grid[1] + j, last), 0)))
    outs = pl.pallas_call(
        functools.partial(_norm_matmul_body, n_casts=len(casts)),
        grid=grid,
        in_specs=[
            pl.BlockSpec((tm, d), lambda i, j: (i, 0)),
            pl.BlockSpec((1, d), lambda i, j: (0, 0)),
            pl.BlockSpec((tn, d), lambda i, j: (j, 0)),
            pl.BlockSpec((1, tn), lambda i, j: (0, j)),
            pl.BlockSpec((ns, d), lambda i, j: (0, 0)),
            pl.BlockSpec((1, ns), lambda i, j: (0, 0)),
        ] + cast_specs,
        out_specs=[
            pl.BlockSpec((tm, tn), lambda i, j: (i, j)),
            pl.BlockSpec((tm, ns), lambda i, j: (i, 0)),
        ] + cast_specs,
        out_shape=[
            jax.ShapeDtypeStruct((m, n), BF16),
            jax.ShapeDtypeStruct((m, ns), F32),
        ] + [jax.ShapeDtypeStruct(a.shape, BF16) for a in casts],
        scratch_shapes=[pltpu.VMEM((tm, d), BF16)],
        compiler_params=_params(("arbitrary", "arbitrary")),
        name=name,
    )(x, g, w, b, w_side, b_side, *casts)
    return outs[0], outs[1], outs[2:]


def _stage_body(a_ref, b_ref, f_ref, o_ref, of_ref, *, first_shifted, shift):
    j = pl.program_id(0)

    @pl.when(j == 0)
    def _():
        pad = jnp.zeros((of_ref.shape[0] - shift, of_ref.shape[1]), f_ref.dtype)
        of_ref[...] = jnp.concatenate([f_ref[...], pad], axis=0).astype(of_ref.dtype)

    @pl.when(j < first_shifted)
    def _():
        o_ref[...] = a_ref[...].astype(o_ref.dtype)

    @pl.when(j >= first_shifted)
    def _():
        o_ref[...] = jnp.concatenate([a_ref[shift:, :], b_ref[...]], axis=0).astype(o_ref.dtype)


def _stage_in_weights(wt, *, skip_start, skip, name):
    n, d = wt.shape
    tn = _pick_tile(skip_start, 1024)
    n_out = n - skip
    assert skip % 8 == 0 and n_out % tn == 0 and skip_start % tn == 0
    edge = tn // skip
    return pl.pallas_call(
        functools.partial(_stage_body, first_shifted=skip_start // tn, shift=skip),
        grid=(n_out // tn,),
        in_specs=[
            pl.BlockSpec((tn, d), lambda j: (j, 0)),
            pl.BlockSpec((skip, d), lambda j: ((j + 1) * edge, 0)),
            pl.BlockSpec((skip, d), lambda j: (skip_start // skip, 0)),
        ],
        out_specs=[
            pl.BlockSpec((tn, d), lambda j: (j, 0)),
            pl.BlockSpec((LANES, d), lambda j: (0, 0)),
        ],
        out_shape=[
            jax.ShapeDtypeStruct((n_out, d), BF16),
            jax.ShapeDtypeStruct((LANES, d), BF16),
        ],
        compiler_params=_params(("arbitrary",)),
        name=name,
    )(wt, wt, wt)


def _cum_body(f_ref, col_ref, row_ref, *, heads):
    x = f_ref[...]
    rows = x.shape[0]
    c = jnp.minimum(x, 0.0) - jnp.log1p(jnp.exp(-jnp.abs(x)))
    t = lax.broadcasted_iota(jnp.int32, c.shape, 0)
    shift = 1
    while shift < rows:
        c = c + jnp.where(t >= shift, pltpu.roll(c, shift, axis=0), 0.0)
        shift *= 2
    col_ref[...] = c
    row_ref[...] = c.T[:heads, :]


def _forget_cumsum(flog, *, batches, heads, name):
    rows = flog.shape[0] // batches
    return pl.pallas_call(
        functools.partial(_cum_body, heads=heads),
        grid=(batches,),
        in_specs=[pl.BlockSpec((rows, flog.shape[1]), lambda b: (b, 0))],
        out_specs=[
            pl.BlockSpec((rows, flog.shape[1]), lambda b: (b, 0)),
            pl.BlockSpec((None, heads, rows), lambda b: (b, 0, 0)),
        ],
        out_shape=[
            jax.ShapeDtypeStruct(flog.shape, F32),
            jax.ShapeDtypeStruct((batches, heads, rows), F32),
        ],
        compiler_params=_params(("arbitrary",)),
        name=name,
    )(flog)


def _attn_prefix_body(q_ref, k_ref, v_ref, c_ref, o_ref, *, scale):
    n = q_ref.shape[0]
    q = (q_ref[...].astype(F32) * scale).astype(BF16)
    s = lax.dot_general(q, k_ref[...], (((1,), (1,)), ((), ())), preferred_element_type=F32)
    s = s - c_ref[...]
    row = lax.broadcasted_iota(jnp.int32, (n, n), 0)
    col = lax.broadcasted_iota(jnp.int32, (n, n), 1)
    s = jnp.where(row >= col, s, MASK_VALUE)
    p = jnp.exp(s - jnp.max(s, axis=-1, keepdims=True))
    l = jnp.sum(p, axis=-1, keepdims=True)
    acc = jnp.dot(p.astype(BF16), v_ref[...], preferred_element_type=F32)
    o_ref[...] = (acc / l).astype(o_ref.dtype)


def _attention_prefix(z, c_row, *, heads, name):
    n = z.shape[0]
    return pl.pallas_call(
        functools.partial(_attn_prefix_body, scale=HEAD_DIM ** -0.5),
        grid=(heads,),
        in_specs=[
            pl.BlockSpec((n, HEAD_DIM), lambda h: (0, h)),
            pl.BlockSpec((n, HEAD_DIM), lambda h: (0, heads + h)),
            pl.BlockSpec((n, HEAD_DIM), lambda h: (0, 2 * heads + h)),
            pl.BlockSpec((None, 1, n), lambda h: (h, 0, 0)),
        ],
        out_specs=pl.BlockSpec((n, HEAD_DIM), lambda h: (0, h)),
        out_shape=jax.ShapeDtypeStruct((n, heads * HEAD_DIM), BF16),
        compiler_params=_params(("arbitrary",)),
        name=name,
    )(z, z, z, c_row)


def _split3(x):
    p1 = x.astype(BF16)
    r1 = x - p1.astype(F32)
    p2 = r1.astype(BF16)
    p3 = (r1 - p2.astype(F32)).astype(BF16)
    return p1, p2, p3


def _bias_block(x, n_heads):
    lane = lax.broadcasted_iota(jnp.int32, x.shape, 1)
    pieces = _split3(jnp.where(lane < n_heads, x, 0.0))
    block = pieces[0].astype(F32)
    for k in range(1, N_SPLIT):
        block = block + pltpu.roll(pieces[k].astype(F32), k * n_heads, axis=1)
    return block.astype(BF16)


def _attn_main_body(q_ref, k_ref, v_ref, c_ref, kp_ref, vp_ref, cp_ref, o_ref,
                    e_ref, vt_ref, ep_ref, vpt_ref, qaug_ref, m_ref, acc_ref, alpha_ref, p_ref,
                    *, tq, hp, n_heads, scale):
    seq = q_ref.shape[0]
    npre = kp_ref.shape[0]
    group = pl.program_id(1)

    e_ref[...] = _bias_block(c_ref[...] * (-LOG2E), n_heads)
    cp = cp_ref[0:npre, :]
    ep_ref[...] = _bias_block((cp - cp[npre - 1:npre, :]) * (-LOG2E), n_heads)

    def sum_rows(n):
        r = lax.broadcasted_iota(jnp.int32, (SUM_ROWS, n), 0)
        return jnp.where(r == 0, 1.0, 0.0).astype(BF16)

    for hh in range(hp):
        lanes = slice(hh * HEAD_DIM, (hh + 1) * HEAD_DIM)
        vt_ref[hh, 0:HEAD_DIM, :] = v_ref[:, lanes].astype(F32).T.astype(BF16)
        vt_ref[hh, HEAD_DIM:, :] = sum_rows(seq)
        vp = jnp.concatenate([vp_ref[:, lanes].astype(F32),
                              jnp.zeros((LANES - npre, HEAD_DIM), F32)], axis=0)
        vpt_ref[hh, 0:HEAD_DIM, :] = vp.T.astype(BF16)
        vpt_ref[hh, HEAD_DIM:, :] = sum_rows(LANES)

    lane = lax.broadcasted_iota(jnp.int32, (tq, HEAD_DIM), 1)
    key = lax.broadcasted_iota(jnp.int32, (tq, tq), 0)
    qry = lax.broadcasted_iota(jnp.int32, (tq, tq), 1)
    causal = key <= qry
    heads = range(hp)

    def k_aug(hh, k0):
        lanes = slice(hh * HEAD_DIM, (hh + 1) * HEAD_DIM)
        return jnp.concatenate([k_ref[pl.ds(k0, tq), lanes], e_ref[pl.ds(k0, tq), :]], axis=1)

    def scores_t(kaug, hh):
        return lax.dot_general(kaug, qaug_ref[hh], (((1,), (1,)), ((), ())),
                               preferred_element_type=F32)

    def softmax_update(hh, st):
        m = m_ref[hh]
        m_new = jnp.maximum(m, jnp.max(st, axis=0, keepdims=True))
        alpha = jnp.exp2(m - m_new)
        m_ref[hh] = m_new
        return alpha, jnp.exp2((st - m_new).astype(BF16))

    def value_update(hh, k_pend):
        vt = vt_ref[hh, :, pl.ds(k_pend, tq)]
        acc_ref[hh] = alpha_ref[hh] * acc_ref[hh] + jnp.dot(vt, p_ref[hh], preferred_element_type=F32)

    def q_tile(qi, carry_unused):
        q0 = pl.multiple_of(qi * tq, tq)
        for hh in heads:
            lanes = slice(hh * HEAD_DIM, (hh + 1) * HEAD_DIM)
            head = group * hp + hh
            q = (q_ref[pl.ds(q0, tq), lanes].astype(F32) * (scale * LOG2E)).astype(BF16)
            picks = (lane == head) | (lane == n_heads + head) | (lane == 2 * n_heads + head)
            qaug_ref[hh] = jnp.concatenate([q, jnp.where(picks, 1.0, 0.0).astype(BF16)], axis=1)

        sts = []
        for hh in heads:
            lanes = slice(hh * HEAD_DIM, (hh + 1) * HEAD_DIM)
            kp_aug = jnp.concatenate([kp_ref[:, lanes], ep_ref[...]], axis=1)
            sts.append(scores_t(jnp.concatenate([kp_aug, k_aug(hh, q0)], axis=0), hh))
        for hh in heads:
            st_pre = sts[hh][0:npre, :]
            st = jnp.where(causal, sts[hh][npre:, :], MASK_VALUE)
            m = jnp.maximum(jnp.max(st, axis=0, keepdims=True), jnp.max(st_pre, axis=0, keepdims=True))
            m_ref[hh] = m
            p_ref[hh] = jnp.exp2((st - m).astype(BF16))
            alpha_ref[hh] = jnp.ones_like(m)
            p_pre = jnp.concatenate([jnp.exp2((st_pre - m).astype(BF16)),
                                     jnp.zeros((LANES - npre, tq), BF16)], axis=0)
            acc_ref[hh] = jnp.dot(vpt_ref[hh], p_pre, preferred_element_type=F32)

        def kv_tile(j, k_pend):
            k0 = pl.multiple_of(j * tq, tq)
            sts = [scores_t(k_aug(hh, k0), hh) for hh in heads]
            for hh in heads:
                value_update(hh, pl.multiple_of(k_pend, tq))
            for hh in heads:
                alpha_ref[hh], p_ref[hh] = softmax_update(hh, sts[hh])
            return k0

        k_pend = lax.fori_loop(0, qi, kv_tile, q0)

        for hh in heads:
            value_update(hh, pl.multiple_of(k_pend, tq))
        for hh in heads:
            lanes = slice(hh * HEAD_DIM, (hh + 1) * HEAD_DIM)
            acc = acc_ref[hh]
            out = acc[0:HEAD_DIM, :] / acc[HEAD_DIM:HEAD_DIM + 1, :]
            o_ref[pl.ds(q0, tq), lanes] = out.T.astype(o_ref.dtype)
        return carry_unused

    lax.fori_loop(0, seq // tq, q_tile, 0)


def _attention_main(z, c_col, z_prefix, c_col_prefix, *, batches, heads, name):
    seq = z.shape[0] // batches
    npre = z_prefix.shape[0]
    tq = _pick_tile(seq, 256)
    hp = _pick_tile(heads, 8)
    assert N_SPLIT * heads <= LANES
    groups = heads // hp
    width = hp * HEAD_DIM
    return pl.pallas_call(
        functools.partial(_attn_main_body, tq=tq, hp=hp, n_heads=heads, scale=HEAD_DIM ** -0.5),
        grid=(batches, groups),
        in_specs=[
            pl.BlockSpec((seq, width), lambda b, g: (b, g)),
            pl.BlockSpec((seq, width), lambda b, g: (b, groups + g)),
            pl.BlockSpec((seq, width), lambda b, g: (b, 2 * groups + g)),
            pl.BlockSpec((seq, LANES), lambda b, g: (b, 0)),
            pl.BlockSpec((npre, width), lambda b, g: (0, groups + g)),
            pl.BlockSpec((npre, width), lambda b, g: (0, 2 * groups + g)),
            pl.BlockSpec(c_col_prefix.shape, lambda b, g: (0, 0)),
        ],
        out_specs=pl.BlockSpec((seq, width), lambda b, g: (b, g)),
        out_shape=jax.ShapeDtypeStruct((batches * seq, heads * HEAD_DIM), BF16),
        scratch_shapes=[
            pltpu.VMEM((seq, LANES), BF16),
            pltpu.VMEM((hp, HEAD_DIM + SUM_ROWS, seq), BF16),
            pltpu.VMEM((npre, LANES), BF16),
            pltpu.VMEM((hp, HEAD_DIM + SUM_ROWS, LANES), BF16),
            pltpu.VMEM((hp, tq, 2 * HEAD_DIM), BF16),
            pltpu.VMEM((hp, 1, tq), F32),
            pltpu.VMEM((hp, HEAD_DIM + SUM_ROWS, tq), F32),
            pltpu.VMEM((hp, 1, tq), F32),
            pltpu.VMEM((hp, tq, tq), BF16),
        ],
        compiler_params=_params(("arbitrary", "arbitrary")),
        name=name,
    )(z, z, z, c_col, z_prefix, z_prefix, c_col_prefix)


def _pool_body(u_ref, *rest, gdim, pos_offset, tiles_per_batch, has_halo):
    if has_halo:
        uprev_ref, upre_ref, pw_ref, ps_ref, o_ref, ext_ref = rest
    else:
        pw_ref, ps_ref, o_ref, ext_ref = rest
    tm = u_ref.shape[0]
    if has_halo:
        first = (pl.program_id(0) % tiles_per_batch) == 0
        halo = jnp.where(first, upre_ref[...], uprev_ref[...]).astype(F32)
    else:
        halo = jnp.zeros((HALO_ROWS, u_ref.shape[1]), F32)
    ext_ref[0:HALO_ROWS, :] = halo
    ext_ref[HALO_ROWS:HALO_ROWS + tm, :] = u_ref[...].astype(F32)

    for g, w in enumerate(POOL_WINDOWS):
        cols = slice(g * gdim, (g + 1) * gdim)
        wsum = ext_ref[:, cols]
        shift = 1
        while shift < w:
            wsum = wsum + pltpu.roll(wsum, shift, axis=0)
            shift *= 2
        wsum = wsum[HALO_ROWS:, :]
        cur = ext_ref[HALO_ROWS:HALO_ROWS + tm, cols]
        if pos_offset + 1 >= w:
            mean = wsum * (1.0 / w)
        else:
            pos = lax.broadcasted_iota(jnp.int32, (tm, 1), 0) + pos_offset
            mean = wsum / jnp.minimum(pos + 1, w).astype(F32)
        d = (mean - cur).astype(BF16)
        y = jnp.dot(d, pw_ref[g], preferred_element_type=F32) * ps_ref[:, cols]
        o_ref[:, cols] = y.astype(o_ref.dtype)


def _pool(z, pool_w, pool_scale, prefix_z, *, batches, ublock, name):
    groups, gdim, _ = pool_w.shape
    dp = groups * gdim
    rows = z.shape[0]
    seq = rows // batches
    has_halo = prefix_z is not None
    tm = _pick_tile(seq, 512)
    tiles_per_batch = seq // tm
    in_specs = [pl.BlockSpec((tm, dp), lambda i: (i, ublock))]
    args = [z]
    if has_halo:
        halo_blocks = tm // HALO_ROWS
        in_specs += [
            pl.BlockSpec((HALO_ROWS, dp), lambda i: (jnp.maximum(i * halo_blocks - 1, 0), ublock)),
            pl.BlockSpec((HALO_ROWS, dp), lambda i: (prefix_z.shape[0] // HALO_ROWS - 1, ublock)),
        ]
        args += [z, prefix_z]
        pos_offset = prefix_z.shape[0]
    else:
        pos_offset = 0
    in_specs += [
        pl.BlockSpec((groups, gdim, gdim), lambda i: (0, 0, 0)),
        pl.BlockSpec((1, dp), lambda i: (0, 0)),
    ]
    args += [pool_w, pool_scale]
    return pl.pallas_call(
        functools.partial(_pool_body, gdim=gdim, pos_offset=pos_offset,
                          tiles_per_batch=tiles_per_batch, has_halo=has_halo),
        grid=(rows // tm,),
        in_specs=in_specs,
        out_specs=pl.BlockSpec((tm, dp), lambda i: (i, 0)),
        out_shape=jax.ShapeDtypeStruct((rows, dp), BF16),
        scratch_shapes=[pltpu.VMEM((tm + HALO_ROWS, dp), F32)],
        compiler_params=_params(("arbitrary",)),
        name=name,
    )(*args)


def _mix_body(a_ref, p_ref, ga_ref, gp_ref, x_ref, wa_ref, wp_ref, wo_ref, g_ref, o_ref):
    ya = jnp.dot(a_ref[...], wa_ref[...], preferred_element_type=F32)
    yp = jnp.dot(p_ref[...], wp_ref[...], preferred_element_type=F32)
    m = (jax.nn.sigmoid(ga_ref[...].astype(F32)) * ya
         + jax.nn.sigmoid(gp_ref[...].astype(F32)) * yp)
    mo = jnp.dot(m.astype(BF16), wo_ref[...], preferred_element_type=F32)
    o_ref[...] = x_ref[...] + _rms(mo, g_ref[...])


def _mix(attn, pool, z, x, w_attn_o, w_pool_o, w_out, g_post, *, gate_block, name):
    rows, d = x.shape
    da = attn.shape[1]
    dp = pool.shape[1]
    tm = _pick_tile(rows, 256)
    resident = dict(pipeline_mode=pl.Buffered(1))
    return pl.pallas_call(
        _mix_body,
        grid=(rows // tm,),
        in_specs=[
            pl.BlockSpec((tm, da), lambda i: (i, 0)),
            pl.BlockSpec((tm, dp), lambda i: (i, 0)),
            pl.BlockSpec((tm, d), lambda i: (i, gate_block)),
            pl.BlockSpec((tm, d), lambda i: (i, gate_block + 1)),
            pl.BlockSpec((tm, d), lambda i: (i, 0)),
            pl.BlockSpec((da, d), lambda i: (0, 0), **resident),
            pl.BlockSpec((dp, d), lambda i: (0, 0), **resident),
            pl.BlockSpec((d, d), lambda i: (0, 0), **resident),
            pl.BlockSpec((1, d), lambda i: (0, 0)),
        ],
        out_specs=pl.BlockSpec((tm, d), lambda i: (i, 0)),
        out_shape=jax.ShapeDtypeStruct((rows, d), F32),
        compiler_params=_params(("arbitrary",)),
        name=name,
    )(attn, pool, z, z, x, w_attn_o, w_pool_o, w_out, g_post)


def _gelu_tanh(x):
    c = math.sqrt(2.0 / math.pi)
    v = (x * x).astype(F32) * (0.044715 * c) + c
    return (0.5 * x) * (1.0 + jnp.tanh(x * v.astype(x.dtype)))


def _ffn_body(ru_ref, rprev_ref, rpre_ref, rd_ref, gpre_ref, wg_ref, wv_ref, cwg_ref, cwv_ref,
              cbg_ref, cbv_ref, wd_ref, gpost_ref, o_ref, h_ref, gate_ref, act0_ref, act1_ref, *, nf,
              n_steps, tiles_per_batch):
    t = pl.program_id(0)
    tu = jnp.minimum(t, n_steps - 1)
    fu = tu % nf
    iu = tu // nf
    fd = jnp.maximum(t - 1, 0) % nf
    tm = o_ref.shape[0]

    @pl.when(t == 0)
    def _():
        act1_ref[...] = jnp.zeros(act1_ref.shape, act1_ref.dtype)

    @pl.when(jnp.logical_and(fu == 0, t < n_steps))
    def _():
        first = (iu % tiles_per_batch) == 0
        halo = jnp.where(first, rpre_ref[...], rprev_ref[...])
        h_ref[0:HALO_ROWS, :] = _rms(halo, gpre_ref[...]).astype(BF16)
        _rms_rows_to(h_ref, HALO_ROWS, ru_ref, gpre_ref, 128)

    @pl.when(fd == 0)
    def _():
        o_ref[...] = jnp.zeros(o_ref.shape, o_ref.dtype)

    def up_conv(w_ref, cw_ref, cb_ref):
        a = jnp.dot(h_ref[...], w_ref[...], preferred_element_type=F32)
        a0, a1, a2 = [(a if k == 0 else pltpu.roll(a, k, axis=0))[HALO_ROWS:, :].astype(BF16)
                      for k in range(3)]
        cw = cw_ref[...].astype(BF16)
        out = cb_ref[...].astype(BF16) + a2 * cw[0:1, :]
        out = out + a1 * cw[1:2, :]
        out = out + a0 * cw[2:3, :]
        return out

    def step(act_u_ref, act_d_ref):
        gate_ref[...] = up_conv(wg_ref, cwg_ref, cbg_ref)
        val = up_conv(wv_ref, cwv_ref, cbv_ref)
        act_u_ref[...] = _gelu_tanh(gate_ref[...]) * val
        o_ref[...] += jnp.dot(act_d_ref[...], wd_ref[...], preferred_element_type=F32)

    @pl.when(t % 2 == 0)
    def _():
        step(act0_ref, act1_ref)

    @pl.when(t % 2 == 1)
    def _():
        step(act1_ref, act0_ref)

    @pl.when(jnp.logical_and(fd == nf - 1, t > 0))
    def _():
        chunk = min(128, tm)

        def body(c, carry):
            r0 = pl.multiple_of(c * chunk, chunk)
            rows = pl.ds(r0, chunk)
            o_ref[rows, :] = rd_ref[rows, :] + _rms(o_ref[rows, :], gpost_ref[...])
            return carry

        lax.fori_loop(0, tm // chunk, body, 0)


def _conv_ffn(r, r_prefix, g_pre, w_up, conv_w, conv_b, w_down, g_post, *, batches, name):
    rows, d = r.shape
    dff = w_down.shape[0]
    seq = rows // batches
    tm = _pick_tile(seq, 512)
    tf = _pick_tile(dff, 512)
    nf = dff // tf
    assert nf >= 2
    n_steps = (rows // tm) * nf
    tiles_per_batch = seq // tm
    halo_blocks = tm // HALO_ROWS
    kw = conv_w.shape[0]

    def up_item(t):
        tu = jnp.minimum(t, n_steps - 1)
        return tu // nf, tu % nf

    def down_item(t):
        td = jnp.maximum(t - 1, 0)
        return td // nf, td % nf

    return pl.pallas_call(
        functools.partial(_ffn_body, nf=nf, n_steps=n_steps, tiles_per_batch=tiles_per_batch),
        grid=(n_steps + 1,),
        in_specs=[
            pl.BlockSpec((tm, d), lambda t: (up_item(t)[0], 0)),
            pl.BlockSpec((HALO_ROWS, d), lambda t: (jnp.maximum(up_item(t)[0] * halo_blocks - 1, 0), 0)),
            pl.BlockSpec((HALO_ROWS, d), lambda t: (r_prefix.shape[0] // HALO_ROWS - 1, 0)),
            pl.BlockSpec((tm, d), lambda t: (down_item(t)[0], 0)),
            pl.BlockSpec((1, d), lambda t: (0, 0)),
            pl.BlockSpec((d, tf), lambda t: (0, up_item(t)[1])),
            pl.BlockSpec((d, tf), lambda t: (0, nf + up_item(t)[1])),
            pl.BlockSpec((kw, tf), lambda t: (0, up_item(t)[1])),
            pl.BlockSpec((kw, tf), lambda t: (0, nf + up_item(t)[1])),
            pl.BlockSpec((1, tf), lambda t: (0, up_item(t)[1])),
            pl.BlockSpec((1, tf), lambda t: (0, nf + up_item(t)[1])),
            pl.BlockSpec((tf, d), lambda t: (down_item(t)[1], 0)),
            pl.BlockSpec((1, d), lambda t: (0, 0)),
        ],
        out_specs=pl.BlockSpec((tm, d), lambda t: (down_item(t)[0], 0)),
        out_shape=jax.ShapeDtypeStruct((rows, d), F32),
        scratch_shapes=[
            pltpu.VMEM((HALO_ROWS + tm, d), BF16),
            pltpu.VMEM((tm, tf), BF16),
            pltpu.VMEM((tm, tf), BF16),
            pltpu.VMEM((tm, tf), BF16),
        ],
        compiler_params=_params(("arbitrary",)),
        name=name,
    )(r, r, r_prefix, r, g_pre, w_up, w_up, conv_w, conv_w, conv_b, conv_b, w_down, g_post)


def kernel(x, meta_tokens, mix_pre_g, w_in, b_in, w_attn_o, pool_w, pool_scale, w_pool_o, w_out,
           mix_post_g, ffn_pre_g, w_ffn_up, ffn_conv_w, ffn_conv_b, w_ffn_down, ffn_post_g):
    batches, seq, d = x.shape
    n_meta = meta_tokens.shape[0]
    assert w_in.shape[0] == 1, "single-layer block"
    assert n_meta == HALO_ROWS and n_meta >= max(POOL_WINDOWS) and ffn_conv_w.shape[1] <= HALO_ROWS
    d_attn = w_attn_o.shape[1]
    d_pool = w_pool_o.shape[1]
    heads = d_attn // HEAD_DIM
    assert d_attn == d_pool and 2 * d_attn == d
    assert w_in.shape[2] == 3 * d_attn + heads + d_pool + 2 * d

    f0 = 3 * d_attn
    w_t = w_in[0].T
    b = b_in[0]
    w_main, w_f = _stage_in_weights(w_t, skip_start=f0, skip=heads, name="stage_w_in")
    b_main = jnp.concatenate([b[:f0], b[f0 + heads:]])[None, :]
    b_f = jnp.pad(b[f0:f0 + heads], (0, LANES - heads))[None, :]
    pw = pool_w[0].astype(BF16)
    later_weights = (w_attn_o[0], w_pool_o[0], w_out[0], w_ffn_up[0], w_ffn_down[0])
    ublock = f0 // d_pool
    gate_block = (f0 + d_pool) // d

    def in_project(rows_in, nb, tag, casts=()):
        s = rows_in.shape[0] // nb
        return _norm_matmul(rows_in, mix_pre_g, w_main, b_main, w_f, b_f, casts,
                            tm=_pick_tile(s, 1024), tn=_pick_tile(w_main.shape[0], 1024),
                            name=f"inproj_{tag}")

    def mixer(rows_in, projected, prefix, nb, tag):
        n, _ = rows_in.shape
        z, flog = projected
        if prefix is None:
            flog = jnp.pad(flog, ((0, LANES - n), (0, 0)))
            c_col, c_row = _forget_cumsum(flog, batches=nb, heads=heads, name=f"cumsum_{tag}")
            attn = _attention_prefix(z, c_row[0, :, :n].reshape(heads, 1, n), heads=heads,
                                     name=f"attention_{tag}")
        else:
            c_col, _ = _forget_cumsum(flog, batches=nb, heads=heads, name=f"cumsum_{tag}")
            attn = _attention_main(z, c_col, prefix[0], prefix[1], batches=nb, heads=heads,
                                   name=f"attention_{tag}")
        pool = _pool(z, pw, pool_scale, None if prefix is None else prefix[0],
                     batches=nb, ublock=ublock, name=f"pool_{tag}")
        r = _mix(attn, pool, z, rows_in, wa, wp, wo, mix_post_g,
                 gate_block=gate_block, name=f"mix_{tag}")
        return r, z, c_col

    x_rows = x.reshape(batches * seq, d)
    meta_rows = meta_tokens.astype(x.dtype)
    z_x, flog_x, (wa, wp, wo, w_up, w_down) = in_project(x_rows, batches, "main", later_weights)
    z_m, flog_m, _ = in_project(meta_rows, 1, "prefix")
    r_meta, z_meta, c_meta = mixer(meta_rows, (z_m, flog_m), None, 1, "prefix")
    r, _, _ = mixer(x_rows, (z_x, flog_x), (z_meta, c_meta), batches, "main")
    out = _conv_ffn(r, r_meta, ffn_pre_g, w_up, ffn_conv_w[0], ffn_conv_b, w_down, ffn_post_g,
                    batches=batches, name="conv_ffn")
    return out.reshape(batches, seq, d)
```

```python
import functools
import math

import jax
import jax.numpy as jnp
from jax import lax
from jax.experimental import pallas as pl
from jax.experimental.pallas import tpu as pltpu

HEAD_DIM = 128
POOL_WINDOWS = (2, 4, 8, 16)
HALO_ROWS = 16
EPS = 1e-6
MASK_VALUE = -1e30
F32 = jnp.float32
BF16 = jnp.bfloat16
LANES = 128
LOG2E = math.log2(math.e)
N_SPLIT = 3
SUM_ROWS = 16
VMEM_LIMIT_BYTES = 56 * 1024 * 1024


def _pick_tile(n, preferred):
    t = min(preferred, n)
    while n % t:
        t //= 2
    return t


def _params(semantics, vmem=VMEM_LIMIT_BYTES):
    return pltpu.CompilerParams(dimension_semantics=semantics, vmem_limit_bytes=vmem)


def _rms(x, g):
    ms = jnp.mean(x * x, axis=-1, keepdims=True)
    return x * lax.rsqrt(ms + EPS) * g


def _rms_rows_to(dst_ref, dst_row0, src_ref, g_ref, chunk):
    rows = src_ref.shape[0]
    chunk = min(chunk, rows)

    def body(c, carry):
        r0 = pl.multiple_of(c * chunk, chunk)
        x = src_ref[pl.ds(r0, chunk), :]
        dst_ref[pl.ds(dst_row0 + r0, chunk), :] = _rms(x, g_ref[...]).astype(dst_ref.dtype)
        return carry

    lax.fori_loop(0, rows // chunk, body, 0)


def _norm_matmul_body(x_ref, g_ref, w_ref, b_ref, ws_ref, bs_ref, *rest, n_casts):
    cast_in = rest[:n_casts]
    o_ref, os_ref = rest[n_casts:n_casts + 2]
    cast_out = rest[n_casts + 2:2 * n_casts + 2]
    h_ref = rest[-1]

    @pl.when(pl.program_id(1) == 0)
    def _():
        _rms_rows_to(h_ref, 0, x_ref, g_ref, 128)
        os_ref[...] = lax.dot_general(h_ref[...], ws_ref[...], (((1,), (1,)), ((), ())),
                                      preferred_element_type=F32) + bs_ref[...]

    acc = lax.dot_general(h_ref[...], w_ref[...], (((1,), (1,)), ((), ())),
                          preferred_element_type=F32)
    o_ref[...] = (acc + b_ref[...]).astype(o_ref.dtype)

    for src, dst in zip(cast_in, cast_out):
        dst[...] = src[...].astype(dst.dtype)


def _norm_matmul(x, g, w, b, w_side, b_side, casts=(), *, tm, tn, name):
    m, d = x.shape
    n = w.shape[0]
    ns = w_side.shape[0]
    grid = (m // tm, n // tn)
    n_steps = grid[0] * grid[1]
    cast_specs = []
    for a in casts:
        chunk = next(c for c in range(16, a.shape[0] + 1, 16)
                     if a.shape[0] % c == 0 and a.shape[0] // c <= n_steps)
        last = a.shape[0] // chunk - 1
        cast_specs.append(pl.BlockSpec(
            (chunk, a.shape[1]), lambda i, j, last=last: (jnp.minimum(i * grid[1] + j, last), 0)))
    outs = pl.pallas_call(
        functools.partial(_norm_matmul_body, n_casts=len(casts)),
        grid=grid,
        in_specs=[
            pl.BlockSpec((tm, d), lambda i, j: (i, 0)),
            pl.BlockSpec((1, d), lambda i, j: (0, 0)),
            pl.BlockSpec((tn, d), lambda i, j: (j, 0)),
            pl.BlockSpec((1, tn), lambda i, j: (0, j)),
            pl.BlockSpec((ns, d), lambda i, j: (0, 0)),
            pl.BlockSpec((1, ns), lambda i, j: (0, 0)),
        ] + cast_specs,
        out_specs=[
            pl.BlockSpec((tm, tn), lambda i, j: (i, j)),
            pl.BlockSpec((tm, ns), lambda i, j: (i, 0)),
        ] + cast_specs,
        out_shape=[
            jax.ShapeDtypeStruct((m, n), BF16),
            jax.ShapeDtypeStruct((m, ns), F32),
        ] + [jax.ShapeDtypeStruct(a.shape, BF16) for a in casts],
        scratch_shapes=[pltpu.VMEM((tm, d), BF16)],
        compiler_params=_params(("arbitrary", "arbitrary")),
        name=name,
    )(x, g, w, b, w_side, b_side, *casts)
    return outs[0], outs[1], outs[2:]


def _stage_body(a_ref, b_ref, f_ref, x_ref, g_ref, bias_ref, bias_f_ref, o_ref, of_ref, z_ref, zf_ref,
                h_ref, *, first_shifted, shift):
    j = pl.program_id(0)
    nt = (((1,), (1,)), ((), ()))

    @pl.when(j == 0)
    def _():
        pad = jnp.zeros((of_ref.shape[0] - shift, of_ref.shape[1]), f_ref.dtype)
        wf = jnp.concatenate([f_ref[...], pad], axis=0).astype(of_ref.dtype)
        of_ref[...] = wf
        h_ref[...] = _rms(x_ref[...], g_ref[...]).astype(h_ref.dtype)
        zf_ref[...] = lax.dot_general(h_ref[...], wf, nt, preferred_element_type=F32) + bias_f_ref[...]

    def emit(w_tile):
        o_ref[...] = w_tile
        z = lax.dot_general(h_ref[...], w_tile, nt, preferred_element_type=F32) + bias_ref[...]
        z_ref[...] = z.astype(z_ref.dtype)

    @pl.when(j < first_shifted)
    def _():
        emit(a_ref[...].astype(o_ref.dtype))

    @pl.when(j >= first_shifted)
    def _():
        emit(jnp.concatenate([a_ref[shift:, :], b_ref[...]], axis=0).astype(o_ref.dtype))


def _stage_in_weights(wt, x_prefix, g, bias, bias_f, *, skip_start, skip, name):
    n, d = wt.shape
    npre = x_prefix.shape[0]
    tn = _pick_tile(skip_start, 1024)
    n_out = n - skip
    assert skip % 8 == 0 and n_out % tn == 0 and skip_start % tn == 0
    edge = tn // skip
    return pl.pallas_call(
        functools.partial(_stage_body, first_shifted=skip_start // tn, shift=skip),
        grid=(n_out // tn,),
        in_specs=[
            pl.BlockSpec((tn, d), lambda j: (j, 0)),
            pl.BlockSpec((skip, d), lambda j: ((j + 1) * edge, 0)),
            pl.BlockSpec((skip, d), lambda j: (skip_start // skip, 0)),
            pl.BlockSpec((npre, d), lambda j: (0, 0)),
            pl.BlockSpec((1, d), lambda j: (0, 0)),
            pl.BlockSpec((1, tn), lambda j: (0, j)),
            pl.BlockSpec((1, LANES), lambda j: (0, 0)),
        ],
        out_specs=[
            pl.BlockSpec((tn, d), lambda j: (j, 0)),
            pl.BlockSpec((LANES, d), lambda j: (0, 0)),
            pl.BlockSpec((npre, tn), lambda j: (0, j)),
            pl.BlockSpec((npre, LANES), lambda j: (0, 0)),
        ],
        out_shape=[
            jax.ShapeDtypeStruct((n_out, d), BF16),
            jax.ShapeDtypeStruct((LANES, d), BF16),
            jax.ShapeDtypeStruct((npre, n_out), BF16),
            jax.ShapeDtypeStruct((npre, LANES), F32),
        ],
        scratch_shapes=[pltpu.VMEM((npre, d), BF16)],
        compiler_params=_params(("arbitrary",)),
        name=name,
    )(wt, wt, wt, x_prefix, g, bias, bias_f)


def _cum_body(f_ref, col_ref, row_ref, *, heads):
    x = f_ref[...]
    rows = x.shape[0]
    c = jnp.minimum(x, 0.0) - jnp.log1p(jnp.exp(-jnp.abs(x)))
    t = lax.broadcasted_iota(jnp.int32, c.shape, 0)
    shift = 1
    while shift < rows:
        c = c + jnp.where(t >= shift, pltpu.roll(c, shift, axis=0), 0.0)
        shift *= 2
    col_ref[...] = c
    row_ref[...] = c.T[:heads, :]


def _forget_cumsum(flog, *, batches, heads, name):
    rows = flog.shape[0] // batches
    return pl.pallas_call(
        functools.partial(_cum_body, heads=heads),
        grid=(batches,),
        in_specs=[pl.BlockSpec((rows, flog.shape[1]), lambda b: (b, 0))],
        out_specs=[
            pl.BlockSpec((rows, flog.shape[1]), lambda b: (b, 0)),
            pl.BlockSpec((None, heads, rows), lambda b: (b, 0, 0)),
        ],
        out_shape=[
            jax.ShapeDtypeStruct(flog.shape, F32),
            jax.ShapeDtypeStruct((batches, heads, rows), F32),
        ],
        compiler_params=_params(("arbitrary",)),
        name=name,
    )(flog)


def _attn_prefix_body(q_ref, k_ref, v_ref, c_ref, o_ref, *, scale):
    n = q_ref.shape[0]
    q = (q_ref[...].astype(F32) * scale).astype(BF16)
    s = lax.dot_general(q, k_ref[...], (((1,), (1,)), ((), ())), preferred_element_type=F32)
    s = s - c_ref[...]
    row = lax.broadcasted_iota(jnp.int32, (n, n), 0)
    col = lax.broadcasted_iota(jnp.int32, (n, n), 1)
    s = jnp.where(row >= col, s, MASK_VALUE)
    p = jnp.exp(s - jnp.max(s, axis=-1, keepdims=True))
    l = jnp.sum(p, axis=-1, keepdims=True)
    acc = jnp.dot(p.astype(BF16), v_ref[...], preferred_element_type=F32)
    o_ref[...] = (acc / l).astype(o_ref.dtype)


def _attention_prefix(z, c_row, *, heads, name):
    n = z.shape[0]
    return pl.pallas_call(
        functools.partial(_attn_prefix_body, scale=HEAD_DIM ** -0.5),
        grid=(heads,),
        in_specs=[
            pl.BlockSpec((n, HEAD_DIM), lambda h: (0, h)),
            pl.BlockSpec((n, HEAD_DIM), lambda h: (0, heads + h)),
            pl.BlockSpec((n, HEAD_DIM), lambda h: (0, 2 * heads + h)),
            pl.BlockSpec((None, 1, n), lambda h: (h, 0, 0)),
        ],
        out_specs=pl.BlockSpec((n, HEAD_DIM), lambda h: (0, h)),
        out_shape=jax.ShapeDtypeStruct((n, heads * HEAD_DIM), BF16),
        compiler_params=_params(("arbitrary",)),
        name=name,
    )(z, z, z, c_row)


def _split3(x):
    p1 = x.astype(BF16)
    r1 = x - p1.astype(F32)
    p2 = r1.astype(BF16)
    p3 = (r1 - p2.astype(F32)).astype(BF16)
    return p1, p2, p3


def _bias_block(x, n_heads):
    lane = lax.broadcasted_iota(jnp.int32, x.shape, 1)
    pieces = _split3(jnp.where(lane < n_heads, x, 0.0))
    block = pieces[0].astype(F32)
    for k in range(1, N_SPLIT):
        block = block + pltpu.roll(pieces[k].astype(F32), k * n_heads, axis=1)
    return block.astype(BF16)


def _attn_main_body(q_ref, k_ref, v_ref, c_ref, kp_ref, vp_ref, cp_ref, o_ref,
                    e_ref, vt_ref, ep_ref, vpt_ref, qaug_ref, m_ref, acc_ref, alpha_ref, p_ref,
                    *, tq, hp, n_heads, scale):
    seq = q_ref.shape[0]
    npre = kp_ref.shape[0]
    group = pl.program_id(1)

    e_ref[...] = _bias_block(c_ref[...] * (-LOG2E), n_heads)
    cp = cp_ref[0:npre, :]
    ep_ref[...] = _bias_block((cp - cp[npre - 1:npre, :]) * (-LOG2E), n_heads)

    def sum_rows(n):
        r = lax.broadcasted_iota(jnp.int32, (SUM_ROWS, n), 0)
        return jnp.where(r == 0, 1.0, 0.0).astype(BF16)

    for hh in range(hp):
        lanes = slice(hh * HEAD_DIM, (hh + 1) * HEAD_DIM)
        vt_ref[hh, 0:HEAD_DIM, :] = v_ref[:, lanes].astype(F32).T.astype(BF16)
        vt_ref[hh, HEAD_DIM:, :] = sum_rows(seq)
        vp = jnp.concatenate([vp_ref[:, lanes].astype(F32),
                              jnp.zeros((LANES - npre, HEAD_DIM), F32)], axis=0)
        vpt_ref[hh, 0:HEAD_DIM, :] = vp.T.astype(BF16)
        vpt_ref[hh, HEAD_DIM:, :] = sum_rows(LANES)

    lane = lax.broadcasted_iota(jnp.int32, (tq, HEAD_DIM), 1)
    key = lax.broadcasted_iota(jnp.int32, (tq, tq), 0)
    qry = lax.broadcasted_iota(jnp.int32, (tq, tq), 1)
    causal = key <= qry
    heads = range(hp)

    def k_aug(hh, k0):
        lanes = slice(hh * HEAD_DIM, (hh + 1) * HEAD_DIM)
        return jnp.concatenate([k_ref[pl.ds(k0, tq), lanes], e_ref[pl.ds(k0, tq), :]], axis=1)

    def scores_t(kaug, hh):
        return lax.dot_general(kaug, qaug_ref[hh], (((1,), (1,)), ((), ())),
                               preferred_element_type=F32)

    def softmax_update(hh, st):
        m = m_ref[hh]
        m_new = jnp.maximum(m, jnp.max(st, axis=0, keepdims=True))
        alpha = jnp.exp2(m - m_new)
        m_ref[hh] = m_new
        return alpha, jnp.exp2((st - m_new).astype(BF16))

    def value_update(hh, k_pend):
        vt = vt_ref[hh, :, pl.ds(k_pend, tq)]
        acc_ref[hh] = alpha_ref[hh] * acc_ref[hh] + jnp.dot(vt, p_ref[hh], preferred_element_type=F32)

    def q_tile(qi, carry_unused):
        q0 = pl.multiple_of(qi * tq, tq)
        for hh in heads:
            lanes = slice(hh * HEAD_DIM, (hh + 1) * HEAD_DIM)
            head = group * hp + hh
            q = (q_ref[pl.ds(q0, tq), lanes].astype(F32) * (scale * LOG2E)).astype(BF16)
            picks = (lane == head) | (lane == n_heads + head) | (lane == 2 * n_heads + head)
            qaug_ref[hh] = jnp.concatenate([q, jnp.where(picks, 1.0, 0.0).astype(BF16)], axis=1)

        sts = []
        for hh in heads:
            lanes = slice(hh * HEAD_DIM, (hh + 1) * HEAD_DIM)
            kp_aug = jnp.concatenate([kp_ref[:, lanes], ep_ref[...]], axis=1)
            sts.append(scores_t(jnp.concatenate([kp_aug, k_aug(hh, q0)], axis=0), hh))
        for hh in heads:
            st_pre = sts[hh][0:npre, :]
            st = jnp.where(causal, sts[hh][npre:, :], MASK_VALUE)
            m = jnp.maximum(jnp.max(st, axis=0, keepdims=True), jnp.max(st_pre, axis=0, keepdims=True))
            m_ref[hh] = m
            p_ref[hh] = jnp.exp2((st - m).astype(BF16))
            alpha_ref[hh] = jnp.ones_like(m)
            p_pre = jnp.concatenate([jnp.exp2((st_pre - m).astype(BF16)),
                                     jnp.zeros((LANES - npre, tq), BF16)], axis=0)
            acc_ref[hh] = jnp.dot(vpt_ref[hh], p_pre, preferred_element_type=F32)

        def kv_tile(j, k_pend):
            k0 = pl.multiple_of(j * tq, tq)
            sts = [scores_t(k_aug(hh, k0), hh) for hh in heads]
            for hh in heads:
                value_update(hh, pl.multiple_of(k_pend, tq))
            for hh in heads:
                alpha_ref[hh], p_ref[hh] = softmax_update(hh, sts[hh])
            return k0

        k_pend = lax.fori_loop(0, qi, kv_tile, q0)

        for hh in heads:
            value_update(hh, pl.multiple_of(k_pend, tq))
        for hh in heads:
            lanes = slice(hh * HEAD_DIM, (hh + 1) * HEAD_DIM)
            acc = acc_ref[hh]
            out = acc[0:HEAD_DIM, :] / acc[HEAD_DIM:HEAD_DIM + 1, :]
            o_ref[pl.ds(q0, tq), lanes] = out.T.astype(o_ref.dtype)
        return carry_unused

    lax.fori_loop(0, seq // tq, q_tile, 0)


def _attention_main(z, c_col, z_prefix, c_col_prefix, *, batches, heads, name):
    seq = z.shape[0] // batches
    npre = z_prefix.shape[0]
    tq = _pick_tile(seq, 256)
    hp = _pick_tile(heads, 8)
    assert N_SPLIT * heads <= LANES
    groups = heads // hp
    width = hp * HEAD_DIM
    return pl.pallas_call(
        functools.partial(_attn_main_body, tq=tq, hp=hp, n_heads=heads, scale=HEAD_DIM ** -0.5),
        grid=(batches, groups),
        in_specs=[
            pl.BlockSpec((seq, width), lambda b, g: (b, g)),
            pl.BlockSpec((seq, width), lambda b, g: (b, groups + g)),
            pl.BlockSpec((seq, width), lambda b, g: (b, 2 * groups + g)),
            pl.BlockSpec((seq, LANES), lambda b, g: (b, 0)),
            pl.BlockSpec((npre, width), lambda b, g: (0, groups + g)),
            pl.BlockSpec((npre, width), lambda b, g: (0, 2 * groups + g)),
            pl.BlockSpec(c_col_prefix.shape, lambda b, g: (0, 0)),
        ],
        out_specs=pl.BlockSpec((seq, width), lambda b, g: (b, g)),
        out_shape=jax.ShapeDtypeStruct((batches * seq, heads * HEAD_DIM), BF16),
        scratch_shapes=[
            pltpu.VMEM((seq, LANES), BF16),
            pltpu.VMEM((hp, HEAD_DIM + SUM_ROWS, seq), BF16),
            pltpu.VMEM((npre, LANES), BF16),
            pltpu.VMEM((hp, HEAD_DIM + SUM_ROWS, LANES), BF16),
            pltpu.VMEM((hp, tq, 2 * HEAD_DIM), BF16),
            pltpu.VMEM((hp, 1, tq), F32),
            pltpu.VMEM((hp, HEAD_DIM + SUM_ROWS, tq), F32),
            pltpu.VMEM((hp, 1, tq), F32),
            pltpu.VMEM((hp, tq, tq), BF16),
        ],
        compiler_params=_params(("arbitrary", "arbitrary")),
        name=name,
    )(z, z, z, c_col, z_prefix, z_prefix, c_col_prefix)


def _pool_body(u_ref, *rest, gdim, pos_offset, tiles_per_batch, has_halo):
    if has_halo:
        uprev_ref, upre_ref, pw_ref, ps_ref, o_ref, ext_ref = rest
    else:
        pw_ref, ps_ref, o_ref, ext_ref = rest
    tm = u_ref.shape[0]
    if has_halo:
        first = (pl.program_id(0) % tiles_per_batch) == 0
        halo = jnp.where(first, upre_ref[...], uprev_ref[...]).astype(F32)
    else:
        halo = jnp.zeros((HALO_ROWS, u_ref.shape[1]), F32)
    ext_ref[0:HALO_ROWS, :] = halo
    ext_ref[HALO_ROWS:HALO_ROWS + tm, :] = u_ref[...].astype(F32)

    for g, w in enumerate(POOL_WINDOWS):
        cols = slice(g * gdim, (g + 1) * gdim)
        wsum = ext_ref[:, cols]
        shift = 1
        while shift < w:
            wsum = wsum + pltpu.roll(wsum, shift, axis=0)
            shift *= 2
        wsum = wsum[HALO_ROWS:, :]
        cur = ext_ref[HALO_ROWS:HALO_ROWS + tm, cols]
        if pos_offset + 1 >= w:
            mean = wsum * (1.0 / w)
        else:
            pos = lax.broadcasted_iota(jnp.int32, (tm, 1), 0) + pos_offset
            mean = wsum / jnp.minimum(pos + 1, w).astype(F32)
        d = (mean - cur).astype(BF16)
        y = jnp.dot(d, pw_ref[g], preferred_element_type=F32) * ps_ref[:, cols]
        o_ref[:, cols] = y.astype(o_ref.dtype)


def _pool(z, pool_w, pool_scale, prefix_z, *, batches, ublock, name):
    groups, gdim, _ = pool_w.shape
    dp = groups * gdim
    rows = z.shape[0]
    seq = rows // batches
    has_halo = prefix_z is not None
    tm = _pick_tile(seq, 512)
    tiles_per_batch = seq // tm
    in_specs = [pl.BlockSpec((tm, dp), lambda i: (i, ublock))]
    args = [z]
    if has_halo:
        halo_blocks = tm // HALO_ROWS
        in_specs += [
            pl.BlockSpec((HALO_ROWS, dp), lambda i: (jnp.maximum(i * halo_blocks - 1, 0), ublock)),
            pl.BlockSpec((HALO_ROWS, dp), lambda i: (prefix_z.shape[0] // HALO_ROWS - 1, ublock)),
        ]
        args += [z, prefix_z]
        pos_offset = prefix_z.shape[0]
    else:
        pos_offset = 0
    in_specs += [
        pl.BlockSpec((groups, gdim, gdim), lambda i: (0, 0, 0)),
        pl.BlockSpec((1, dp), lambda i: (0, 0)),
    ]
    args += [pool_w, pool_scale]
    return pl.pallas_call(
        functools.partial(_pool_body, gdim=gdim, pos_offset=pos_offset,
                          tiles_per_batch=tiles_per_batch, has_halo=has_halo),
        grid=(rows // tm,),
        in_specs=in_specs,
        out_specs=pl.BlockSpec((tm, dp), lambda i: (i, 0)),
        out_shape=jax.ShapeDtypeStruct((rows, dp), BF16),
        scratch_shapes=[pltpu.VMEM((tm + HALO_ROWS, dp), F32)],
        compiler_params=_params(("arbitrary",)),
        name=name,
    )(*args)


def _mix_body(a_ref, p_ref, ga_ref, gp_ref, x_ref, wa_ref, wp_ref, wo_ref, g_ref, o_ref):
    ya = jnp.dot(a_ref[...], wa_ref[...], preferred_element_type=F32)
    yp = jnp.dot(p_ref[...], wp_ref[...], preferred_element_type=F32)
    m = (jax.nn.sigmoid(ga_ref[...].astype(F32)) * ya
         + jax.nn.sigmoid(gp_ref[...].astype(F32)) * yp)
    mo = jnp.dot(m.astype(BF16), wo_ref[...], preferred_element_type=F32)
    o_ref[...] = x_ref[...] + _rms(mo, g_ref[...])


def _mix(attn, pool, z, x, w_attn_o, w_pool_o, w_out, g_post, *, gate_block, name):
    rows, d = x.shape
    da = attn.shape[1]
    dp = pool.shape[1]
    tm = _pick_tile(rows, 256)
    resident = dict(pipeline_mode=pl.Buffered(1))
    return pl.pallas_call(
        _mix_body,
        grid=(rows // tm,),
        in_specs=[
            pl.BlockSpec((tm, da), lambda i: (i, 0)),
            pl.BlockSpec((tm, dp), lambda i: (i, 0)),
            pl.BlockSpec((tm, d), lambda i: (i, gate_block)),
            pl.BlockSpec((tm, d), lambda i: (i, gate_block + 1)),
            pl.BlockSpec((tm, d), lambda i: (i, 0)),
            pl.BlockSpec((da, d), lambda i: (0, 0), **resident),
            pl.BlockSpec((dp, d), lambda i: (0, 0), **resident),
            pl.BlockSpec((d, d), lambda i: (0, 0), **resident),
            pl.BlockSpec((1, d), lambda i: (0, 0)),
        ],
        out_specs=pl.BlockSpec((tm, d), lambda i: (i, 0)),
        out_shape=jax.ShapeDtypeStruct((rows, d), F32),
        compiler_params=_params(("arbitrary",)),
        name=name,
    )(attn, pool, z, z, x, w_attn_o, w_pool_o, w_out, g_post)


def _gelu_tanh(x):
    c = math.sqrt(2.0 / math.pi)
    v = (x * x).astype(F32) * (0.044715 * c) + c
    return (0.5 * x) * (1.0 + jnp.tanh(x * v.astype(x.dtype)))


def _ffn_body(ru_ref, rprev_ref, rpre_ref, rd_ref, gpre_ref, wg_ref, wv_ref, cwg_ref, cwv_ref,
              cbg_ref, cbv_ref, wd_ref, gpost_ref, o_ref, h_ref, gate_ref, act0_ref, act1_ref, *, nf,
              n_steps, tiles_per_batch):
    t = pl.program_id(0)
    tu = jnp.minimum(t, n_steps - 1)
    fu = tu % nf
    iu = tu // nf
    fd = jnp.maximum(t - 1, 0) % nf
    tm = o_ref.shape[0]

    @pl.when(t == 0)
    def _():
        act1_ref[...] = jnp.zeros(act1_ref.shape, act1_ref.dtype)

    @pl.when(jnp.logical_and(fu == 0, t < n_steps))
    def _():
        first = (iu % tiles_per_batch) == 0
        halo = jnp.where(first, rpre_ref[...], rprev_ref[...])
        h_ref[0:HALO_ROWS, :] = _rms(halo, gpre_ref[...]).astype(BF16)
        _rms_rows_to(h_ref, HALO_ROWS, ru_ref, gpre_ref, 128)

    @pl.when(fd == 0)
    def _():
        o_ref[...] = jnp.zeros(o_ref.shape, o_ref.dtype)

    def up_conv(w_ref, cw_ref, cb_ref):
        a = jnp.dot(h_ref[...], w_ref[...], preferred_element_type=F32)
        a0, a1, a2 = [(a if k == 0 else pltpu.roll(a, k, axis=0))[HALO_ROWS:, :].astype(BF16)
                      for k in range(3)]
        cw = cw_ref[...].astype(BF16)
        out = cb_ref[...].astype(BF16) + a2 * cw[0:1, :]
        out = out + a1 * cw[1:2, :]
        out = out + a0 * cw[2:3, :]
        return out

    def step(act_u_ref, act_d_ref):
        gate_ref[...] = up_conv(wg_ref, cwg_ref, cbg_ref)
        val = up_conv(wv_ref, cwv_ref, cbv_ref)
        act_u_ref[...] = _gelu_tanh(gate_ref[...]) * val
        o_ref[...] += jnp.dot(act_d_ref[...], wd_ref[...], preferred_element_type=F32)

    @pl.when(t % 2 == 0)
    def _():
        step(act0_ref, act1_ref)

    @pl.when(t % 2 == 1)
    def _():
        step(act1_ref, act0_ref)

    @pl.when(jnp.logical_and(fd == nf - 1, t > 0))
    def _():
        chunk = min(128, tm)

        def body(c, carry):
            r0 = pl.multiple_of(c * chunk, chunk)
            rows = pl.ds(r0, chunk)
            o_ref[rows, :] = rd_ref[rows, :] + _rms(o_ref[rows, :], gpost_ref[...])
            return carry

        lax.fori_loop(0, tm // chunk, body, 0)


def _conv_ffn(r, r_prefix, g_pre, w_up, conv_w, conv_b, w_down, g_post, *, batches, name):
    rows, d = r.shape
    dff = w_down.shape[0]
    seq = rows // batches
    tm = _pick_tile(seq, 512)
    tf = _pick_tile(dff, 512)
    nf = dff // tf
    assert nf >= 2
    n_steps = (rows // tm) * nf
    tiles_per_batch = seq // tm
    halo_blocks = tm // HALO_ROWS
    kw = conv_w.shape[0]

    def up_item(t):
        tu = jnp.minimum(t, n_steps - 1)
        return tu // nf, tu % nf

    def down_item(t):
        td = jnp.maximum(t - 1, 0)
        return td // nf, td % nf

    return pl.pallas_call(
        functools.partial(_ffn_body, nf=nf, n_steps=n_steps, tiles_per_batch=tiles_per_batch),
        grid=(n_steps + 1,),
        in_specs=[
            pl.BlockSpec((tm, d), lambda t: (up_item(t)[0], 0)),
            pl.BlockSpec((HALO_ROWS, d), lambda t: (jnp.maximum(up_item(t)[0] * halo_blocks - 1, 0), 0)),
            pl.BlockSpec((HALO_ROWS, d), lambda t: (r_prefix.shape[0] // HALO_ROWS - 1, 0)),
            pl.BlockSpec((tm, d), lambda t: (down_item(t)[0], 0)),
            pl.BlockSpec((1, d), lambda t: (0, 0)),
            pl.BlockSpec((d, tf), lambda t: (0, up_item(t)[1])),
            pl.BlockSpec((d, tf), lambda t: (0, nf + up_item(t)[1])),
            pl.BlockSpec((kw, tf), lambda t: (0, up_item(t)[1])),
            pl.BlockSpec((kw, tf), lambda t: (0, nf + up_item(t)[1])),
            pl.BlockSpec((1, tf), lambda t: (0, up_item(t)[1])),
            pl.BlockSpec((1, tf), lambda t: (0, nf + up_item(t)[1])),
            pl.BlockSpec((tf, d), lambda t: (down_item(t)[1], 0)),
            pl.BlockSpec((1, d), lambda t: (0, 0)),
        ],
        out_specs=pl.BlockSpec((tm, d), lambda t: (down_item(t)[0], 0)),
        out_shape=jax.ShapeDtypeStruct((rows, d), F32),
        scratch_shapes=[
            pltpu.VMEM((HALO_ROWS + tm, d), BF16),
            pltpu.VMEM((tm, tf), BF16),
            pltpu.VMEM((tm, tf), BF16),
            pltpu.VMEM((tm, tf), BF16),
        ],
        compiler_params=_params(("arbitrary",)),
        name=name,
    )(r, r, r_prefix, r, g_pre, w_up, w_up, conv_w, conv_w, conv_b, conv_b, w_down, g_post)


def kernel(x, meta_tokens, mix_pre_g, w_in, b_in, w_attn_o, pool_w, pool_scale, w_pool_o, w_out,
           mix_post_g, ffn_pre_g, w_ffn_up, ffn_conv_w, ffn_conv_b, w_ffn_down, ffn_post_g):
    batches, seq, d = x.shape
    n_meta = meta_tokens.shape[0]
    assert w_in.shape[0] == 1, "single-layer block"
    assert n_meta == HALO_ROWS and n_meta >= max(POOL_WINDOWS) and ffn_conv_w.shape[1] <= HALO_ROWS
    d_attn = w_attn_o.shape[1]
    d_pool = w_pool_o.shape[1]
    heads = d_attn // HEAD_DIM
    assert d_attn == d_pool and 2 * d_attn == d
    assert w_in.shape[2] == 3 * d_attn + heads + d_pool + 2 * d

    f0 = 3 * d_attn
    w_t = w_in[0].T
    b = b_in[0]
    b_main = jnp.concatenate([b[:f0], b[f0 + heads:]])[None, :]
    b_f = jnp.pad(b[f0:f0 + heads], (0, LANES - heads))[None, :]
    meta_rows = meta_tokens.astype(x.dtype)
    w_main, w_f, z_m, flog_m = _stage_in_weights(w_t, meta_rows, mix_pre_g, b_main, b_f,
                                                  skip_start=f0, skip=heads, name="stage_w_in")
    pw = pool_w[0].astype(BF16)
    later_weights = (w_attn_o[0], w_pool_o[0], w_out[0], w_ffn_up[0], w_ffn_down[0])
    ublock = f0 // d_pool
    gate_block = (f0 + d_pool) // d

    def in_project(rows_in, nb, tag, casts=()):
        s = rows_in.shape[0] // nb
        return _norm_matmul(rows_in, mix_pre_g, w_main, b_main, w_f, b_f, casts,
                            tm=_pick_tile(s, 1024), tn=_pick_tile(w_main.shape[0], 1024),
                            name=f"inproj_{tag}")

    def mixer(rows_in, projected, prefix, nb, tag):
        n, _ = rows_in.shape
        z, flog = projected
        if prefix is None:
            flog = jnp.pad(flog, ((0, LANES - n), (0, 0)))
            c_col, c_row = _forget_cumsum(flog, batches=nb, heads=heads, name=f"cumsum_{tag}")
            attn = _attention_prefix(z, c_row[0, :, :n].reshape(heads, 1, n), heads=heads,
                                     name=f"attention_{tag}")
        else:
            c_col, _ = _forget_cumsum(flog, batches=nb, heads=heads, name=f"cumsum_{tag}")
            attn = _attention_main(z, c_col, prefix[0], prefix[1], batches=nb, heads=heads,
                                   name=f"attention_{tag}")
        pool = _pool(z, pw, pool_scale, None if prefix is None else prefix[0],
                     batches=nb, ublock=ublock, name=f"pool_{tag}")
        r = _mix(attn, pool, z, rows_in, wa, wp, wo, mix_post_g,
                 gate_block=gate_block, name=f"mix_{tag}")
        return r, z, c_col

    x_rows = x.reshape(batches * seq, d)
    z_x, flog_x, (wa, wp, wo, w_up, w_down) = in_project(x_rows, batches, "main", later_weights)
    r_meta, z_meta, c_meta = mixer(meta_rows, (z_m, flog_m), None, 1, "prefix")
    r, _, _ = mixer(x_rows, (z_x, flog_x), (z_meta, c_meta), batches, "main")
    out = _conv_ffn(r, r_meta, ffn_pre_g, w_up, ffn_conv_w[0], ffn_conv_b, w_down, ffn_post_g,
                    batches=batches, name="conv_ffn")
    return out.reshape(batches, seq, d)
```

```python
import functools
import math

import jax
import jax.numpy as jnp
from jax import lax
from jax.experimental import pallas as pl
from jax.experimental.pallas import tpu as pltpu

HEAD_DIM = 128
POOL_WINDOWS = (2, 4, 8, 16)
HALO_ROWS = 16
EPS = 1e-6
MASK_VALUE = -1e30
F32 = jnp.float32
BF16 = jnp.bfloat16
LANES = 128
LOG2E = math.log2(math.e)
N_SPLIT = 3
SUM_ROWS = 16
VMEM_LIMIT_BYTES = 56 * 1024 * 1024


def _pick_tile(n, preferred):
    t = min(preferred, n)
    while n % t:
        t //= 2
    return t


def _params(semantics, vmem=VMEM_LIMIT_BYTES):
    return pltpu.CompilerParams(dimension_semantics=semantics, vmem_limit_bytes=vmem)


def _rms(x, g):
    ms = jnp.mean(x * x, axis=-1, keepdims=True)
    return x * lax.rsqrt(ms + EPS) * g


def _rms_rows_to(dst_ref, dst_row0, src_ref, g_ref, chunk):
    rows = src_ref.shape[0]
    chunk = min(chunk, rows)

    def body(c, carry):
        r0 = pl.multiple_of(c * chunk, chunk)
        x = src_ref[pl.ds(r0, chunk), :]
        dst_ref[pl.ds(dst_row0 + r0, chunk), :] = _rms(x, g_ref[...]).astype(dst_ref.dtype)
        return carry

    lax.fori_loop(0, rows // chunk, body, 0)


def _norm_matmul_body(x_ref, g_ref, w_ref, b_ref, ws_ref, bs_ref, *rest, n_casts):
    cast_in = rest[:n_casts]
    o_ref, os_ref = rest[n_casts:n_casts + 2]
    cast_out = rest[n_casts + 2:2 * n_casts + 2]
    h_ref = rest[-1]

    @pl.when(pl.program_id(1) == 0)
    def _():
        _rms_rows_to(h_ref, 0, x_ref, g_ref, 128)
        os_ref[...] = lax.dot_general(h_ref[...], ws_ref[...], (((1,), (1,)), ((), ())),
                                      preferred_element_type=F32) + bs_ref[...]

    acc = lax.dot_general(h_ref[...], w_ref[...], (((1,), (1,)), ((), ())),
                          preferred_element_type=F32)
    o_ref[...] = (acc + b_ref[...]).astype(o_ref.dtype)

    for src, dst in zip(cast_in, cast_out):
        dst[...] = src[...].astype(dst.dtype)


def _norm_matmul(x, g, w, b, w_side, b_side, casts=(), *, tm, tn, name):
    m, d = x.shape
    n = w.shape[0]
    ns = w_side.shape[0]
    grid = (m // tm, n // tn)
    n_steps = grid[0] * grid[1]
    cast_specs = []
    for a in casts:
        chunk = next(c for c in range(16, a.shape[0] + 1, 16)
                     if a.shape[0] % c == 0 and a.shape[0] // c <= n_steps)
        last = a.shape[0] // chunk - 1
        cast_specs.append(pl.BlockSpec(
            (chunk, a.shape[1]), lambda i, j, last=last: (jnp.minimum(i * grid[1] + j, last), 0)))
    outs = pl.pallas_call(
        functools.partial(_norm_matmul_body, n_casts=len(casts)),
        grid=grid,
        in_specs=[
            pl.BlockSpec((tm, d), lambda i, j: (i, 0)),
            pl.BlockSpec((1, d), lambda i, j: (0, 0)),
            pl.BlockSpec((tn, d), lambda i, j: (j, 0)),
            pl.BlockSpec((1, tn), lambda i, j: (0, j)),
            pl.BlockSpec((ns, d), lambda i, j: (0, 0)),
            pl.BlockSpec((1, ns), lambda i, j: (0, 0)),
        ] + cast_specs,
        out_specs=[
            pl.BlockSpec((tm, tn), lambda i, j: (i, j)),
            pl.BlockSpec((tm, ns), lambda i, j: (i, 0)),
        ] + cast_specs,
        out_shape=[
            jax.ShapeDtypeStruct((m, n), BF16),
            jax.ShapeDtypeStruct((m, ns), F32),
        ] + [jax.ShapeDtypeStruct(a.shape, BF16) for a in casts],
        scratch_shapes=[pltpu.VMEM((tm, d), BF16)],
        compiler_params=_params(("arbitrary", "arbitrary")),
        name=name,
    )(x, g, w, b, w_side, b_side, *casts)
    return outs[0], outs[1], outs[2:]


def _stage_body(a_ref, b_ref, f_ref, x_ref, g_ref, bias_ref, bias_f_ref, o_ref, of_ref, z_ref, zf_ref,
                h_ref, *, first_shifted, shift):
    j = pl.program_id(0)
    nt = (((1,), (1,)), ((), ()))

    @pl.when(j == 0)
    def _():
        pad = jnp.zeros((of_ref.shape[0] - shift, of_ref.shape[1]), f_ref.dtype)
        wf = jnp.concatenate([f_ref[...], pad], axis=0).astype(of_ref.dtype)
        of_ref[...] = wf
        h_ref[...] = _rms(x_ref[...], g_ref[...]).astype(h_ref.dtype)
        zf_ref[...] = lax.dot_general(h_ref[...], wf, nt, preferred_element_type=F32) + bias_f_ref[...]

    def emit(w_tile):
        o_ref[...] = w_tile
        z = lax.dot_general(h_ref[...], w_tile, nt, preferred_element_type=F32) + bias_ref[...]
        z_ref[...] = z.astype(z_ref.dtype)

    @pl.when(j < first_shifted)
    def _():
        emit(a_ref[...].astype(o_ref.dtype))

    @pl.when(j >= first_shifted)
    def _():
        emit(jnp.concatenate([a_ref[shift:, :], b_ref[...]], axis=0).astype(o_ref.dtype))


def _stage_in_weights(wt, x_prefix, g, bias, bias_f, *, skip_start, skip, name):
    n, d = wt.shape
    npre = x_prefix.shape[0]
    tn = _pick_tile(skip_start, 1024)
    n_out = n - skip
    assert skip % 8 == 0 and n_out % tn == 0 and skip_start % tn == 0
    edge = tn // skip
    return pl.pallas_call(
        functools.partial(_stage_body, first_shifted=skip_start // tn, shift=skip),
        grid=(n_out // tn,),
        in_specs=[
            pl.BlockSpec((tn, d), lambda j: (j, 0)),
            pl.BlockSpec((skip, d), lambda j: ((j + 1) * edge, 0)),
            pl.BlockSpec((skip, d), lambda j: (skip_start // skip, 0)),
            pl.BlockSpec((npre, d), lambda j: (0, 0)),
            pl.BlockSpec((1, d), lambda j: (0, 0)),
            pl.BlockSpec((1, tn), lambda j: (0, j)),
            pl.BlockSpec((1, LANES), lambda j: (0, 0)),
        ],
        out_specs=[
            pl.BlockSpec((tn, d), lambda j: (j, 0)),
            pl.BlockSpec((LANES, d), lambda j: (0, 0)),
            pl.BlockSpec((npre, tn), lambda j: (0, j)),
            pl.BlockSpec((npre, LANES), lambda j: (0, 0)),
        ],
        out_shape=[
            jax.ShapeDtypeStruct((n_out, d), BF16),
            jax.ShapeDtypeStruct((LANES, d), BF16),
            jax.ShapeDtypeStruct((npre, n_out), BF16),
            jax.ShapeDtypeStruct((npre, LANES), F32),
        ],
        scratch_shapes=[pltpu.VMEM((npre, d), BF16)],
        compiler_params=_params(("arbitrary",)),
        name=name,
    )(wt, wt, wt, x_prefix, g, bias, bias_f)


def _cum_body(f_ref, col_ref, row_ref, *, heads):
    rows = f_ref.shape[0]
    x = f_ref[...].T[:heads, :]
    c = jnp.minimum(x, 0.0) - jnp.log1p(jnp.exp(-jnp.abs(x)))
    t = lax.broadcasted_iota(jnp.int32, c.shape, 1)
    shift = 1
    while shift < rows:
        c = c + jnp.where(t >= shift, pltpu.roll(c, shift, axis=1), 0.0)
        shift *= 2
    row_ref[...] = c
    pad = jnp.zeros((col_ref.shape[1] - heads, rows), c.dtype)
    col_ref[...] = jnp.concatenate([c, pad], axis=0).T


def _forget_cumsum(flog, *, batches, heads, name):
    rows = flog.shape[0] // batches
    return pl.pallas_call(
        functools.partial(_cum_body, heads=heads),
        grid=(batches,),
        in_specs=[pl.BlockSpec((rows, flog.shape[1]), lambda b: (b, 0))],
        out_specs=[
            pl.BlockSpec((rows, flog.shape[1]), lambda b: (b, 0)),
            pl.BlockSpec((None, heads, rows), lambda b: (b, 0, 0)),
        ],
        out_shape=[
            jax.ShapeDtypeStruct(flog.shape, F32),
            jax.ShapeDtypeStruct((batches, heads, rows), F32),
        ],
        compiler_params=_params(("arbitrary",)),
        name=name,
    )(flog)


def _attn_prefix_body(q_ref, k_ref, v_ref, c_ref, o_ref, *, scale):
    n = q_ref.shape[0]
    q = (q_ref[...].astype(F32) * scale).astype(BF16)
    s = lax.dot_general(q, k_ref[...], (((1,), (1,)), ((), ())), preferred_element_type=F32)
    s = s - c_ref[...]
    row = lax.broadcasted_iota(jnp.int32, (n, n), 0)
    col = lax.broadcasted_iota(jnp.int32, (n, n), 1)
    s = jnp.where(row >= col, s, MASK_VALUE)
    p = jnp.exp(s - jnp.max(s, axis=-1, keepdims=True))
    l = jnp.sum(p, axis=-1, keepdims=True)
    acc = jnp.dot(p.astype(BF16), v_ref[...], preferred_element_type=F32)
    o_ref[...] = (acc / l).astype(o_ref.dtype)


def _attention_prefix(z, c_row, *, heads, name):
    n = z.shape[0]
    return pl.pallas_call(
        functools.partial(_attn_prefix_body, scale=HEAD_DIM ** -0.5),
        grid=(heads,),
        in_specs=[
            pl.BlockSpec((n, HEAD_DIM), lambda h: (0, h)),
            pl.BlockSpec((n, HEAD_DIM), lambda h: (0, heads + h)),
            pl.BlockSpec((n, HEAD_DIM), lambda h: (0, 2 * heads + h)),
            pl.BlockSpec((None, 1, n), lambda h: (h, 0, 0)),
        ],
        out_specs=pl.BlockSpec((n, HEAD_DIM), lambda h: (0, h)),
        out_shape=jax.ShapeDtypeStruct((n, heads * HEAD_DIM), BF16),
        compiler_params=_params(("arbitrary",)),
        name=name,
    )(z, z, z, c_row)


def _split3(x):
    p1 = x.astype(BF16)
    r1 = x - p1.astype(F32)
    p2 = r1.astype(BF16)
    p3 = (r1 - p2.astype(F32)).astype(BF16)
    return p1, p2, p3


def _bias_block(x, n_heads):
    lane = lax.broadcasted_iota(jnp.int32, x.shape, 1)
    pieces = _split3(jnp.where(lane < n_heads, x, 0.0))
    block = pieces[0].astype(F32)
    for k in range(1, N_SPLIT):
        block = block + pltpu.roll(pieces[k].astype(F32), k * n_heads, axis=1)
    return block.astype(BF16)


def _attn_main_body(q_ref, k_ref, v_ref, c_ref, kp_ref, vp_ref, cp_ref, o_ref,
                    e_ref, vt_ref, ep_ref, vpt_ref, qaug_ref, m_ref, acc_ref, alpha_ref, p_ref,
                    *, tq, hp, n_heads, scale):
    seq = q_ref.shape[0]
    npre = kp_ref.shape[0]
    group = pl.program_id(1)

    e_ref[...] = _bias_block(c_ref[...] * (-LOG2E), n_heads)
    cp = cp_ref[0:npre, :]
    ep_ref[...] = _bias_block((cp - cp[npre - 1:npre, :]) * (-LOG2E), n_heads)

    def sum_rows(n):
        r = lax.broadcasted_iota(jnp.int32, (SUM_ROWS, n), 0)
        return jnp.where(r == 0, 1.0, 0.0).astype(BF16)

    for hh in range(hp):
        lanes = slice(hh * HEAD_DIM, (hh + 1) * HEAD_DIM)
        vt_ref[hh, 0:HEAD_DIM, :] = v_ref[:, lanes].astype(F32).T.astype(BF16)
        vt_ref[hh, HEAD_DIM:, :] = sum_rows(seq)
        vp = jnp.concatenate([vp_ref[:, lanes].astype(F32),
                              jnp.zeros((LANES - npre, HEAD_DIM), F32)], axis=0)
        vpt_ref[hh, 0:HEAD_DIM, :] = vp.T.astype(BF16)
        vpt_ref[hh, HEAD_DIM:, :] = sum_rows(LANES)

    lane = lax.broadcasted_iota(jnp.int32, (tq, HEAD_DIM), 1)
    key = lax.broadcasted_iota(jnp.int32, (tq, tq), 0)
    qry = lax.broadcasted_iota(jnp.int32, (tq, tq), 1)
    causal = key <= qry
    heads = range(hp)

    def k_aug(hh, k0):
        lanes = slice(hh * HEAD_DIM, (hh + 1) * HEAD_DIM)
        return jnp.concatenate([k_ref[pl.ds(k0, tq), lanes], e_ref[pl.ds(k0, tq), :]], axis=1)

    def scores_t(kaug, hh):
        return lax.dot_general(kaug, qaug_ref[hh], (((1,), (1,)), ((), ())),
                               preferred_element_type=F32)

    def softmax_update(hh, st):
        m = m_ref[hh]
        m_new = jnp.maximum(m, jnp.max(st, axis=0, keepdims=True))
        alpha = jnp.exp2(m - m_new)
        m_ref[hh] = m_new
        return alpha, jnp.exp2((st - m_new).astype(BF16))

    def value_update(hh, k_pend):
        vt = vt_ref[hh, :, pl.ds(k_pend, tq)]
        acc_ref[hh] = alpha_ref[hh] * acc_ref[hh] + jnp.dot(vt, p_ref[hh], preferred_element_type=F32)

    def q_tile(qi, carry_unused):
        q0 = pl.multiple_of(qi * tq, tq)
        for hh in heads:
            lanes = slice(hh * HEAD_DIM, (hh + 1) * HEAD_DIM)
            head = group * hp + hh
            q = (q_ref[pl.ds(q0, tq), lanes].astype(F32) * (scale * LOG2E)).astype(BF16)
            picks = (lane == head) | (lane == n_heads + head) | (lane == 2 * n_heads + head)
            qaug_ref[hh] = jnp.concatenate([q, jnp.where(picks, 1.0, 0.0).astype(BF16)], axis=1)

        sts = []
        for hh in heads:
            lanes = slice(hh * HEAD_DIM, (hh + 1) * HEAD_DIM)
            kp_aug = jnp.concatenate([kp_ref[:, lanes], ep_ref[...]], axis=1)
            sts.append(scores_t(jnp.concatenate([kp_aug, k_aug(hh, q0)], axis=0), hh))
        for hh in heads:
            st_pre = sts[hh][0:npre, :]
            st = jnp.where(causal, sts[hh][npre:, :], MASK_VALUE)
            m = jnp.maximum(jnp.max(st, axis=0, keepdims=True), jnp.max(st_pre, axis=0, keepdims=True))
            m_ref[hh] = m
            p_ref[hh] = jnp.exp2((st - m).astype(BF16))
            alpha_ref[hh] = jnp.ones_like(m)
            p_pre = jnp.concatenate([jnp.exp2((st_pre - m).astype(BF16)),
                                     jnp.zeros((LANES - npre, tq), BF16)], axis=0)
            acc_ref[hh] = jnp.dot(vpt_ref[hh], p_pre, preferred_element_type=F32)

        def kv_tile(j, k_pend):
            k0 = pl.multiple_of(j * tq, tq)
            sts = [scores_t(k_aug(hh, k0), hh) for hh in heads]
            for hh in heads:
                value_update(hh, pl.multiple_of(k_pend, tq))
            for hh in heads:
                alpha_ref[hh], p_ref[hh] = softmax_update(hh, sts[hh])
            return k0

        k_pend = lax.fori_loop(0, qi, kv_tile, q0)

        for hh in heads:
            value_update(hh, pl.multiple_of(k_pend, tq))
        for hh in heads:
            lanes = slice(hh * HEAD_DIM, (hh + 1) * HEAD_DIM)
            acc = acc_ref[hh]
            out = acc[0:HEAD_DIM, :] / acc[HEAD_DIM:HEAD_DIM + 1, :]
            o_ref[pl.ds(q0, tq), lanes] = out.T.astype(o_ref.dtype)
        return carry_unused

    lax.fori_loop(0, seq // tq, q_tile, 0)


def _attention_main(z, c_col, z_prefix, c_col_prefix, *, batches, heads, name):
    seq = z.shape[0] // batches
    npre = z_prefix.shape[0]
    tq = _pick_tile(seq, 256)
    hp = _pick_tile(heads, 8)
    assert N_SPLIT * heads <= LANES
    groups = heads // hp
    width = hp * HEAD_DIM
    return pl.pallas_call(
        functools.partial(_attn_main_body, tq=tq, hp=hp, n_heads=heads, scale=HEAD_DIM ** -0.5),
        grid=(batches, groups),
        in_specs=[
            pl.BlockSpec((seq, width), lambda b, g: (b, g)),
            pl.BlockSpec((seq, width), lambda b, g: (b, groups + g)),
            pl.BlockSpec((seq, width), lambda b, g: (b, 2 * groups + g)),
            pl.BlockSpec((seq, LANES), lambda b, g: (b, 0)),
            pl.BlockSpec((npre, width), lambda b, g: (0, groups + g)),
            pl.BlockSpec((npre, width), lambda b, g: (0, 2 * groups + g)),
            pl.BlockSpec(c_col_prefix.shape, lambda b, g: (0, 0)),
        ],
        out_specs=pl.BlockSpec((seq, width), lambda b, g: (b, g)),
        out_shape=jax.ShapeDtypeStruct((batches * seq, heads * HEAD_DIM), BF16),
        scratch_shapes=[
            pltpu.VMEM((seq, LANES), BF16),
            pltpu.VMEM((hp, HEAD_DIM + SUM_ROWS, seq), BF16),
            pltpu.VMEM((npre, LANES), BF16),
            pltpu.VMEM((hp, HEAD_DIM + SUM_ROWS, LANES), BF16),
            pltpu.VMEM((hp, tq, 2 * HEAD_DIM), BF16),
            pltpu.VMEM((hp, 1, tq), F32),
            pltpu.VMEM((hp, HEAD_DIM + SUM_ROWS, tq), F32),
            pltpu.VMEM((hp, 1, tq), F32),
            pltpu.VMEM((hp, tq, tq), BF16),
        ],
        compiler_params=_params(("arbitrary", "arbitrary")),
        name=name,
    )(z, z, z, c_col, z_prefix, z_prefix, c_col_prefix)


def _pool_body(u_ref, *rest, gdim, pos_offset, tiles_per_batch, has_halo):
    if has_halo:
        uprev_ref, upre_ref, pw_ref, ps_ref, o_ref, ext_ref = rest
    else:
        pw_ref, ps_ref, o_ref, ext_ref = rest
    tm = u_ref.shape[0]
    if has_halo:
        first = (pl.program_id(0) % tiles_per_batch) == 0
        halo = jnp.where(first, upre_ref[...], uprev_ref[...]).astype(F32)
    else:
        halo = jnp.zeros((HALO_ROWS, u_ref.shape[1]), F32)
    ext_ref[0:HALO_ROWS, :] = halo
    ext_ref[HALO_ROWS:HALO_ROWS + tm, :] = u_ref[...].astype(F32)

    for g, w in enumerate(POOL_WINDOWS):
        cols = slice(g * gdim, (g + 1) * gdim)
        wsum = ext_ref[:, cols]
        shift = 1
        while shift < w:
            wsum = wsum + pltpu.roll(wsum, shift, axis=0)
            shift *= 2
        wsum = wsum[HALO_ROWS:, :]
        cur = ext_ref[HALO_ROWS:HALO_ROWS + tm, cols]
        if pos_offset + 1 >= w:
            mean = wsum * (1.0 / w)
        else:
            pos = lax.broadcasted_iota(jnp.int32, (tm, 1), 0) + pos_offset
            mean = wsum / jnp.minimum(pos + 1, w).astype(F32)
        d = (mean - cur).astype(BF16)
        y = jnp.dot(d, pw_ref[g], preferred_element_type=F32) * ps_ref[:, cols]
        o_ref[:, cols] = y.astype(o_ref.dtype)


def _pool(z, pool_w, pool_scale, prefix_z, *, batches, ublock, name):
    groups, gdim, _ = pool_w.shape
    dp = groups * gdim
    rows = z.shape[0]
    seq = rows // batches
    has_halo = prefix_z is not None
    tm = _pick_tile(seq, 512)
    tiles_per_batch = seq // tm
    in_specs = [pl.BlockSpec((tm, dp), lambda i: (i, ublock))]
    args = [z]
    if has_halo:
        halo_blocks = tm // HALO_ROWS
        in_specs += [
            pl.BlockSpec((HALO_ROWS, dp), lambda i: (jnp.maximum(i * halo_blocks - 1, 0), ublock)),
            pl.BlockSpec((HALO_ROWS, dp), lambda i: (prefix_z.shape[0] // HALO_ROWS - 1, ublock)),
        ]
        args += [z, prefix_z]
        pos_offset = prefix_z.shape[0]
    else:
        pos_offset = 0
    in_specs += [
        pl.BlockSpec((groups, gdim, gdim), lambda i: (0, 0, 0)),
        pl.BlockSpec((1, dp), lambda i: (0, 0)),
    ]
    args += [pool_w, pool_scale]
    return pl.pallas_call(
        functools.partial(_pool_body, gdim=gdim, pos_offset=pos_offset,
                          tiles_per_batch=tiles_per_batch, has_halo=has_halo),
        grid=(rows // tm,),
        in_specs=in_specs,
        out_specs=pl.BlockSpec((tm, dp), lambda i: (i, 0)),
        out_shape=jax.ShapeDtypeStruct((rows, dp), BF16),
        scratch_shapes=[pltpu.VMEM((tm + HALO_ROWS, dp), F32)],
        compiler_params=_params(("arbitrary",)),
        name=name,
    )(*args)


def _mix_body(a_ref, p_ref, ga_ref, gp_ref, x_ref, wa_ref, wp_ref, wo_ref, g_ref, o_ref):
    ya = jnp.dot(a_ref[...], wa_ref[...], preferred_element_type=F32)
    yp = jnp.dot(p_ref[...], wp_ref[...], preferred_element_type=F32)
    m = (jax.nn.sigmoid(ga_ref[...].astype(F32)) * ya
         + jax.nn.sigmoid(gp_ref[...].astype(F32)) * yp)
    mo = jnp.dot(m.astype(BF16), wo_ref[...], preferred_element_type=F32)
    o_ref[...] = x_ref[...] + _rms(mo, g_ref[...])


def _mix(attn, pool, z, x, w_attn_o, w_pool_o, w_out, g_post, *, gate_block, name):
    rows, d = x.shape
    da = attn.shape[1]
    dp = pool.shape[1]
    tm = _pick_tile(rows, 256)
    resident = dict(pipeline_mode=pl.Buffered(1))
    return pl.pallas_call(
        _mix_body,
        grid=(rows // tm,),
        in_specs=[
            pl.BlockSpec((tm, da), lambda i: (i, 0)),
            pl.BlockSpec((tm, dp), lambda i: (i, 0)),
            pl.BlockSpec((tm, d), lambda i: (i, gate_block)),
            pl.BlockSpec((tm, d), lambda i: (i, gate_block + 1)),
            pl.BlockSpec((tm, d), lambda i: (i, 0)),
            pl.BlockSpec((da, d), lambda i: (0, 0), **resident),
            pl.BlockSpec((dp, d), lambda i: (0, 0), **resident),
            pl.BlockSpec((d, d), lambda i: (0, 0), **resident),
            pl.BlockSpec((1, d), lambda i: (0, 0)),
        ],
        out_specs=pl.BlockSpec((tm, d), lambda i: (i, 0)),
        out_shape=jax.ShapeDtypeStruct((rows, d), F32),
        compiler_params=_params(("arbitrary",)),
        name=name,
    )(attn, pool, z, z, x, w_attn_o, w_pool_o, w_out, g_post)


def _gelu_tanh(x):
    c = math.sqrt(2.0 / math.pi)
    v = (x * x).astype(F32) * (0.044715 * c) + c
    return (0.5 * x) * (1.0 + jnp.tanh(x * v.astype(x.dtype)))


def _ffn_body(ru_ref, rprev_ref, rpre_ref, rd_ref, gpre_ref, wg_ref, wv_ref, cwg_ref, cwv_ref,
              cbg_ref, cbv_ref, wd_ref, gpost_ref, o_ref, h_ref, gate_ref, act0_ref, act1_ref, *, nf,
              n_steps, tiles_per_batch):
    t = pl.program_id(0)
    tu = jnp.minimum(t, n_steps - 1)
    fu = tu % nf
    iu = tu // nf
    fd = jnp.maximum(t - 1, 0) % nf
    tm = o_ref.shape[0]

    @pl.when(t == 0)
    def _():
        act1_ref[...] = jnp.zeros(act1_ref.shape, act1_ref.dtype)

    @pl.when(jnp.logical_and(fu == 0, t < n_steps))
    def _():
        first = (iu % tiles_per_batch) == 0
        halo = jnp.where(first, rpre_ref[...], rprev_ref[...])
        h_ref[0:HALO_ROWS, :] = _rms(halo, gpre_ref[...]).astype(BF16)
        _rms_rows_to(h_ref, HALO_ROWS, ru_ref, gpre_ref, 128)

    def up_conv(w_ref, cw_ref, cb_ref):
        a = jnp.dot(h_ref[...], w_ref[...], preferred_element_type=F32)
        a0, a1, a2 = [(a if k == 0 else pltpu.roll(a, k, axis=0))[HALO_ROWS:, :].astype(BF16)
                      for k in range(3)]
        cw = cw_ref[...].astype(BF16)
        out = cb_ref[...].astype(BF16) + a2 * cw[0:1, :]
        out = out + a1 * cw[1:2, :]
        out = out + a0 * cw[2:3, :]
        return out

    def step(act_u_ref, act_d_ref, first):
        gate_ref[...] = up_conv(wg_ref, cwg_ref, cbg_ref)
        val = up_conv(wv_ref, cwv_ref, cbv_ref)
        act_u_ref[...] = _gelu_tanh(gate_ref[...]) * val
        down = jnp.dot(act_d_ref[...], wd_ref[...], preferred_element_type=F32)
        if first:
            o_ref[...] = down
        else:
            o_ref[...] += down

    for parity, (act_u_ref, act_d_ref) in enumerate(((act0_ref, act1_ref), (act1_ref, act0_ref))):
        for first in (True, False):
            pl.when(jnp.logical_and(t % 2 == parity, (fd == 0) == first))(
                functools.partial(step, act_u_ref, act_d_ref, first))

    @pl.when(jnp.logical_and(fd == nf - 1, t > 0))
    def _():
        chunk = min(128, tm)

        def body(c, carry):
            r0 = pl.multiple_of(c * chunk, chunk)
            rows = pl.ds(r0, chunk)
            o_ref[rows, :] = rd_ref[rows, :] + _rms(o_ref[rows, :], gpost_ref[...])
            return carry

        lax.fori_loop(0, tm // chunk, body, 0)


def _conv_ffn(r, r_prefix, g_pre, w_up, conv_w, conv_b, w_down, g_post, *, batches, name):
    rows, d = r.shape
    dff = w_down.shape[0]
    seq = rows // batches
    tm = _pick_tile(seq, 512)
    tf = _pick_tile(dff, 512)
    nf = dff // tf
    assert nf >= 2
    n_steps = (rows // tm) * nf
    tiles_per_batch = seq // tm
    halo_blocks = tm // HALO_ROWS
    kw = conv_w.shape[0]

    def up_item(t):
        tu = jnp.minimum(t, n_steps - 1)
        return tu // nf, tu % nf

    def down_item(t):
        td = jnp.maximum(t - 1, 0)
        return td // nf, td % nf

    return pl.pallas_call(
        functools.partial(_ffn_body, nf=nf, n_steps=n_steps, tiles_per_batch=tiles_per_batch),
        grid=(n_steps + 1,),
        in_specs=[
            pl.BlockSpec((tm, d), lambda t: (up_item(t)[0], 0)),
            pl.BlockSpec((HALO_ROWS, d), lambda t: (jnp.maximum(up_item(t)[0] * halo_blocks - 1, 0), 0)),
            pl.BlockSpec((HALO_ROWS, d), lambda t: (r_prefix.shape[0] // HALO_ROWS - 1, 0)),
            pl.BlockSpec((tm, d), lambda t: (down_item(t)[0], 0)),
            pl.BlockSpec((1, d), lambda t: (0, 0)),
            pl.BlockSpec((d, tf), lambda t: (0, up_item(t)[1])),
            pl.BlockSpec((d, tf), lambda t: (0, nf + up_item(t)[1])),
            pl.BlockSpec((kw, tf), lambda t: (0, up_item(t)[1])),
            pl.BlockSpec((kw, tf), lambda t: (0, nf + up_item(t)[1])),
            pl.BlockSpec((1, tf), lambda t: (0, up_item(t)[1])),
            pl.BlockSpec((1, tf), lambda t: (0, nf + up_item(t)[1])),
            pl.BlockSpec((tf, d), lambda t: (down_item(t)[1], 0)),
            pl.BlockSpec((1, d), lambda t: (0, 0)),
        ],
        out_specs=pl.BlockSpec((tm, d), lambda t: (down_item(t)[0], 0)),
        out_shape=jax.ShapeDtypeStruct((rows, d), F32),
        scratch_shapes=[
            pltpu.VMEM((HALO_ROWS + tm, d), BF16),
            pltpu.VMEM((tm, tf), BF16),
            pltpu.VMEM((tm, tf), BF16),
            pltpu.VMEM((tm, tf), BF16),
        ],
        compiler_params=_params(("arbitrary",)),
        name=name,
    )(r, r, r_prefix, r, g_pre, w_up, w_up, conv_w, conv_w, conv_b, conv_b, w_down, g_post)


def kernel(x, meta_tokens, mix_pre_g, w_in, b_in, w_attn_o, pool_w, pool_scale, w_pool_o, w_out,
           mix_post_g, ffn_pre_g, w_ffn_up, ffn_conv_w, ffn_conv_b, w_ffn_down, ffn_post_g):
    batches, seq, d = x.shape
    n_meta = meta_tokens.shape[0]
    assert w_in.shape[0] == 1, "single-layer block"
    assert n_meta == HALO_ROWS and n_meta >= max(POOL_WINDOWS) and ffn_conv_w.shape[1] <= HALO_ROWS
    d_attn = w_attn_o.shape[1]
    d_pool = w_pool_o.shape[1]
    heads = d_attn // HEAD_DIM
    assert d_attn == d_pool and 2 * d_attn == d
    assert w_in.shape[2] == 3 * d_attn + heads + d_pool + 2 * d

    f0 = 3 * d_attn
    w_t = w_in[0].T
    b = b_in[0]
    b_main = jnp.concatenate([b[:f0], b[f0 + heads:]])[None, :]
    b_f = jnp.pad(b[f0:f0 + heads], (0, LANES - heads))[None, :]
    meta_rows = meta_tokens.astype(x.dtype)
    w_main, w_f, z_m, flog_m = _stage_in_weights(w_t, meta_rows, mix_pre_g, b_main, b_f,
                                                  skip_start=f0, skip=heads, name="stage_w_in")
    pw = pool_w[0].astype(BF16)
    later_weights = (w_attn_o[0], w_pool_o[0], w_out[0], w_ffn_up[0], w_ffn_down[0])
    ublock = f0 // d_pool
    gate_block = (f0 + d_pool) // d

    def in_project(rows_in, nb, tag, casts=()):
        s = rows_in.shape[0] // nb
        return _norm_matmul(rows_in, mix_pre_g, w_main, b_main, w_f, b_f, casts,
                            tm=_pick_tile(s, 1024), tn=_pick_tile(w_main.shape[0], 1024),
                            name=f"inproj_{tag}")

    def mixer(rows_in, projected, prefix, nb, tag):
        n, _ = rows_in.shape
        z, flog = projected
        if prefix is None:
            flog = jnp.pad(flog, ((0, LANES - n), (0, 0)))
            c_col, c_row = _forget_cumsum(flog, batches=nb, heads=heads, name=f"cumsum_{tag}")
            attn = _attention_prefix(z, c_row[0, :, :n].reshape(heads, 1, n), heads=heads,
                                     name=f"attention_{tag}")
        else:
            c_col, _ = _forget_cumsum(flog, batches=nb, heads=heads, name=f"cumsum_{tag}")
            attn = _attention_main(z, c_col, prefix[0], prefix[1], batches=nb, heads=heads,
                                   name=f"attention_{tag}")
        pool = _pool(z, pw, pool_scale, None if prefix is None else prefix[0],
                     batches=nb, ublock=ublock, name=f"pool_{tag}")
        r = _mix(attn, pool, z, rows_in, wa, wp, wo, mix_post_g,
                 gate_block=gate_block, name=f"mix_{tag}")
        return r, z, c_col

    x_rows = x.reshape(batches * seq, d)
    z_x, flog_x, (wa, wp, wo, w_up, w_down) = in_project(x_rows, batches, "main", later_weights)
    r_meta, z_meta, c_meta = mixer(meta_rows, (z_m, flog_m), None, 1, "prefix")
    r, _, _ = mixer(x_rows, (z_x, flog_x), (z_meta, c_meta), batches, "main")
    out = _conv_ffn(r, r_meta, ffn_pre_g, w_up, ffn_conv_w[0], ffn_conv_b, w_down, ffn_post_g,
                    batches=batches, name="conv_ffn")
    return out.reshape(batches, seq, d)
```

```python
import functools
import math

import jax
import jax.numpy as jnp
from jax import lax
from jax.experimental import pallas as pl
from jax.experimental.pallas import tpu as pltpu

HEAD_DIM = 128
POOL_WINDOWS = (2, 4, 8, 16)
HALO_ROWS = 16
EPS = 1e-6
MASK_VALUE = -1e30
F32 = jnp.float32
BF16 = jnp.bfloat16
LANES = 128
LOG2E = math.log2(math.e)
N_SPLIT = 3
SUM_ROWS = 16
VMEM_LIMIT_BYTES = 56 * 1024 * 1024


def _pick_tile(n, preferred):
    t = min(preferred, n)
    while n % t:
        t //= 2
    return t


def _params(semantics, vmem=VMEM_LIMIT_BYTES):
    return pltpu.CompilerParams(dimension_semantics=semantics, vmem_limit_bytes=vmem)


def _rms(x, g):
    ms = jnp.mean(x * x, axis=-1, keepdims=True)
    return x * lax.rsqrt(ms + EPS) * g


def _rms_rows_to(dst_ref, dst_row0, src_ref, g_ref, chunk):
    rows = src_ref.shape[0]
    chunk = min(chunk, rows)

    def body(c, carry):
        r0 = pl.multiple_of(c * chunk, chunk)
        x = src_ref[pl.ds(r0, chunk), :]
        dst_ref[pl.ds(dst_row0 + r0, chunk), :] = _rms(x, g_ref[...]).astype(dst_ref.dtype)
        return carry

    lax.fori_loop(0, rows // chunk, body, 0)


def _norm_matmul_body(x_ref, g_ref, w_ref, b_ref, ws_ref, bs_ref, *rest, n_casts):
    cast_in = rest[:n_casts]
    o_ref, os_ref = rest[n_casts:n_casts + 2]
    cast_out = rest[n_casts + 2:2 * n_casts + 2]
    h_ref = rest[-1]

    @pl.when(pl.program_id(1) == 0)
    def _():
        _rms_rows_to(h_ref, 0, x_ref, g_ref, 128)
        os_ref[...] = lax.dot_general(h_ref[...], ws_ref[...], (((1,), (1,)), ((), ())),
                                      preferred_element_type=F32) + bs_ref[...]

    acc = lax.dot_general(h_ref[...], w_ref[...], (((1,), (1,)), ((), ())),
                          preferred_element_type=F32)
    o_ref[...] = (acc + b_ref[...]).astype(o_ref.dtype)

    for src, dst in zip(cast_in, cast_out):
        if len(dst.shape) == 2:
            dst[...] = src[...].astype(dst.dtype)
        else:
            width = dst.shape[2]
            for k in range(dst.shape[0]):
                dst[k] = src[:, k * width:(k + 1) * width].astype(dst.dtype)


def _norm_matmul(x, g, w, b, w_side, b_side, casts=(), *, tm, tn, name):
    m, d = x.shape
    n = w.shape[0]
    ns = w_side.shape[0]
    grid = (m // tm, n // tn)
    n_steps = grid[0] * grid[1]
    cast_in_specs, cast_out_specs, cast_shapes = [], [], []
    for a, col_tile in casts:
        rows_a, cols_a = a.shape
        chunk = next(c for c in range(16, rows_a + 1, 16) if rows_a % c == 0 and rows_a // c <= n_steps)
        last = rows_a // chunk - 1
        cast_in_specs.append(pl.BlockSpec(
            (chunk, cols_a), lambda i, j, last=last: (jnp.minimum(i * grid[1] + j, last), 0)))
        if col_tile is None:
            cast_out_specs.append(cast_in_specs[-1])
            cast_shapes.append(jax.ShapeDtypeStruct(a.shape, BF16))
        else:
            tiles = cols_a // col_tile
            cast_out_specs.append(pl.BlockSpec(
                (tiles, chunk, col_tile),
                lambda i, j, last=last: (0, jnp.minimum(i * grid[1] + j, last), 0)))
            cast_shapes.append(jax.ShapeDtypeStruct((tiles, rows_a, col_tile), BF16))
    outs = pl.pallas_call(
        functools.partial(_norm_matmul_body, n_casts=len(casts)),
        grid=grid,
        in_specs=[
            pl.BlockSpec((tm, d), lambda i, j: (i, 0)),
            pl.BlockSpec((1, d), lambda i, j: (0, 0)),
            pl.BlockSpec((tn, d), lambda i, j: (j, 0)),
            pl.BlockSpec((1, tn), lambda i, j: (0, j)),
            pl.BlockSpec((ns, d), lambda i, j: (0, 0)),
            pl.BlockSpec((1, ns), lambda i, j: (0, 0)),
        ] + cast_in_specs,
        out_specs=[
            pl.BlockSpec((tm, tn), lambda i, j: (i, j)),
            pl.BlockSpec((tm, ns), lambda i, j: (i, 0)),
        ] + cast_out_specs,
        out_shape=[
            jax.ShapeDtypeStruct((m, n), BF16),
            jax.ShapeDtypeStruct((m, ns), F32),
        ] + cast_shapes,
        scratch_shapes=[pltpu.VMEM((tm, d), BF16)],
        compiler_params=_params(("arbitrary", "arbitrary")),
        name=name,
    )(x, g, w, b, w_side, b_side, *[a for a, _ in casts])
    return outs[0], outs[1], outs[2:]


def _stage_body(a_ref, b_ref, f_ref, x_ref, g_ref, bias_ref, bias_f_ref, o_ref, of_ref, z_ref, zf_ref,
                h_ref, *, first_shifted, shift):
    j = pl.program_id(0)
    nt = (((1,), (1,)), ((), ()))

    @pl.when(j == 0)
    def _():
        pad = jnp.zeros((of_ref.shape[0] - shift, of_ref.shape[1]), f_ref.dtype)
        wf = jnp.concatenate([f_ref[...], pad], axis=0).astype(of_ref.dtype)
        of_ref[...] = wf
        h_ref[...] = _rms(x_ref[...], g_ref[...]).astype(h_ref.dtype)
        zf_ref[...] = lax.dot_general(h_ref[...], wf, nt, preferred_element_type=F32) + bias_f_ref[...]

    def emit(w_tile):
        o_ref[...] = w_tile
        z = lax.dot_general(h_ref[...], w_tile, nt, preferred_element_type=F32) + bias_ref[...]
        z_ref[...] = z.astype(z_ref.dtype)

    @pl.when(j < first_shifted)
    def _():
        emit(a_ref[...].astype(o_ref.dtype))

    @pl.when(j >= first_shifted)
    def _():
        emit(jnp.concatenate([a_ref[shift:, :], b_ref[...]], axis=0).astype(o_ref.dtype))


def _stage_in_weights(wt, x_prefix, g, bias, bias_f, *, skip_start, skip, name):
    n, d = wt.shape
    npre = x_prefix.shape[0]
    tn = _pick_tile(skip_start, 1024)
    n_out = n - skip
    assert skip % 8 == 0 and n_out % tn == 0 and skip_start % tn == 0
    edge = tn // skip
    return pl.pallas_call(
        functools.partial(_stage_body, first_shifted=skip_start // tn, shift=skip),
        grid=(n_out // tn,),
        in_specs=[
            pl.BlockSpec((tn, d), lambda j: (j, 0)),
            pl.BlockSpec((skip, d), lambda j: ((j + 1) * edge, 0)),
            pl.BlockSpec((skip, d), lambda j: (skip_start // skip, 0)),
            pl.BlockSpec((npre, d), lambda j: (0, 0)),
            pl.BlockSpec((1, d), lambda j: (0, 0)),
            pl.BlockSpec((1, tn), lambda j: (0, j)),
            pl.BlockSpec((1, LANES), lambda j: (0, 0)),
        ],
        out_specs=[
            pl.BlockSpec((tn, d), lambda j: (j, 0)),
            pl.BlockSpec((LANES, d), lambda j: (0, 0)),
            pl.BlockSpec((npre, tn), lambda j: (0, j)),
            pl.BlockSpec((npre, LANES), lambda j: (0, 0)),
        ],
        out_shape=[
            jax.ShapeDtypeStruct((n_out, d), BF16),
            jax.ShapeDtypeStruct((LANES, d), BF16),
            jax.ShapeDtypeStruct((npre, n_out), BF16),
            jax.ShapeDtypeStruct((npre, LANES), F32),
        ],
        scratch_shapes=[pltpu.VMEM((npre, d), BF16)],
        compiler_params=_params(("arbitrary",)),
        name=name,
    )(wt, wt, wt, x_prefix, g, bias, bias_f)


def _cum_body(f_ref, col_ref, row_ref, *, heads):
    x = f_ref[...]
    rows = x.shape[0]
    c = jnp.minimum(x, 0.0) - jnp.log1p(jnp.exp(-jnp.abs(x)))
    t = lax.broadcasted_iota(jnp.int32, c.shape, 0)
    shift = 1
    while shift < rows:
        c = c + jnp.where(t >= shift, pltpu.roll(c, shift, axis=0), 0.0)
        shift *= 2
    col_ref[...] = c
    row_ref[...] = c.T[:heads, :]


def _forget_cumsum(flog, *, batches, heads, name):
    rows = flog.shape[0] // batches
    return pl.pallas_call(
        functools.partial(_cum_body, heads=heads),
        grid=(batches,),
        in_specs=[pl.BlockSpec((rows, flog.shape[1]), lambda b: (b, 0))],
        out_specs=[
            pl.BlockSpec((rows, flog.shape[1]), lambda b: (b, 0)),
            pl.BlockSpec((None, heads, rows), lambda b: (b, 0, 0)),
        ],
        out_shape=[
            jax.ShapeDtypeStruct(flog.shape, F32),
            jax.ShapeDtypeStruct((batches, heads, rows), F32),
        ],
        compiler_params=_params(("arbitrary",)),
        name=name,
    )(flog)


def _attn_prefix_body(q_ref, k_ref, v_ref, c_ref, o_ref, *, scale):
    n = q_ref.shape[0]
    q = (q_ref[...].astype(F32) * scale).astype(BF16)
    s = lax.dot_general(q, k_ref[...], (((1,), (1,)), ((), ())), preferred_element_type=F32)
    s = s - c_ref[...]
    row = lax.broadcasted_iota(jnp.int32, (n, n), 0)
    col = lax.broadcasted_iota(jnp.int32, (n, n), 1)
    s = jnp.where(row >= col, s, MASK_VALUE)
    p = jnp.exp(s - jnp.max(s, axis=-1, keepdims=True))
    l = jnp.sum(p, axis=-1, keepdims=True)
    acc = jnp.dot(p.astype(BF16), v_ref[...], preferred_element_type=F32)
    o_ref[...] = (acc / l).astype(o_ref.dtype)


def _attention_prefix(z, c_row, *, heads, name):
    n = z.shape[0]
    return pl.pallas_call(
        functools.partial(_attn_prefix_body, scale=HEAD_DIM ** -0.5),
        grid=(heads,),
        in_specs=[
            pl.BlockSpec((n, HEAD_DIM), lambda h: (0, h)),
            pl.BlockSpec((n, HEAD_DIM), lambda h: (0, heads + h)),
            pl.BlockSpec((n, HEAD_DIM), lambda h: (0, 2 * heads + h)),
            pl.BlockSpec((None, 1, n), lambda h: (h, 0, 0)),
        ],
        out_specs=pl.BlockSpec((n, HEAD_DIM), lambda h: (0, h)),
        out_shape=jax.ShapeDtypeStruct((n, heads * HEAD_DIM), BF16),
        compiler_params=_params(("arbitrary",)),
        name=name,
    )(z, z, z, c_row)


def _split3(x):
    p1 = x.astype(BF16)
    r1 = x - p1.astype(F32)
    p2 = r1.astype(BF16)
    p3 = (r1 - p2.astype(F32)).astype(BF16)
    return p1, p2, p3


def _bias_block(x, n_heads):
    lane = lax.broadcasted_iota(jnp.int32, x.shape, 1)
    pieces = _split3(jnp.where(lane < n_heads, x, 0.0))
    block = pieces[0].astype(F32)
    for k in range(1, N_SPLIT):
        block = block + pltpu.roll(pieces[k].astype(F32), k * n_heads, axis=1)
    return block.astype(BF16)


def _attn_main_body(q_ref, k_ref, v_ref, c_ref, kp_ref, vp_ref, cp_ref, o_ref,
                    e_ref, vt_ref, ep_ref, vpt_ref, qaug_ref, m_ref, acc_ref, alpha_ref, p_ref,
                    *, tq, hp, n_heads, scale):
    seq = q_ref.shape[0]
    npre = kp_ref.shape[0]
    group = pl.program_id(1)

    e_ref[...] = _bias_block(c_ref[...] * (-LOG2E), n_heads)
    cp = cp_ref[0:npre, :]
    ep_ref[...] = _bias_block((cp - cp[npre - 1:npre, :]) * (-LOG2E), n_heads)

    def sum_rows(n):
        r = lax.broadcasted_iota(jnp.int32, (SUM_ROWS, n), 0)
        return jnp.where(r == 0, 1.0, 0.0).astype(BF16)

    for hh in range(hp):
        lanes = slice(hh * HEAD_DIM, (hh + 1) * HEAD_DIM)
        vt_ref[hh, 0:HEAD_DIM, :] = v_ref[:, lanes].astype(F32).T.astype(BF16)
        vt_ref[hh, HEAD_DIM:, :] = sum_rows(seq)
        vp = jnp.concatenate([vp_ref[:, lanes].astype(F32),
                              jnp.zeros((LANES - npre, HEAD_DIM), F32)], axis=0)
        vpt_ref[hh, 0:HEAD_DIM, :] = vp.T.astype(BF16)
        vpt_ref[hh, HEAD_DIM:, :] = sum_rows(LANES)

    lane = lax.broadcasted_iota(jnp.int32, (tq, HEAD_DIM), 1)
    key = lax.broadcasted_iota(jnp.int32, (tq, tq), 0)
    qry = lax.broadcasted_iota(jnp.int32, (tq, tq), 1)
    causal = key <= qry
    heads = range(hp)

    def k_aug(hh, k0):
        lanes = slice(hh * HEAD_DIM, (hh + 1) * HEAD_DIM)
        return jnp.concatenate([k_ref[pl.ds(k0, tq), lanes], e_ref[pl.ds(k0, tq), :]], axis=1)

    def scores_t(kaug, hh):
        return lax.dot_general(kaug, qaug_ref[hh], (((1,), (1,)), ((), ())),
                               preferred_element_type=F32)

    def softmax_update(hh, st):
        m = m_ref[hh]
        m_new = jnp.maximum(m, jnp.max(st, axis=0, keepdims=True))
        alpha = jnp.exp2(m - m_new)
        m_ref[hh] = m_new
        return alpha, jnp.exp2((st - m_new).astype(BF16))

    def value_update(hh, k_pend):
        vt = vt_ref[hh, :, pl.ds(k_pend, tq)]
        acc_ref[hh] = alpha_ref[hh] * acc_ref[hh] + jnp.dot(vt, p_ref[hh], preferred_element_type=F32)

    def q_tile(qi, carry_unused):
        q0 = pl.multiple_of(qi * tq, tq)
        for hh in heads:
            lanes = slice(hh * HEAD_DIM, (hh + 1) * HEAD_DIM)
            head = group * hp + hh
            q = (q_ref[pl.ds(q0, tq), lanes].astype(F32) * (scale * LOG2E)).astype(BF16)
            picks = (lane == head) | (lane == n_heads + head) | (lane == 2 * n_heads + head)
            qaug_ref[hh] = jnp.concatenate([q, jnp.where(picks, 1.0, 0.0).astype(BF16)], axis=1)

        sts = []
        for hh in heads:
            lanes = slice(hh * HEAD_DIM, (hh + 1) * HEAD_DIM)
            kp_aug = jnp.concatenate([kp_ref[:, lanes], ep_ref[...]], axis=1)
            sts.append(scores_t(jnp.concatenate([kp_aug, k_aug(hh, q0)], axis=0), hh))
        for hh in heads:
            st_pre = sts[hh][0:npre, :]
            st = jnp.where(causal, sts[hh][npre:, :], MASK_VALUE)
            m = jnp.maximum(jnp.max(st, axis=0, keepdims=True), jnp.max(st_pre, axis=0, keepdims=True))
            m_ref[hh] = m
            p_ref[hh] = jnp.exp2((st - m).astype(BF16))
            alpha_ref[hh] = jnp.ones_like(m)
            p_pre = jnp.concatenate([jnp.exp2((st_pre - m).astype(BF16)),
                                     jnp.zeros((LANES - npre, tq), BF16)], axis=0)
            acc_ref[hh] = jnp.dot(vpt_ref[hh], p_pre, preferred_element_type=F32)

        def kv_tile(j, k_pend):
            k0 = pl.multiple_of(j * tq, tq)
            sts = [scores_t(k_aug(hh, k0), hh) for hh in heads]
            for hh in heads:
                value_update(hh, pl.multiple_of(k_pend, tq))
            for hh in heads:
                alpha_ref[hh], p_ref[hh] = softmax_update(hh, sts[hh])
            return k0

        k_pend = lax.fori_loop(0, qi, kv_tile, q0)

        for hh in heads:
            value_update(hh, pl.multiple_of(k_pend, tq))
        for hh in heads:
            lanes = slice(hh * HEAD_DIM, (hh + 1) * HEAD_DIM)
            acc = acc_ref[hh]
            out = acc[0:HEAD_DIM, :] / acc[HEAD_DIM:HEAD_DIM + 1, :]
            o_ref[pl.ds(q0, tq), lanes] = out.T.astype(o_ref.dtype)
        return carry_unused

    lax.fori_loop(0, seq // tq, q_tile, 0)


def _attention_main(z, c_col, z_prefix, c_col_prefix, *, batches, heads, name):
    seq = z.shape[0] // batches
    npre = z_prefix.shape[0]
    tq = _pick_tile(seq, 256)
    hp = _pick_tile(heads, 8)
    assert N_SPLIT * heads <= LANES
    groups = heads // hp
    width = hp * HEAD_DIM
    return pl.pallas_call(
        functools.partial(_attn_main_body, tq=tq, hp=hp, n_heads=heads, scale=HEAD_DIM ** -0.5),
        grid=(batches, groups),
        in_specs=[
            pl.BlockSpec((seq, width), lambda b, g: (b, g)),
            pl.BlockSpec((seq, width), lambda b, g: (b, groups + g)),
            pl.BlockSpec((seq, width), lambda b, g: (b, 2 * groups + g)),
            pl.BlockSpec((seq, LANES), lambda b, g: (b, 0)),
            pl.BlockSpec((npre, width), lambda b, g: (0, groups + g)),
            pl.BlockSpec((npre, width), lambda b, g: (0, 2 * groups + g)),
            pl.BlockSpec(c_col_prefix.shape, lambda b, g: (0, 0)),
        ],
        out_specs=pl.BlockSpec((seq, width), lambda b, g: (b, g)),
        out_shape=jax.ShapeDtypeStruct((batches * seq, heads * HEAD_DIM), BF16),
        scratch_shapes=[
            pltpu.VMEM((seq, LANES), BF16),
            pltpu.VMEM((hp, HEAD_DIM + SUM_ROWS, seq), BF16),
            pltpu.VMEM((npre, LANES), BF16),
            pltpu.VMEM((hp, HEAD_DIM + SUM_ROWS, LANES), BF16),
            pltpu.VMEM((hp, tq, 2 * HEAD_DIM), BF16),
            pltpu.VMEM((hp, 1, tq), F32),
            pltpu.VMEM((hp, HEAD_DIM + SUM_ROWS, tq), F32),
            pltpu.VMEM((hp, 1, tq), F32),
            pltpu.VMEM((hp, tq, tq), BF16),
        ],
        compiler_params=_params(("arbitrary", "arbitrary")),
        name=name,
    )(z, z, z, c_col, z_prefix, z_prefix, c_col_prefix)


def _pool_body(u_ref, *rest, gdim, pos_offset, tiles_per_batch, has_halo):
    if has_halo:
        uprev_ref, upre_ref, pw_ref, ps_ref, o_ref, ext_ref = rest
    else:
        pw_ref, ps_ref, o_ref, ext_ref = rest
    tm = u_ref.shape[0]
    if has_halo:
        first = (pl.program_id(0) % tiles_per_batch) == 0
        halo = jnp.where(first, upre_ref[...], uprev_ref[...]).astype(F32)
    else:
        halo = jnp.zeros((HALO_ROWS, u_ref.shape[1]), F32)
    ext_ref[0:HALO_ROWS, :] = halo
    ext_ref[HALO_ROWS:HALO_ROWS + tm, :] = u_ref[...].astype(F32)

    for g, w in enumerate(POOL_WINDOWS):
        cols = slice(g * gdim, (g + 1) * gdim)
        wsum = ext_ref[:, cols]
        shift = 1
        while shift < w:
            wsum = wsum + pltpu.roll(wsum, shift, axis=0)
            shift *= 2
        wsum = wsum[HALO_ROWS:, :]
        cur = ext_ref[HALO_ROWS:HALO_ROWS + tm, cols]
        if pos_offset + 1 >= w:
            mean = wsum * (1.0 / w)
        else:
            pos = lax.broadcasted_iota(jnp.int32, (tm, 1), 0) + pos_offset
            mean = wsum / jnp.minimum(pos + 1, w).astype(F32)
        d = (mean - cur).astype(BF16)
        y = jnp.dot(d, pw_ref[g], preferred_element_type=F32) * ps_ref[:, cols]
        o_ref[:, cols] = y.astype(o_ref.dtype)


def _pool(z, pool_w, pool_scale, prefix_z, *, batches, ublock, name):
    groups, gdim, _ = pool_w.shape
    dp = groups * gdim
    rows = z.shape[0]
    seq = rows // batches
    has_halo = prefix_z is not None
    tm = _pick_tile(seq, 512)
    tiles_per_batch = seq // tm
    in_specs = [pl.BlockSpec((tm, dp), lambda i: (i, ublock))]
    args = [z]
    if has_halo:
        halo_blocks = tm // HALO_ROWS
        in_specs += [
            pl.BlockSpec((HALO_ROWS, dp), lambda i: (jnp.maximum(i * halo_blocks - 1, 0), ublock)),
            pl.BlockSpec((HALO_ROWS, dp), lambda i: (prefix_z.shape[0] // HALO_ROWS - 1, ublock)),
        ]
        args += [z, prefix_z]
        pos_offset = prefix_z.shape[0]
    else:
        pos_offset = 0
    in_specs += [
        pl.BlockSpec((groups, gdim, gdim), lambda i: (0, 0, 0)),
        pl.BlockSpec((1, dp), lambda i: (0, 0)),
    ]
    args += [pool_w, pool_scale]
    return pl.pallas_call(
        functools.partial(_pool_body, gdim=gdim, pos_offset=pos_offset,
                          tiles_per_batch=tiles_per_batch, has_halo=has_halo),
        grid=(rows // tm,),
        in_specs=in_specs,
        out_specs=pl.BlockSpec((tm, dp), lambda i: (i, 0)),
        out_shape=jax.ShapeDtypeStruct((rows, dp), BF16),
        scratch_shapes=[pltpu.VMEM((tm + HALO_ROWS, dp), F32)],
        compiler_params=_params(("arbitrary",)),
        name=name,
    )(*args)


def _mix_body(a_ref, p_ref, ga_ref, gp_ref, x_ref, wa_ref, wp_ref, wo_ref, g_ref, o_ref):
    ya = jnp.dot(a_ref[...], wa_ref[...], preferred_element_type=F32)
    yp = jnp.dot(p_ref[...], wp_ref[...], preferred_element_type=F32)
    m = (jax.nn.sigmoid(ga_ref[...].astype(F32)) * ya
         + jax.nn.sigmoid(gp_ref[...].astype(F32)) * yp)
    mo = jnp.dot(m.astype(BF16), wo_ref[...], preferred_element_type=F32)
    o_ref[...] = x_ref[...] + _rms(mo, g_ref[...])


def _mix(attn, pool, z, x, w_attn_o, w_pool_o, w_out, g_post, *, gate_block, name):
    rows, d = x.shape
    da = attn.shape[1]
    dp = pool.shape[1]
    tm = _pick_tile(rows, 256)
    resident = dict(pipeline_mode=pl.Buffered(1))
    return pl.pallas_call(
        _mix_body,
        grid=(rows // tm,),
        in_specs=[
            pl.BlockSpec((tm, da), lambda i: (i, 0)),
            pl.BlockSpec((tm, dp), lambda i: (i, 0)),
            pl.BlockSpec((tm, d), lambda i: (i, gate_block)),
            pl.BlockSpec((tm, d), lambda i: (i, gate_block + 1)),
            pl.BlockSpec((tm, d), lambda i: (i, 0)),
            pl.BlockSpec((da, d), lambda i: (0, 0), **resident),
            pl.BlockSpec((dp, d), lambda i: (0, 0), **resident),
            pl.BlockSpec((d, d), lambda i: (0, 0), **resident),
            pl.BlockSpec((1, d), lambda i: (0, 0)),
        ],
        out_specs=pl.BlockSpec((tm, d), lambda i: (i, 0)),
        out_shape=jax.ShapeDtypeStruct((rows, d), F32),
        compiler_params=_params(("arbitrary",)),
        name=name,
    )(attn, pool, z, z, x, w_attn_o, w_pool_o, w_out, g_post)


def _gelu_tanh(x):
    c = math.sqrt(2.0 / math.pi)
    v = (x * x).astype(F32) * (0.044715 * c) + c
    return (0.5 * x) * (1.0 + jnp.tanh(x * v.astype(x.dtype)))


def _ffn_body(ru_ref, rprev_ref, rpre_ref, rd_ref, gpre_ref, wg_ref, wv_ref, cwg_ref, cwv_ref,
              cbg_ref, cbv_ref, wd_ref, gpost_ref, o_ref, h_ref, gate_ref, act0_ref, act1_ref, *, nf,
              n_steps, tiles_per_batch):
    t = pl.program_id(0)
    tu = jnp.minimum(t, n_steps - 1)
    fu = tu % nf
    iu = tu // nf
    fd = jnp.maximum(t - 1, 0) % nf
    tm = o_ref.shape[0]

    @pl.when(t == 0)
    def _():
        act1_ref[...] = jnp.zeros(act1_ref.shape, act1_ref.dtype)

    @pl.when(jnp.logical_and(fu == 0, t < n_steps))
    def _():
        first = (iu % tiles_per_batch) == 0
        halo = jnp.where(first, rpre_ref[...], rprev_ref[...])
        h_ref[0:HALO_ROWS, :] = _rms(halo, gpre_ref[...]).astype(BF16)
        _rms_rows_to(h_ref, HALO_ROWS, ru_ref, gpre_ref, 128)

    @pl.when(fd == 0)
    def _():
        o_ref[...] = jnp.zeros(o_ref.shape, o_ref.dtype)

    def up_conv(w_ref, cw_ref, cb_ref):
        a = jnp.dot(h_ref[...], w_ref[...], preferred_element_type=F32)
        a0, a1, a2 = [(a if k == 0 else pltpu.roll(a, k, axis=0))[HALO_ROWS:, :].astype(BF16)
                      for k in range(3)]
        cw = cw_ref[...].astype(BF16)
        out = cb_ref[...].astype(BF16) + a2 * cw[0:1, :]
        out = out + a1 * cw[1:2, :]
        out = out + a0 * cw[2:3, :]
        return out

    def step(act_u_ref, act_d_ref):
        gate_ref[...] = up_conv(wg_ref, cwg_ref, cbg_ref)
        val = up_conv(wv_ref, cwv_ref, cbv_ref)
        act_u_ref[...] = _gelu_tanh(gate_ref[...]) * val
        o_ref[...] += jnp.dot(act_d_ref[...], wd_ref[...], preferred_element_type=F32)

    @pl.when(t % 2 == 0)
    def _():
        step(act0_ref, act1_ref)

    @pl.when(t % 2 == 1)
    def _():
        step(act1_ref, act0_ref)

    @pl.when(jnp.logical_and(fd == nf - 1, t > 0))
    def _():
        chunk = min(128, tm)

        def body(c, carry):
            r0 = pl.multiple_of(c * chunk, chunk)
            rows = pl.ds(r0, chunk)
            o_ref[rows, :] = rd_ref[rows, :] + _rms(o_ref[rows, :], gpost_ref[...])
            return carry

        lax.fori_loop(0, tm // chunk, body, 0)


def _conv_ffn(r, r_prefix, g_pre, w_up, conv_w, conv_b, w_down, g_post, *, batches, name):
    rows, d = r.shape
    dff = w_down.shape[0]
    seq = rows // batches
    tm = _pick_tile(seq, 512)
    tf = w_up.shape[2]
    nf = dff // tf
    assert nf >= 2
    n_steps = (rows // tm) * nf
    tiles_per_batch = seq // tm
    halo_blocks = tm // HALO_ROWS
    kw = conv_w.shape[0]

    def up_item(t):
        tu = jnp.minimum(t, n_steps - 1)
        return tu // nf, tu % nf

    def down_item(t):
        td = jnp.maximum(t - 1, 0)
        return td // nf, td % nf

    return pl.pallas_call(
        functools.partial(_ffn_body, nf=nf, n_steps=n_steps, tiles_per_batch=tiles_per_batch),
        grid=(n_steps + 1,),
        in_specs=[
            pl.BlockSpec((tm, d), lambda t: (up_item(t)[0], 0)),
            pl.BlockSpec((HALO_ROWS, d), lambda t: (jnp.maximum(up_item(t)[0] * halo_blocks - 1, 0), 0)),
            pl.BlockSpec((HALO_ROWS, d), lambda t: (r_prefix.shape[0] // HALO_ROWS - 1, 0)),
            pl.BlockSpec((tm, d), lambda t: (down_item(t)[0], 0)),
            pl.BlockSpec((1, d), lambda t: (0, 0)),
            pl.BlockSpec((None, d, tf), lambda t: (up_item(t)[1], 0, 0)),
            pl.BlockSpec((None, d, tf), lambda t: (nf + up_item(t)[1], 0, 0)),
            pl.BlockSpec((kw, tf), lambda t: (0, up_item(t)[1])),
            pl.BlockSpec((kw, tf), lambda t: (0, nf + up_item(t)[1])),
            pl.BlockSpec((1, tf), lambda t: (0, up_item(t)[1])),
            pl.BlockSpec((1, tf), lambda t: (0, nf + up_item(t)[1])),
            pl.BlockSpec((tf, d), lambda t: (down_item(t)[1], 0)),
            pl.BlockSpec((1, d), lambda t: (0, 0)),
        ],
        out_specs=pl.BlockSpec((tm, d), lambda t: (down_item(t)[0], 0)),
        out_shape=jax.ShapeDtypeStruct((rows, d), F32),
        scratch_shapes=[
            pltpu.VMEM((HALO_ROWS + tm, d), BF16),
            pltpu.VMEM((tm, tf), BF16),
            pltpu.VMEM((tm, tf), BF16),
            pltpu.VMEM((tm, tf), BF16),
        ],
        compiler_params=_params(("arbitrary",)),
        name=name,
    )(r, r, r_prefix, r, g_pre, w_up, w_up, conv_w, conv_w, conv_b, conv_b, w_down, g_post)


def kernel(x, meta_tokens, mix_pre_g, w_in, b_in, w_attn_o, pool_w, pool_scale, w_pool_o, w_out,
           mix_post_g, ffn_pre_g, w_ffn_up, ffn_conv_w, ffn_conv_b, w_ffn_down, ffn_post_g):
    batches, seq, d = x.shape
    n_meta = meta_tokens.shape[0]
    assert w_in.shape[0] == 1, "single-layer block"
    assert n_meta == HALO_ROWS and n_meta >= max(POOL_WINDOWS) and ffn_conv_w.shape[1] <= HALO_ROWS
    d_attn = w_attn_o.shape[1]
    d_pool = w_pool_o.shape[1]
    heads = d_attn // HEAD_DIM
    assert d_attn == d_pool and 2 * d_attn == d
    assert w_in.shape[2] == 3 * d_attn + heads + d_pool + 2 * d

    f0 = 3 * d_attn
    w_t = w_in[0].T
    b = b_in[0]
    b_main = jnp.concatenate([b[:f0], b[f0 + heads:]])[None, :]
    b_f = jnp.pad(b[f0:f0 + heads], (0, LANES - heads))[None, :]
    meta_rows = meta_tokens.astype(x.dtype)
    w_main, w_f, z_m, flog_m = _stage_in_weights(w_t, meta_rows, mix_pre_g, b_main, b_f,
                                                  skip_start=f0, skip=heads, name="stage_w_in")
    pw = pool_w[0].astype(BF16)
    ffn_tile = _pick_tile(w_ffn_down.shape[1], 512)
    later_weights = ((w_attn_o[0], None), (w_pool_o[0], None), (w_out[0], None),
                     (w_ffn_up[0], ffn_tile), (w_ffn_down[0], None))
    ublock = f0 // d_pool
    gate_block = (f0 + d_pool) // d

    def in_project(rows_in, nb, tag, casts=()):
        s = rows_in.shape[0] // nb
        return _norm_matmul(rows_in, mix_pre_g, w_main, b_main, w_f, b_f, casts,
                            tm=_pick_tile(s, 1024), tn=_pick_tile(w_main.shape[0], 1024),
                            name=f"inproj_{tag}")

    def mixer(rows_in, projected, prefix, nb, tag):
        n, _ = rows_in.shape
        z, flog = projected
        if prefix is None:
            flog = jnp.pad(flog, ((0, LANES - n), (0, 0)))
            c_col, c_row = _forget_cumsum(flog, batches=nb, heads=heads, name=f"cumsum_{tag}")
            attn = _attention_prefix(z, c_row[0, :, :n].reshape(heads, 1, n), heads=heads,
                                     name=f"attention_{tag}")
        else:
            c_col, _ = _forget_cumsum(flog, batches=nb, heads=heads, name=f"cumsum_{tag}")
            attn = _attention_main(z, c_col, prefix[0], prefix[1], batches=nb, heads=heads,
                                   name=f"attention_{tag}")
        pool = _pool(z, pw, pool_scale, None if prefix is None else prefix[0],
                     batches=nb, ublock=ublock, name=f"pool_{tag}")
        r = _mix(attn, pool, z, rows_in, wa, wp, wo, mix_post_g,
                 gate_block=gate_block, name=f"mix_{tag}")
        return r, z, c_col

    x_rows = x.reshape(batches * seq, d)
    z_x, flog_x, (wa, wp, wo, w_up, w_down) = in_project(x_rows, batches, "main", later_weights)
    r_meta, z_meta, c_meta = mixer(meta_rows, (z_m, flog_m), None, 1, "prefix")
    r, _, _ = mixer(x_rows, (z_x, flog_x), (z_meta, c_meta), batches, "main")
    out = _conv_ffn(r, r_meta, ffn_pre_g, w_up, ffn_conv_w[0], ffn_conv_b, w_down, ffn_post_g,
                    batches=batches, name="conv_ffn")
    return out.reshape(batches, seq, d)
```

```python
import functools
import math

import jax
import jax.numpy as jnp
from jax import lax
from jax.experimental import pallas as pl
from jax.experimental.pallas import tpu as pltpu

HEAD_DIM = 128
POOL_WINDOWS = (2, 4, 8, 16)
HALO_ROWS = 16
EPS = 1e-6
MASK_VALUE = -1e30
F32 = jnp.float32
BF16 = jnp.bfloat16
LANES = 128
LOG2E = math.log2(math.e)
N_SPLIT = 3
SUM_ROWS = 16
VMEM_LIMIT_BYTES = 56 * 1024 * 1024


def _pick_tile(n, preferred):
    t = min(preferred, n)
    while n % t:
        t //= 2
    return t


def _params(semantics, vmem=VMEM_LIMIT_BYTES):
    return pltpu.CompilerParams(dimension_semantics=semantics, vmem_limit_bytes=vmem)


def _rms(x, g):
    ms = jnp.mean(x * x, axis=-1, keepdims=True)
    return x * lax.rsqrt(ms + EPS) * g


def _rms_rows_to(dst_ref, dst_row0, src_ref, g_ref, chunk):
    rows = src_ref.shape[0]
    chunk = min(chunk, rows)

    def body(c, carry):
        r0 = pl.multiple_of(c * chunk, chunk)
        x = src_ref[pl.ds(r0, chunk), :]
        dst_ref[pl.ds(dst_row0 + r0, chunk), :] = _rms(x, g_ref[...]).astype(dst_ref.dtype)
        return carry

    lax.fori_loop(0, rows // chunk, body, 0)


def _norm_matmul_body(x_ref, g_ref, w_ref, b_ref, ws_ref, bs_ref, *rest, n_casts):
    cast_in = rest[:n_casts]
    o_ref, os_ref = rest[n_casts:n_casts + 2]
    cast_out = rest[n_casts + 2:2 * n_casts + 2]
    h_ref = rest[-1]

    @pl.when(pl.program_id(1) == 0)
    def _():
        _rms_rows_to(h_ref, 0, x_ref, g_ref, 128)
        os_ref[...] = lax.dot_general(h_ref[...], ws_ref[...], (((1,), (1,)), ((), ())),
                                      preferred_element_type=F32) + bs_ref[...]

    acc = lax.dot_general(h_ref[...], w_ref[...], (((1,), (1,)), ((), ())),
                          preferred_element_type=F32)
    o_ref[...] = (acc + b_ref[...]).astype(o_ref.dtype)

    for src, dst in zip(cast_in, cast_out):
        dst[...] = src[...].astype(dst.dtype)


def _norm_matmul(x, g, w, b, w_side, b_side, casts=(), *, tm, tn, name):
    m, d = x.shape
    n = w.shape[0]
    ns = w_side.shape[0]
    grid = (m // tm, n // tn)
    n_steps = grid[0] * grid[1]
    cast_specs = []
    for a in casts:
        chunk = next(c for c in range(16, a.shape[0] + 1, 16)
                     if a.shape[0] % c == 0 and a.shape[0] // c <= n_steps)
        last = a.shape[0] // chunk - 1
        cast_specs.append(pl.BlockSpec(
            (chunk, a.shape[1]), lambda i, j, last=last: (jnp.minimum(i * grid[1] + j, last), 0)))
    outs = pl.pallas_call(
        functools.partial(_norm_matmul_body, n_casts=len(casts)),
        grid=grid,
        in_specs=[
            pl.BlockSpec((tm, d), lambda i, j: (i, 0)),
            pl.BlockSpec((1, d), lambda i, j: (0, 0)),
            pl.BlockSpec((tn, d), lambda i, j: (j, 0)),
            pl.BlockSpec((1, tn), lambda i, j: (0, j)),
            pl.BlockSpec((ns, d), lambda i, j: (0, 0)),
            pl.BlockSpec((1, ns), lambda i, j: (0, 0)),
        ] + cast_specs,
        out_specs=[
            pl.BlockSpec((tm, tn), lambda i, j: (i, j)),
            pl.BlockSpec((tm, ns), lambda i, j: (i, 0)),
        ] + cast_specs,
        out_shape=[
            jax.ShapeDtypeStruct((m, n), BF16),
            jax.ShapeDtypeStruct((m, ns), F32),
        ] + [jax.ShapeDtypeStruct(a.shape, BF16) for a in casts],
        scratch_shapes=[pltpu.VMEM((tm, d), BF16)],
        compiler_params=_params(("arbitrary", "arbitrary")),
        name=name,
    )(x, g, w, b, w_side, b_side, *casts)
    return outs[0], outs[1], outs[2:]


def _stage_body(a_ref, b_ref, f_ref, x_ref, g_ref, bias_ref, bias_f_ref, o_ref, of_ref, z_ref, zf_ref,
                h_ref, *, first_shifted, shift):
    j = pl.program_id(0)
    nt = (((1,), (1,)), ((), ()))

    @pl.when(j == 0)
    def _():
        pad = jnp.zeros((of_ref.shape[0] - shift, of_ref.shape[1]), f_ref.dtype)
        wf = jnp.concatenate([f_ref[...], pad], axis=0).astype(of_ref.dtype)
        of_ref[...] = wf
        h_ref[...] = _rms(x_ref[...], g_ref[...]).astype(h_ref.dtype)
        zf_ref[...] = lax.dot_general(h_ref[...], wf, nt, preferred_element_type=F32) + bias_f_ref[...]

    def emit(w_tile):
        o_ref[...] = w_tile
        z = lax.dot_general(h_ref[...], w_tile, nt, preferred_element_type=F32) + bias_ref[...]
        z_ref[...] = z.astype(z_ref.dtype)

    @pl.when(j < first_shifted)
    def _():
        emit(a_ref[...].astype(o_ref.dtype))

    @pl.when(j >= first_shifted)
    def _():
        emit(jnp.concatenate([a_ref[shift:, :], b_ref[...]], axis=0).astype(o_ref.dtype))


def _stage_in_weights(wt, x_prefix, g, bias, bias_f, *, skip_start, skip, name):
    n, d = wt.shape
    npre = x_prefix.shape[0]
    tn = _pick_tile(skip_start, 1024)
    n_out = n - skip
    assert skip % 8 == 0 and n_out % tn == 0 and skip_start % tn == 0
    edge = tn // skip
    return pl.pallas_call(
        functools.partial(_stage_body, first_shifted=skip_start // tn, shift=skip),
        grid=(n_out // tn,),
        in_specs=[
            pl.BlockSpec((tn, d), lambda j: (j, 0)),
            pl.BlockSpec((skip, d), lambda j: ((j + 1) * edge, 0)),
            pl.BlockSpec((skip, d), lambda j: (skip_start // skip, 0)),
            pl.BlockSpec((npre, d), lambda j: (0, 0)),
            pl.BlockSpec((1, d), lambda j: (0, 0)),
            pl.BlockSpec((1, tn), lambda j: (0, j)),
            pl.BlockSpec((1, LANES), lambda j: (0, 0)),
        ],
        out_specs=[
            pl.BlockSpec((tn, d), lambda j: (j, 0)),
            pl.BlockSpec((LANES, d), lambda j: (0, 0)),
            pl.BlockSpec((npre, tn), lambda j: (0, j)),
            pl.BlockSpec((npre, LANES), lambda j: (0, 0)),
        ],
        out_shape=[
            jax.ShapeDtypeStruct((n_out, d), BF16),
            jax.ShapeDtypeStruct((LANES, d), BF16),
            jax.ShapeDtypeStruct((npre, n_out), BF16),
            jax.ShapeDtypeStruct((npre, LANES), F32),
        ],
        scratch_shapes=[pltpu.VMEM((npre, d), BF16)],
        compiler_params=_params(("arbitrary",)),
        name=name,
    )(wt, wt, wt, x_prefix, g, bias, bias_f)


def _cum_body(f_ref, col_ref, row_ref, *, heads):
    x = f_ref[...]
    rows = x.shape[0]
    c = jnp.minimum(x, 0.0) - jnp.log1p(jnp.exp(-jnp.abs(x)))
    t = lax.broadcasted_iota(jnp.int32, c.shape, 0)
    shift = 1
    while shift < rows:
        c = c + jnp.where(t >= shift, pltpu.roll(c, shift, axis=0), 0.0)
        shift *= 2
    col_ref[...] = c
    row_ref[...] = c.T[:heads, :]


def _forget_cumsum(flog, *, batches, heads, name):
    rows = flog.shape[0] // batches
    return pl.pallas_call(
        functools.partial(_cum_body, heads=heads),
        grid=(batches,),
        in_specs=[pl.BlockSpec((rows, flog.shape[1]), lambda b: (b, 0))],
        out_specs=[
            pl.BlockSpec((rows, flog.shape[1]), lambda b: (b, 0)),
            pl.BlockSpec((None, heads, rows), lambda b: (b, 0, 0)),
        ],
        out_shape=[
            jax.ShapeDtypeStruct(flog.shape, F32),
            jax.ShapeDtypeStruct((batches, heads, rows), F32),
        ],
        compiler_params=_params(("arbitrary",)),
        name=name,
    )(flog)


def _attn_prefix_body(q_ref, k_ref, v_ref, c_ref, o_ref, *, scale):
    n = q_ref.shape[0]
    q = (q_ref[...].astype(F32) * scale).astype(BF16)
    s = lax.dot_general(q, k_ref[...], (((1,), (1,)), ((), ())), preferred_element_type=F32)
    s = s - c_ref[...]
    row = lax.broadcasted_iota(jnp.int32, (n, n), 0)
    col = lax.broadcasted_iota(jnp.int32, (n, n), 1)
    s = jnp.where(row >= col, s, MASK_VALUE)
    p = jnp.exp(s - jnp.max(s, axis=-1, keepdims=True))
    l = jnp.sum(p, axis=-1, keepdims=True)
    acc = jnp.dot(p.astype(BF16), v_ref[...], preferred_element_type=F32)
    o_ref[...] = (acc / l).astype(o_ref.dtype)


def _attention_prefix(z, c_row, *, heads, name):
    n = z.shape[0]
    return pl.pallas_call(
        functools.partial(_attn_prefix_body, scale=HEAD_DIM ** -0.5),
        grid=(heads,),
        in_specs=[
            pl.BlockSpec((n, HEAD_DIM), lambda h: (0, h)),
            pl.BlockSpec((n, HEAD_DIM), lambda h: (0, heads + h)),
            pl.BlockSpec((n, HEAD_DIM), lambda h: (0, 2 * heads + h)),
            pl.BlockSpec((None, 1, n), lambda h: (h, 0, 0)),
        ],
        out_specs=pl.BlockSpec((n, HEAD_DIM), lambda h: (0, h)),
        out_shape=jax.ShapeDtypeStruct((n, heads * HEAD_DIM), BF16),
        compiler_params=_params(("arbitrary",)),
        name=name,
    )(z, z, z, c_row)


def _split3(x):
    p1 = x.astype(BF16)
    r1 = x - p1.astype(F32)
    p2 = r1.astype(BF16)
    p3 = (r1 - p2.astype(F32)).astype(BF16)
    return p1, p2, p3


def _bias_block(x, n_heads):
    lane = lax.broadcasted_iota(jnp.int32, x.shape, 1)
    pieces = _split3(jnp.where(lane < n_heads, x, 0.0))
    block = pieces[0].astype(F32)
    for k in range(1, N_SPLIT):
        block = block + pltpu.roll(pieces[k].astype(F32), k * n_heads, axis=1)
    return block.astype(BF16)


def _attn_main_body(q_ref, k_ref, v_ref, c_ref, kp_ref, vp_ref, cp_ref, o_ref,
                    e_ref, vt_ref, ep_ref, vpt_ref, qaug_ref, m_ref, acc_ref, alpha_ref, p_ref,
                    *, tq, hp, n_heads, scale):
    seq = q_ref.shape[0]
    npre = kp_ref.shape[0]
    group = pl.program_id(1)

    e_ref[...] = _bias_block(c_ref[...] * (-LOG2E), n_heads)
    cp = cp_ref[0:npre, :]
    ep_ref[...] = _bias_block((cp - cp[npre - 1:npre, :]) * (-LOG2E), n_heads)

    def sum_rows(n):
        r = lax.broadcasted_iota(jnp.int32, (SUM_ROWS, n), 0)
        return jnp.where(r == 0, 1.0, 0.0).astype(BF16)

    for hh in range(hp):
        lanes = slice(hh * HEAD_DIM, (hh + 1) * HEAD_DIM)
        vt_ref[hh, 0:HEAD_DIM, :] = v_ref[:, lanes].astype(F32).T.astype(BF16)
        vt_ref[hh, HEAD_DIM:, :] = sum_rows(seq)
        vp = jnp.concatenate([vp_ref[:, lanes].astype(F32),
                              jnp.zeros((LANES - npre, HEAD_DIM), F32)], axis=0)
        vpt_ref[hh, 0:HEAD_DIM, :] = vp.T.astype(BF16)
        vpt_ref[hh, HEAD_DIM:, :] = sum_rows(LANES)

    lane = lax.broadcasted_iota(jnp.int32, (tq, HEAD_DIM), 1)
    key = lax.broadcasted_iota(jnp.int32, (tq, tq), 0)
    qry = lax.broadcasted_iota(jnp.int32, (tq, tq), 1)
    causal = key <= qry
    heads = range(hp)

    def k_aug(hh, k0):
        lanes = slice(hh * HEAD_DIM, (hh + 1) * HEAD_DIM)
        return jnp.concatenate([k_ref[pl.ds(k0, tq), lanes], e_ref[pl.ds(k0, tq), :]], axis=1)

    def scores_t(kaug, hh):
        return lax.dot_general(kaug, qaug_ref[hh], (((1,), (1,)), ((), ())),
                               preferred_element_type=F32)

    def softmax_update(hh, st):
        m = m_ref[hh]
        m_new = jnp.maximum(m, jnp.max(st, axis=0, keepdims=True))
        alpha = jnp.exp2(m - m_new)
        m_ref[hh] = m_new
        return alpha, jnp.exp2((st - m_new).astype(BF16))

    def value_update(hh, k_pend):
        vt = vt_ref[hh, :, pl.ds(k_pend, tq)]
        acc_ref[hh] = alpha_ref[hh] * acc_ref[hh] + jnp.dot(vt, p_ref[hh], preferred_element_type=F32)

    def q_tile(qi, carry_unused):
        q0 = pl.multiple_of(qi * tq, tq)
        for hh in heads:
            lanes = slice(hh * HEAD_DIM, (hh + 1) * HEAD_DIM)
            head = group * hp + hh
            q = (q_ref[pl.ds(q0, tq), lanes].astype(F32) * (scale * LOG2E)).astype(BF16)
            picks = (lane == head) | (lane == n_heads + head) | (lane == 2 * n_heads + head)
            qaug_ref[hh] = jnp.concatenate([q, jnp.where(picks, 1.0, 0.0).astype(BF16)], axis=1)

        sts = []
        for hh in heads:
            lanes = slice(hh * HEAD_DIM, (hh + 1) * HEAD_DIM)
            kp_aug = jnp.concatenate([kp_ref[:, lanes], ep_ref[...]], axis=1)
            sts.append(scores_t(jnp.concatenate([kp_aug, k_aug(hh, q0)], axis=0), hh))
        for hh in heads:
            st_pre = sts[hh][0:npre, :]
            st = jnp.where(causal, sts[hh][npre:, :], MASK_VALUE)
            m = jnp.maximum(jnp.max(st, axis=0, keepdims=True), jnp.max(st_pre, axis=0, keepdims=True))
            m_ref[hh] = m
            p_ref[hh] = jnp.exp2((st - m).astype(BF16))
            alpha_ref[hh] = jnp.ones_like(m)
            p_pre = jnp.concatenate([jnp.exp2((st_pre - m).astype(BF16)),
                                     jnp.zeros((LANES - npre, tq), BF16)], axis=0)
            acc_ref[hh] = jnp.dot(vpt_ref[hh], p_pre, preferred_element_type=F32)

        def kv_tile(j, k_pend):
            k0 = pl.multiple_of(j * tq, tq)
            sts = [scores_t(k_aug(hh, k0), hh) for hh in heads]
            for hh in heads:
                value_update(hh, pl.multiple_of(k_pend, tq))
            for hh in heads:
                alpha_ref[hh], p_ref[hh] = softmax_update(hh, sts[hh])
            return k0

        k_pend = lax.fori_loop(0, qi, kv_tile, q0)

        for hh in heads:
            value_update(hh, pl.multiple_of(k_pend, tq))
        for hh in heads:
            lanes = slice(hh * HEAD_DIM, (hh + 1) * HEAD_DIM)
            acc = acc_ref[hh]
            out = acc[0:HEAD_DIM, :] / acc[HEAD_DIM:HEAD_DIM + 1, :]
            o_ref[pl.ds(q0, tq), lanes] = out.T.astype(o_ref.dtype)
        return carry_unused

    lax.fori_loop(0, seq // tq, q_tile, 0)


def _attention_main(z, c_col, z_prefix, c_col_prefix, *, batches, heads, name):
    seq = z.shape[0] // batches
    npre = z_prefix.shape[0]
    tq = _pick_tile(seq, 256)
    hp = _pick_tile(heads, 8)
    assert N_SPLIT * heads <= LANES
    groups = heads // hp
    width = hp * HEAD_DIM
    return pl.pallas_call(
        functools.partial(_attn_main_body, tq=tq, hp=hp, n_heads=heads, scale=HEAD_DIM ** -0.5),
        grid=(batches, groups),
        in_specs=[
            pl.BlockSpec((seq, width), lambda b, g: (b, g)),
            pl.BlockSpec((seq, width), lambda b, g: (b, groups + g)),
            pl.BlockSpec((seq, width), lambda b, g: (b, 2 * groups + g)),
            pl.BlockSpec((seq, LANES), lambda b, g: (b, 0)),
            pl.BlockSpec((npre, width), lambda b, g: (0, groups + g)),
            pl.BlockSpec((npre, width), lambda b, g: (0, 2 * groups + g)),
            pl.BlockSpec(c_col_prefix.shape, lambda b, g: (0, 0)),
        ],
        out_specs=pl.BlockSpec((seq, width), lambda b, g: (b, g)),
        out_shape=jax.ShapeDtypeStruct((batches * seq, heads * HEAD_DIM), BF16),
        scratch_shapes=[
            pltpu.VMEM((seq, LANES), BF16),
            pltpu.VMEM((hp, HEAD_DIM + SUM_ROWS, seq), BF16),
            pltpu.VMEM((npre, LANES), BF16),
            pltpu.VMEM((hp, HEAD_DIM + SUM_ROWS, LANES), BF16),
            pltpu.VMEM((hp, tq, 2 * HEAD_DIM), BF16),
            pltpu.VMEM((hp, 1, tq), F32),
            pltpu.VMEM((hp, HEAD_DIM + SUM_ROWS, tq), F32),
            pltpu.VMEM((hp, 1, tq), F32),
            pltpu.VMEM((hp, tq, tq), BF16),
        ],
        compiler_params=_params(("arbitrary", "arbitrary")),
        name=name,
    )(z, z, z, c_col, z_prefix, z_prefix, c_col_prefix)


def _pool_body(u_ref, *rest, gdim, pos_offset, tiles_per_batch, has_halo):
    if has_halo:
        uprev_ref, upre_ref, pw_ref, ps_ref, o_ref, ext_ref = rest
    else:
        pw_ref, ps_ref, o_ref, ext_ref = rest
    tm = u_ref.shape[0]
    if has_halo:
        first = (pl.program_id(0) % tiles_per_batch) == 0
        halo = jnp.where(first, upre_ref[...], uprev_ref[...]).astype(F32)
    else:
        halo = jnp.zeros((HALO_ROWS, u_ref.shape[1]), F32)
    ext_ref[0:HALO_ROWS, :] = halo
    ext_ref[HALO_ROWS:HALO_ROWS + tm, :] = u_ref[...].astype(F32)

    for g, w in enumerate(POOL_WINDOWS):
        cols = slice(g * gdim, (g + 1) * gdim)
        wsum = ext_ref[:, cols]
        shift = 1
        while shift < w:
            wsum = wsum + pltpu.roll(wsum, shift, axis=0)
            shift *= 2
        wsum = wsum[HALO_ROWS:, :]
        cur = ext_ref[HALO_ROWS:HALO_ROWS + tm, cols]
        if pos_offset + 1 >= w:
            mean = wsum * (1.0 / w)
        else:
            pos = lax.broadcasted_iota(jnp.int32, (tm, 1), 0) + pos_offset
            mean = wsum / jnp.minimum(pos + 1, w).astype(F32)
        d = (mean - cur).astype(BF16)
        y = jnp.dot(d, pw_ref[g], preferred_element_type=F32) * ps_ref[:, cols]
        o_ref[:, cols] = y.astype(o_ref.dtype)


def _pool(z, pool_w, pool_scale, prefix_z, *, batches, ublock, name):
    groups, gdim, _ = pool_w.shape
    dp = groups * gdim
    rows = z.shape[0]
    seq = rows // batches
    has_halo = prefix_z is not None
    tm = _pick_tile(seq, 512)
    tiles_per_batch = seq // tm
    in_specs = [pl.BlockSpec((tm, dp), lambda i: (i, ublock))]
    args = [z]
    if has_halo:
        halo_blocks = tm // HALO_ROWS
        in_specs += [
            pl.BlockSpec((HALO_ROWS, dp), lambda i: (jnp.maximum(i * halo_blocks - 1, 0), ublock)),
            pl.BlockSpec((HALO_ROWS, dp), lambda i: (prefix_z.shape[0] // HALO_ROWS - 1, ublock)),
        ]
        args += [z, prefix_z]
        pos_offset = prefix_z.shape[0]
    else:
        pos_offset = 0
    in_specs += [
        pl.BlockSpec((groups, gdim, gdim), lambda i: (0, 0, 0)),
        pl.BlockSpec((1, dp), lambda i: (0, 0)),
    ]
    args += [pool_w, pool_scale]
    return pl.pallas_call(
        functools.partial(_pool_body, gdim=gdim, pos_offset=pos_offset,
                          tiles_per_batch=tiles_per_batch, has_halo=has_halo),
        grid=(rows // tm,),
        in_specs=in_specs,
        out_specs=pl.BlockSpec((tm, dp), lambda i: (i, 0)),
        out_shape=jax.ShapeDtypeStruct((rows, dp), BF16),
        scratch_shapes=[pltpu.VMEM((tm + HALO_ROWS, dp), F32)],
        compiler_params=_params(("arbitrary",)),
        name=name,
    )(*args)


def _mix_body(a_ref, p_ref, ga_ref, gp_ref, x_ref, wa_ref, wp_ref, wo_ref, g_ref, o_ref):
    ya = jnp.dot(a_ref[...], wa_ref[...], preferred_element_type=F32)
    yp = jnp.dot(p_ref[...], wp_ref[...], preferred_element_type=F32)
    m = (jax.nn.sigmoid(ga_ref[...].astype(F32)) * ya
         + jax.nn.sigmoid(gp_ref[...].astype(F32)) * yp)
    mo = jnp.dot(m.astype(BF16), wo_ref[...], preferred_element_type=F32)
    o_ref[...] = x_ref[...] + _rms(mo, g_ref[...])


def _mix(attn, pool, z, x, w_attn_o, w_pool_o, w_out, g_post, *, gate_block, name):
    rows, d = x.shape
    da = attn.shape[1]
    dp = pool.shape[1]
    tm = _pick_tile(rows, 256)
    resident = dict(pipeline_mode=pl.Buffered(1))
    return pl.pallas_call(
        _mix_body,
        grid=(rows // tm,),
        in_specs=[
            pl.BlockSpec((tm, da), lambda i: (i, 0)),
            pl.BlockSpec((tm, dp), lambda i: (i, 0)),
            pl.BlockSpec((tm, d), lambda i: (i, gate_block)),
            pl.BlockSpec((tm, d), lambda i: (i, gate_block + 1)),
            pl.BlockSpec((tm, d), lambda i: (i, 0)),
            pl.BlockSpec((da, d), lambda i: (0, 0), **resident),
            pl.BlockSpec((dp, d), lambda i: (0, 0), **resident),
            pl.BlockSpec((d, d), lambda i: (0, 0), **resident),
            pl.BlockSpec((1, d), lambda i: (0, 0)),
        ],
        out_specs=pl.BlockSpec((tm, d), lambda i: (i, 0)),
        out_shape=jax.ShapeDtypeStruct((rows, d), F32),
        compiler_params=_params(("arbitrary",)),
        name=name,
    )(attn, pool, z, z, x, w_attn_o, w_pool_o, w_out, g_post)


def _gelu_tanh(x):
    c = math.sqrt(2.0 / math.pi)
    v = (x * x).astype(F32) * (0.044715 * c) + c
    return (0.5 * x) * (1.0 + jnp.tanh(x * v.astype(x.dtype)))


def _ffn_body(ru_ref, rprev_ref, rpre_ref, rd_ref, gpre_ref, wg_ref, wv_ref, cwg_ref, cwv_ref,
              cbg_ref, cbv_ref, wd_ref, gpost_ref, o_ref, h_ref, gate_ref, act0_ref, act1_ref, *, nf,
              n_steps, tiles_per_batch):
    t = pl.program_id(0)
    tu = jnp.minimum(t, n_steps - 1)
    fu = tu % nf
    iu = tu // nf
    fd = jnp.maximum(t - 1, 0) % nf
    tm = o_ref.shape[0]

    @pl.when(t == 0)
    def _():
        act1_ref[...] = jnp.zeros(act1_ref.shape, act1_ref.dtype)

    @pl.when(jnp.logical_and(fu == 0, t < n_steps))
    def _():
        first = (iu % tiles_per_batch) == 0
        halo = jnp.where(first, rpre_ref[...], rprev_ref[...])
        h_ref[0:HALO_ROWS, :] = _rms(halo, gpre_ref[...]).astype(BF16)
        _rms_rows_to(h_ref, HALO_ROWS, ru_ref, gpre_ref, 128)

    @pl.when(fd == 0)
    def _():
        o_ref[...] = jnp.zeros(o_ref.shape, o_ref.dtype)

    def conv(a, cw_ref, cb_ref):
        a0, a1, a2 = [(a if k == 0 else pltpu.roll(a, k, axis=0))[HALO_ROWS:, :].astype(BF16)
                      for k in range(3)]
        cw = cw_ref[...].astype(BF16)
        out = cb_ref[...].astype(BF16) + a2 * cw[0:1, :]
        out = out + a1 * cw[1:2, :]
        out = out + a0 * cw[2:3, :]
        return out

    def step(act_u_ref, act_d_ref):
        gate_ref[...] = jnp.dot(h_ref[...], wg_ref[...], preferred_element_type=F32)
        val = conv(jnp.dot(h_ref[...], wv_ref[...], preferred_element_type=F32), cwv_ref, cbv_ref)
        act_u_ref[...] = _gelu_tanh(conv(gate_ref[...], cwg_ref, cbg_ref)) * val
        o_ref[...] += jnp.dot(act_d_ref[...], wd_ref[...], preferred_element_type=F32)

    @pl.when(t % 2 == 0)
    def _():
        step(act0_ref, act1_ref)

    @pl.when(t % 2 == 1)
    def _():
        step(act1_ref, act0_ref)

    @pl.when(jnp.logical_and(fd == nf - 1, t > 0))
    def _():
        chunk = min(128, tm)

        def body(c, carry):
            r0 = pl.multiple_of(c * chunk, chunk)
            rows = pl.ds(r0, chunk)
            o_ref[rows, :] = rd_ref[rows, :] + _rms(o_ref[rows, :], gpost_ref[...])
            return carry

        lax.fori_loop(0, tm // chunk, body, 0)


def _conv_ffn(r, r_prefix, g_pre, w_up, conv_w, conv_b, w_down, g_post, *, batches, name):
    rows, d = r.shape
    dff = w_down.shape[0]
    seq = rows // batches
    tm = _pick_tile(seq, 512)
    tf = _pick_tile(dff, 512)
    nf = dff // tf
    assert nf >= 2
    n_steps = (rows // tm) * nf
    tiles_per_batch = seq // tm
    halo_blocks = tm // HALO_ROWS
    kw = conv_w.shape[0]

    def up_item(t):
        tu = jnp.minimum(t, n_steps - 1)
        return tu // nf, tu % nf

    def down_item(t):
        td = jnp.maximum(t - 1, 0)
        return td // nf, td % nf

    return pl.pallas_call(
        functools.partial(_ffn_body, nf=nf, n_steps=n_steps, tiles_per_batch=tiles_per_batch),
        grid=(n_steps + 1,),
        in_specs=[
            pl.BlockSpec((tm, d), lambda t: (up_item(t)[0], 0)),
            pl.BlockSpec((HALO_ROWS, d), lambda t: (jnp.maximum(up_item(t)[0] * halo_blocks - 1, 0), 0)),
            pl.BlockSpec((HALO_ROWS, d), lambda t: (r_prefix.shape[0] // HALO_ROWS - 1, 0)),
            pl.BlockSpec((tm, d), lambda t: (down_item(t)[0], 0)),
            pl.BlockSpec((1, d), lambda t: (0, 0)),
            pl.BlockSpec((d, tf), lambda t: (0, up_item(t)[1])),
            pl.BlockSpec((d, tf), lambda t: (0, nf + up_item(t)[1])),
            pl.BlockSpec((kw, tf), lambda t: (0, up_item(t)[1])),
            pl.BlockSpec((kw, tf), lambda t: (0, nf + up_item(t)[1])),
            pl.BlockSpec((1, tf), lambda t: (0, up_item(t)[1])),
            pl.BlockSpec((1, tf), lambda t: (0, nf + up_item(t)[1])),
            pl.BlockSpec((tf, d), lambda t: (down_item(t)[1], 0)),
            pl.BlockSpec((1, d), lambda t: (0, 0)),
        ],
        out_specs=pl.BlockSpec((tm, d), lambda t: (down_item(t)[0], 0)),
        out_shape=jax.ShapeDtypeStruct((rows, d), F32),
        scratch_shapes=[
            pltpu.VMEM((HALO_ROWS + tm, d), BF16),
            pltpu.VMEM((HALO_ROWS + tm, tf), F32),
            pltpu.VMEM((tm, tf), BF16),
            pltpu.VMEM((tm, tf), BF16),
        ],
        compiler_params=_params(("arbitrary",)),
        name=name,
    )(r, r, r_prefix, r, g_pre, w_up, w_up, conv_w, conv_w, conv_b, conv_b, w_down, g_post)


def kernel(x, meta_tokens, mix_pre_g, w_in, b_in, w_attn_o, pool_w, pool_scale, w_pool_o, w_out,
           mix_post_g, ffn_pre_g, w_ffn_up, ffn_conv_w, ffn_conv_b, w_ffn_down, ffn_post_g):
    batches, seq, d = x.shape
    n_meta = meta_tokens.shape[0]
    assert w_in.shape[0] == 1, "single-layer block"
    assert n_meta == HALO_ROWS and n_meta >= max(POOL_WINDOWS) and ffn_conv_w.shape[1] <= HALO_ROWS
    d_attn = w_attn_o.shape[1]
    d_pool = w_pool_o.shape[1]
    heads = d_attn // HEAD_DIM
    assert d_attn == d_pool and 2 * d_attn == d
    assert w_in.shape[2] == 3 * d_attn + heads + d_pool + 2 * d

    f0 = 3 * d_attn
    w_t = w_in[0].T
    b = b_in[0]
    b_main = jnp.concatenate([b[:f0], b[f0 + heads:]])[None, :]
    b_f = jnp.pad(b[f0:f0 + heads], (0, LANES - heads))[None, :]
    meta_rows = meta_tokens.astype(x.dtype)
    w_main, w_f, z_m, flog_m = _stage_in_weights(w_t, meta_rows, mix_pre_g, b_main, b_f,
                                                  skip_start=f0, skip=heads, name="stage_w_in")
    pw = pool_w[0].astype(BF16)
    later_weights = (w_attn_o[0], w_pool_o[0], w_out[0], w_ffn_up[0], w_ffn_down[0])
    ublock = f0 // d_pool
    gate_block = (f0 + d_pool) // d

    def in_project(rows_in, nb, tag, casts=()):
        s = rows_in.shape[0] // nb
        return _norm_matmul(rows_in, mix_pre_g, w_main, b_main, w_f, b_f, casts,
                            tm=_pick_tile(s, 1024), tn=_pick_tile(w_main.shape[0], 1024),
                            name=f"inproj_{tag}")

    def mixer(rows_in, projected, prefix, nb, tag):
        n, _ = rows_in.shape
        z, flog = projected
        if prefix is None:
            flog = jnp.pad(flog, ((0, LANES - n), (0, 0)))
            c_col, c_row = _forget_cumsum(flog, batches=nb, heads=heads, name=f"cumsum_{tag}")
            attn = _attention_prefix(z, c_row[0, :, :n].reshape(heads, 1, n), heads=heads,
                                     name=f"attention_{tag}")
        else:
            c_col, _ = _forget_cumsum(flog, batches=nb, heads=heads, name=f"cumsum_{tag}")
            attn = _attention_main(z, c_col, prefix[0], prefix[1], batches=nb, heads=heads,
                                   name=f"attention_{tag}")
        pool = _pool(z, pw, pool_scale, None if prefix is None else prefix[0],
                     batches=nb, ublock=ublock, name=f"pool_{tag}")
        r = _mix(attn, pool, z, rows_in, wa, wp, wo, mix_post_g,
                 gate_block=gate_block, name=f"mix_{tag}")
        return r, z, c_col

    x_rows = x.reshape(batches * seq, d)
    z_x, flog_x, (wa, wp, wo, w_up, w_down) = in_project(x_rows, batches, "main", later_weights)
    r_meta, z_meta, c_meta = mixer(meta_rows, (z_m, flog_m), None, 1, "prefix")
    r, _, _ = mixer(x_rows, (z_x, flog_x), (z_meta, c_meta), batches, "main")
    out = _conv_ffn(r, r_meta, ffn_pre_g, w_up, ffn_conv_w[0], ffn_conv_b, w_down, ffn_post_g,
                    batches=batches, name="conv_ffn")
    return out.reshape(batches, seq, d)
```

```python
import functools
import math

import jax
import jax.numpy as jnp
from jax import lax
from jax.experimental import pallas as pl
from jax.experimental.pallas import tpu as pltpu

HEAD_DIM = 128
POOL_WINDOWS = (2, 4, 8, 16)
HALO_ROWS = 16
EPS = 1e-6
MASK_VALUE = -1e30
F32 = jnp.float32
BF16 = jnp.bfloat16
LANES = 128
LOG2E = math.log2(math.e)
N_SPLIT = 3
SUM_ROWS = 16
VMEM_LIMIT_BYTES = 56 * 1024 * 1024


def _pick_tile(n, preferred):
    t = min(preferred, n)
    while n % t:
        t //= 2
    return t


def _params(semantics, vmem=VMEM_LIMIT_BYTES):
    return pltpu.CompilerParams(dimension_semantics=semantics, vmem_limit_bytes=vmem)


def _rms(x, g):
    ms = jnp.mean(x * x, axis=-1, keepdims=True)
    return x * lax.rsqrt(ms + EPS) * g


def _rms_rows_to(dst_ref, dst_row0, src_ref, g_ref, chunk):
    rows = src_ref.shape[0]
    chunk = min(chunk, rows)

    def body(c, carry):
        r0 = pl.multiple_of(c * chunk, chunk)
        x = src_ref[pl.ds(r0, chunk), :]
        dst_ref[pl.ds(dst_row0 + r0, chunk), :] = _rms(x, g_ref[...]).astype(dst_ref.dtype)
        return carry

    lax.fori_loop(0, rows // chunk, body, 0)


def _norm_matmul_body(x_ref, g_ref, w_ref, b_ref, ws_ref, bs_ref, *rest, n_casts):
    cast_in = rest[:n_casts]
    o_ref, os_ref = rest[n_casts:n_casts + 2]
    cast_out = rest[n_casts + 2:2 * n_casts + 2]
    h_ref = rest[-1]

    @pl.when(pl.program_id(1) == 0)
    def _():
        _rms_rows_to(h_ref, 0, x_ref, g_ref, 128)
        os_ref[...] = lax.dot_general(h_ref[...], ws_ref[...], (((1,), (1,)), ((), ())),
                                      preferred_element_type=F32) + bs_ref[...]

    acc = lax.dot_general(h_ref[...], w_ref[...], (((1,), (1,)), ((), ())),
                          preferred_element_type=F32)
    o_ref[...] = (acc + b_ref[...]).astype(o_ref.dtype)

    for src, dst in zip(cast_in, cast_out):
        dst[...] = src[...].astype(dst.dtype)


def _norm_matmul(x, g, w, b, w_side, b_side, casts=(), *, tm, tn, name):
    m, d = x.shape
    n = w.shape[0]
    ns = w_side.shape[0]
    grid = (m // tm, n // tn)
    n_steps = grid[0] * grid[1]
    cast_specs = []
    for a in casts:
        chunk = next(c for c in range(16, a.shape[0] + 1, 16)
                     if a.shape[0] % c == 0 and a.shape[0] // c <= n_steps)
        last = a.shape[0] // chunk - 1
        cast_specs.append(pl.BlockSpec(
            (chunk, a.shape[1]), lambda i, j, last=last: (jnp.minimum(i * grid[1] + j, last), 0)))
    outs = pl.pallas_call(
        functools.partial(_norm_matmul_body, n_casts=len(casts)),
        grid=grid,
        in_specs=[
            pl.BlockSpec((tm, d), lambda i, j: (i, 0)),
            pl.BlockSpec((1, d), lambda i, j: (0, 0)),
            pl.BlockSpec((tn, d), lambda i, j: (j, 0)),
            pl.BlockSpec((1, tn), lambda i, j: (0, j)),
            pl.BlockSpec((ns, d), lambda i, j: (0, 0)),
            pl.BlockSpec((1, ns), lambda i, j: (0, 0)),
        ] + cast_specs,
        out_specs=[
            pl.BlockSpec((tm, tn), lambda i, j: (i, j)),
            pl.BlockSpec((tm, ns), lambda i, j: (i, 0)),
        ] + cast_specs,
        out_shape=[
            jax.ShapeDtypeStruct((m, n), BF16),
            jax.ShapeDtypeStruct((m, ns), F32),
        ] + [jax.ShapeDtypeStruct(a.shape, BF16) for a in casts],
        scratch_shapes=[pltpu.VMEM((tm, d), BF16)],
        compiler_params=_params(("arbitrary", "arbitrary")),
        name=name,
    )(x, g, w, b, w_side, b_side, *casts)
    return outs[0], outs[1], outs[2:]


def _stage_body(a_ref, b_ref, f_ref, x_ref, g_ref, bias_ref, bias_f_ref, o_ref, of_ref, z_ref, zf_ref,
                h_ref, *, first_shifted, shift):
    j = pl.program_id(0)
    nt = (((1,), (1,)), ((), ()))

    @pl.when(j == 0)
    def _():
        pad = jnp.zeros((of_ref.shape[0] - shift, of_ref.shape[1]), f_ref.dtype)
        wf = jnp.concatenate([f_ref[...], pad], axis=0).astype(of_ref.dtype)
        of_ref[...] = wf
        h_ref[...] = _rms(x_ref[...], g_ref[...]).astype(h_ref.dtype)
        zf_ref[...] = lax.dot_general(h_ref[...], wf, nt, preferred_element_type=F32) + bias_f_ref[...]

    def emit(w_tile):
        o_ref[...] = w_tile
        z = lax.dot_general(h_ref[...], w_tile, nt, preferred_element_type=F32) + bias_ref[...]
        z_ref[...] = z.astype(z_ref.dtype)

    @pl.when(j < first_shifted)
    def _():
        emit(a_ref[...].astype(o_ref.dtype))

    @pl.when(j >= first_shifted)
    def _():
        emit(jnp.concatenate([a_ref[shift:, :], b_ref[...]], axis=0).astype(o_ref.dtype))


def _stage_in_weights(wt, x_prefix, g, bias, bias_f, *, skip_start, skip, name):
    n, d = wt.shape
    npre = x_prefix.shape[0]
    tn = _pick_tile(skip_start, 1024)
    n_out = n - skip
    assert skip % 8 == 0 and n_out % tn == 0 and skip_start % tn == 0
    edge = tn // skip
    return pl.pallas_call(
        functools.partial(_stage_body, first_shifted=skip_start // tn, shift=skip),
        grid=(n_out // tn,),
        in_specs=[
            pl.BlockSpec((tn, d), lambda j: (j, 0)),
            pl.BlockSpec((skip, d), lambda j: ((j + 1) * edge, 0)),
            pl.BlockSpec((skip, d), lambda j: (skip_start // skip, 0)),
            pl.BlockSpec((npre, d), lambda j: (0, 0)),
            pl.BlockSpec((1, d), lambda j: (0, 0)),
            pl.BlockSpec((1, tn), lambda j: (0, j)),
            pl.BlockSpec((1, LANES), lambda j: (0, 0)),
        ],
        out_specs=[
            pl.BlockSpec((tn, d), lambda j: (j, 0)),
            pl.BlockSpec((LANES, d), lambda j: (0, 0)),
            pl.BlockSpec((npre, tn), lambda j: (0, j)),
            pl.BlockSpec((npre, LANES), lambda j: (0, 0)),
        ],
        out_shape=[
            jax.ShapeDtypeStruct((n_out, d), BF16),
            jax.ShapeDtypeStruct((LANES, d), BF16),
            jax.ShapeDtypeStruct((npre, n_out), BF16),
            jax.ShapeDtypeStruct((npre, LANES), F32),
        ],
        scratch_shapes=[pltpu.VMEM((npre, d), BF16)],
        compiler_params=_params(("arbitrary",)),
        name=name,
    )(wt, wt, wt, x_prefix, g, bias, bias_f)


def _cum_body(f_ref, col_ref, row_ref, *, heads):
    x = f_ref[...]
    rows = x.shape[0]
    c = jnp.minimum(x, 0.0) - jnp.log1p(jnp.exp(-jnp.abs(x)))
    t = lax.broadcasted_iota(jnp.int32, c.shape, 0)
    shift = 1
    while shift < rows:
        c = c + jnp.where(t >= shift, pltpu.roll(c, shift, axis=0), 0.0)
        shift *= 2
    col_ref[...] = c
    row_ref[...] = c.T[:heads, :]


def _forget_cumsum(flog, *, batches, heads, name):
    rows = flog.shape[0] // batches
    return pl.pallas_call(
        functools.partial(_cum_body, heads=heads),
        grid=(batches,),
        in_specs=[pl.BlockSpec((rows, flog.shape[1]), lambda b: (b, 0))],
        out_specs=[
            pl.BlockSpec((rows, flog.shape[1]), lambda b: (b, 0)),
            pl.BlockSpec((None, heads, rows), lambda b: (b, 0, 0)),
        ],
        out_shape=[
            jax.ShapeDtypeStruct(flog.shape, F32),
            jax.ShapeDtypeStruct((batches, heads, rows), F32),
        ],
        compiler_params=_params(("arbitrary",)),
        name=name,
    )(flog)


def _attn_prefix_body(q_ref, k_ref, v_ref, c_ref, o_ref, *, scale):
    n = q_ref.shape[0]
    q = (q_ref[...].astype(F32) * scale).astype(BF16)
    s = lax.dot_general(q, k_ref[...], (((1,), (1,)), ((), ())), preferred_element_type=F32)
    s = s - c_ref[...]
    row = lax.broadcasted_iota(jnp.int32, (n, n), 0)
    col = lax.broadcasted_iota(jnp.int32, (n, n), 1)
    s = jnp.where(row >= col, s, MASK_VALUE)
    p = jnp.exp(s - jnp.max(s, axis=-1, keepdims=True))
    l = jnp.sum(p, axis=-1, keepdims=True)
    acc = jnp.dot(p.astype(BF16), v_ref[...], preferred_element_type=F32)
    o_ref[...] = (acc / l).astype(o_ref.dtype)


def _attention_prefix(z, c_row, *, heads, name):
    n = z.shape[0]
    return pl.pallas_call(
        functools.partial(_attn_prefix_body, scale=HEAD_DIM ** -0.5),
        grid=(heads,),
        in_specs=[
            pl.BlockSpec((n, HEAD_DIM), lambda h: (0, h)),
            pl.BlockSpec((n, HEAD_DIM), lambda h: (0, heads + h)),
            pl.BlockSpec((n, HEAD_DIM), lambda h: (0, 2 * heads + h)),
            pl.BlockSpec((None, 1, n), lambda h: (h, 0, 0)),
        ],
        out_specs=pl.BlockSpec((n, HEAD_DIM), lambda h: (0, h)),
        out_shape=jax.ShapeDtypeStruct((n, heads * HEAD_DIM), BF16),
        compiler_params=_params(("arbitrary",)),
        name=name,
    )(z, z, z, c_row)


def _split3(x):
    p1 = x.astype(BF16)
    r1 = x - p1.astype(F32)
    p2 = r1.astype(BF16)
    p3 = (r1 - p2.astype(F32)).astype(BF16)
    return p1, p2, p3


def _bias_block(x, n_heads):
    lane = lax.broadcasted_iota(jnp.int32, x.shape, 1)
    pieces = _split3(jnp.where(lane < n_heads, x, 0.0))
    block = pieces[0].astype(F32)
    for k in range(1, N_SPLIT):
        block = block + pltpu.roll(pieces[k].astype(F32), k * n_heads, axis=1)
    return block.astype(BF16)


def _attn_main_body(q_ref, k_ref, v_ref, c_ref, kp_ref, vp_ref, cp_ref, o_ref,
                    e_ref, vt_ref, ep_ref, vpt_ref, qaug_ref, m_ref, acc_ref, alpha_ref, p_ref,
                    *, tq, hp, n_heads, scale):
    seq = q_ref.shape[0]
    npre = kp_ref.shape[0]
    group = pl.program_id(1)

    e_ref[...] = _bias_block(c_ref[...] * (-LOG2E), n_heads)
    cp = cp_ref[0:npre, :]
    ep_ref[...] = _bias_block((cp - cp[npre - 1:npre, :]) * (-LOG2E), n_heads)

    def sum_rows(n):
        r = lax.broadcasted_iota(jnp.int32, (SUM_ROWS, n), 0)
        return jnp.where(r == 0, 1.0, 0.0).astype(BF16)

    for hh in range(hp):
        lanes = slice(hh * HEAD_DIM, (hh + 1) * HEAD_DIM)
        vt_ref[hh, 0:HEAD_DIM, :] = v_ref[:, lanes].astype(F32).T.astype(BF16)
        vt_ref[hh, HEAD_DIM:, :] = sum_rows(seq)
        vp = jnp.concatenate([vp_ref[:, lanes].astype(F32),
                              jnp.zeros((LANES - npre, HEAD_DIM), F32)], axis=0)
        vpt_ref[hh, 0:HEAD_DIM, :] = vp.T.astype(BF16)
        vpt_ref[hh, HEAD_DIM:, :] = sum_rows(LANES)

    lane = lax.broadcasted_iota(jnp.int32, (tq, HEAD_DIM), 1)
    key = lax.broadcasted_iota(jnp.int32, (tq, tq), 0)
    qry = lax.broadcasted_iota(jnp.int32, (tq, tq), 1)
    causal = key <= qry
    heads = range(hp)

    def k_aug(hh, k0):
        lanes = slice(hh * HEAD_DIM, (hh + 1) * HEAD_DIM)
        return jnp.concatenate([k_ref[pl.ds(k0, tq), lanes], e_ref[pl.ds(k0, tq), :]], axis=1)

    def scores_t(kaug, hh):
        return lax.dot_general(kaug, qaug_ref[hh], (((1,), (1,)), ((), ())),
                               preferred_element_type=F32)

    def softmax_update(hh, st):
        m = m_ref[hh]
        m_new = jnp.maximum(m, jnp.max(st, axis=0, keepdims=True))
        alpha = jnp.exp2(m - m_new)
        m_ref[hh] = m_new
        return alpha, jnp.exp2((st - m_new).astype(BF16))

    def value_update(hh, k_pend):
        vt = vt_ref[hh, :, pl.ds(k_pend, tq)]
        acc_ref[hh] = alpha_ref[hh] * acc_ref[hh] + jnp.dot(vt, p_ref[hh], preferred_element_type=F32)

    def q_tile(qi, carry_unused):
        q0 = pl.multiple_of(qi * tq, tq)
        for hh in heads:
            lanes = slice(hh * HEAD_DIM, (hh + 1) * HEAD_DIM)
            head = group * hp + hh
            q = (q_ref[pl.ds(q0, tq), lanes].astype(F32) * (scale * LOG2E)).astype(BF16)
            picks = (lane == head) | (lane == n_heads + head) | (lane == 2 * n_heads + head)
            qaug_ref[hh] = jnp.concatenate([q, jnp.where(picks, 1.0, 0.0).astype(BF16)], axis=1)

        sts = []
        for hh in heads:
            lanes = slice(hh * HEAD_DIM, (hh + 1) * HEAD_DIM)
            kp_aug = jnp.concatenate([kp_ref[:, lanes], ep_ref[...]], axis=1)
            sts.append(scores_t(jnp.concatenate([kp_aug, k_aug(hh, q0)], axis=0), hh))
        for hh in heads:
            st_pre = sts[hh][0:npre, :]
            st = jnp.where(causal, sts[hh][npre:, :], MASK_VALUE)
            m = jnp.maximum(jnp.max(st, axis=0, keepdims=True), jnp.max(st_pre, axis=0, keepdims=True))
            m_ref[hh] = m
            p_ref[hh] = jnp.exp2((st - m).astype(BF16))
            alpha_ref[hh] = jnp.ones_like(m)
            p_pre = jnp.concatenate([jnp.exp2((st_pre - m).astype(BF16)),
                                     jnp.zeros((LANES - npre, tq), BF16)], axis=0)
            acc_ref[hh] = jnp.dot(vpt_ref[hh], p_pre, preferred_element_type=F32)

        def kv_tile(j, k_pend):
            k0 = pl.multiple_of(j * tq, tq)
            sts = [scores_t(k_aug(hh, k0), hh) for hh in heads]
            for hh in heads:
                value_update(hh, pl.multiple_of(k_pend, tq))
            for hh in heads:
                alpha_ref[hh], p_ref[hh] = softmax_update(hh, sts[hh])
            return k0

        k_pend = lax.fori_loop(0, qi, kv_tile, q0)

        for hh in heads:
            value_update(hh, pl.multiple_of(k_pend, tq))
        for hh in heads:
            lanes = slice(hh * HEAD_DIM, (hh + 1) * HEAD_DIM)
            acc = acc_ref[hh]
            out = acc[0:HEAD_DIM, :] / acc[HEAD_DIM:HEAD_DIM + 1, :]
            o_ref[pl.ds(q0, tq), lanes] = out.T.astype(o_ref.dtype)
        return carry_unused

    lax.fori_loop(0, seq // tq, q_tile, 0)


def _attention_main(z, c_col, z_prefix, c_col_prefix, *, batches, heads, name):
    seq = z.shape[0] // batches
    npre = z_prefix.shape[0]
    tq = _pick_tile(seq, 256)
    hp = _pick_tile(heads, 8)
    assert N_SPLIT * heads <= LANES
    groups = heads // hp
    width = hp * HEAD_DIM
    return pl.pallas_call(
        functools.partial(_attn_main_body, tq=tq, hp=hp, n_heads=heads, scale=HEAD_DIM ** -0.5),
        grid=(batches, groups),
        in_specs=[
            pl.BlockSpec((seq, width), lambda b, g: (b, g)),
            pl.BlockSpec((seq, width), lambda b, g: (b, groups + g)),
            pl.BlockSpec((seq, width), lambda b, g: (b, 2 * groups + g)),
            pl.BlockSpec((seq, LANES), lambda b, g: (b, 0)),
            pl.BlockSpec((npre, width), lambda b, g: (0, groups + g)),
            pl.BlockSpec((npre, width), lambda b, g: (0, 2 * groups + g)),
            pl.BlockSpec(c_col_prefix.shape, lambda b, g: (0, 0)),
        ],
        out_specs=pl.BlockSpec((seq, width), lambda b, g: (b, g)),
        out_shape=jax.ShapeDtypeStruct((batches * seq, heads * HEAD_DIM), BF16),
        scratch_shapes=[
            pltpu.VMEM((seq, LANES), BF16),
            pltpu.VMEM((hp, HEAD_DIM + SUM_ROWS, seq), BF16),
            pltpu.VMEM((npre, LANES), BF16),
            pltpu.VMEM((hp, HEAD_DIM + SUM_ROWS, LANES), BF16),
            pltpu.VMEM((hp, tq, 2 * HEAD_DIM), BF16),
            pltpu.VMEM((hp, 1, tq), F32),
            pltpu.VMEM((hp, HEAD_DIM + SUM_ROWS, tq), F32),
            pltpu.VMEM((hp, 1, tq), F32),
            pltpu.VMEM((hp, tq, tq), BF16),
        ],
        compiler_params=_params(("arbitrary", "arbitrary")),
        name=name,
    )(z, z, z, c_col, z_prefix, z_prefix, c_col_prefix)


def _pool_body(u_ref, *rest, gdim, pos_offset, tiles_per_batch, has_halo):
    if has_halo:
        uprev_ref, upre_ref, pw_ref, ps_ref, o_ref, ext_ref = rest
    else:
        pw_ref, ps_ref, o_ref, ext_ref = rest
    tm = u_ref.shape[0]
    if has_halo:
        first = (pl.program_id(0) % tiles_per_batch) == 0
        halo = jnp.where(first, upre_ref[...], uprev_ref[...]).astype(F32)
    else:
        halo = jnp.zeros((HALO_ROWS, u_ref.shape[1]), F32)
    ext_ref[0:HALO_ROWS, :] = halo
    ext_ref[HALO_ROWS:HALO_ROWS + tm, :] = u_ref[...].astype(F32)

    for g, w in enumerate(POOL_WINDOWS):
        cols = slice(g * gdim, (g + 1) * gdim)
        wsum = ext_ref[:, cols]
        shift = 1
        while shift < w:
            wsum = wsum + pltpu.roll(wsum, shift, axis=0)
            shift *= 2
        wsum = wsum[HALO_ROWS:, :]
        cur = ext_ref[HALO_ROWS:HALO_ROWS + tm, cols]
        if pos_offset + 1 >= w:
            mean = wsum * (1.0 / w)
        else:
            pos = lax.broadcasted_iota(jnp.int32, (tm, 1), 0) + pos_offset
            mean = wsum / jnp.minimum(pos + 1, w).astype(F32)
        d = (mean - cur).astype(BF16)
        y = jnp.dot(d, pw_ref[g], preferred_element_type=F32) * ps_ref[:, cols]
        o_ref[:, cols] = y.astype(o_ref.dtype)


def _pool(z, pool_w, pool_scale, prefix_z, *, batches, ublock, name):
    groups, gdim, _ = pool_w.shape
    dp = groups * gdim
    rows = z.shape[0]
    seq = rows // batches
    has_halo = prefix_z is not None
    tm = _pick_tile(seq, 512)
    tiles_per_batch = seq // tm
    in_specs = [pl.BlockSpec((tm, dp), lambda i: (i, ublock))]
    args = [z]
    if has_halo:
        halo_blocks = tm // HALO_ROWS
        in_specs += [
            pl.BlockSpec((HALO_ROWS, dp), lambda i: (jnp.maximum(i * halo_blocks - 1, 0), ublock)),
            pl.BlockSpec((HALO_ROWS, dp), lambda i: (prefix_z.shape[0] // HALO_ROWS - 1, ublock)),
        ]
        args += [z, prefix_z]
        pos_offset = prefix_z.shape[0]
    else:
        pos_offset = 0
    in_specs += [
        pl.BlockSpec((groups, gdim, gdim), lambda i: (0, 0, 0)),
        pl.BlockSpec((1, dp), lambda i: (0, 0)),
    ]
    args += [pool_w, pool_scale]
    return pl.pallas_call(
        functools.partial(_pool_body, gdim=gdim, pos_offset=pos_offset,
                          tiles_per_batch=tiles_per_batch, has_halo=has_halo),
        grid=(rows // tm,),
        in_specs=in_specs,
        out_specs=pl.BlockSpec((tm, dp), lambda i: (i, 0)),
        out_shape=jax.ShapeDtypeStruct((rows, dp), BF16),
        scratch_shapes=[pltpu.VMEM((tm + HALO_ROWS, dp), F32)],
        compiler_params=_params(("arbitrary",)),
        name=name,
    )(*args)


def _mix_body(a_ref, p_ref, ga_ref, gp_ref, x_ref, wa_ref, wp_ref, wo_ref, g_ref, o_ref):
    ya = jnp.dot(a_ref[...], wa_ref[...], preferred_element_type=F32)
    yp = jnp.dot(p_ref[...], wp_ref[...], preferred_element_type=F32)
    m = (jax.nn.sigmoid(ga_ref[...].astype(F32)) * ya
         + jax.nn.sigmoid(gp_ref[...].astype(F32)) * yp)
    mo = jnp.dot(m.astype(BF16), wo_ref[...], preferred_element_type=F32)
    o_ref[...] = x_ref[...] + _rms(mo, g_ref[...])


def _mix(attn, pool, z, x, w_attn_o, w_pool_o, w_out, g_post, *, gate_block, name):
    rows, d = x.shape
    da = attn.shape[1]
    dp = pool.shape[1]
    tm = _pick_tile(rows, 256)
    resident = dict(pipeline_mode=pl.Buffered(1))
    return pl.pallas_call(
        _mix_body,
        grid=(rows // tm,),
        in_specs=[
            pl.BlockSpec((tm, da), lambda i: (i, 0)),
            pl.BlockSpec((tm, dp), lambda i: (i, 0)),
            pl.BlockSpec((tm, d), lambda i: (i, gate_block)),
            pl.BlockSpec((tm, d), lambda i: (i, gate_block + 1)),
            pl.BlockSpec((tm, d), lambda i: (i, 0)),
            pl.BlockSpec((da, d), lambda i: (0, 0), **resident),
            pl.BlockSpec((dp, d), lambda i: (0, 0), **resident),
            pl.BlockSpec((d, d), lambda i: (0, 0), **resident),
            pl.BlockSpec((1, d), lambda i: (0, 0)),
        ],
        out_specs=pl.BlockSpec((tm, d), lambda i: (i, 0)),
        out_shape=jax.ShapeDtypeStruct((rows, d), F32),
        compiler_params=_params(("arbitrary",)),
        name=name,
    )(attn, pool, z, z, x, w_attn_o, w_pool_o, w_out, g_post)


def _gelu_tanh(x):
    c = math.sqrt(2.0 / math.pi)
    v = (x * x).astype(F32) * (0.044715 * c) + c
    return (0.5 * x) * (1.0 + jnp.tanh(x * v.astype(x.dtype)))


def _ffn_body(ru_ref, rprev_ref, rpre_ref, gpre_ref, wg_ref, wv_ref, cwg_ref, cwv_ref,
              cbg_ref, cbv_ref, wd_ref, gpost_ref, o_ref, h_ref, gate_ref, act0_ref, act1_ref, rkeep_ref, *, nf,
              n_steps, tiles_per_batch):
    t = pl.program_id(0)
    tu = jnp.minimum(t, n_steps - 1)
    fu = tu % nf
    iu = tu // nf
    fd = jnp.maximum(t - 1, 0) % nf
    tm = o_ref.shape[0]

    @pl.when(t == 0)
    def _():
        act1_ref[...] = jnp.zeros(act1_ref.shape, act1_ref.dtype)

    @pl.when(jnp.logical_and(fu == 0, t < n_steps))
    def _():
        first = (iu % tiles_per_batch) == 0
        halo = jnp.where(first, rpre_ref[...], rprev_ref[...])
        h_ref[0:HALO_ROWS, :] = _rms(halo, gpre_ref[...]).astype(BF16)
        _rms_rows_to(h_ref, HALO_ROWS, ru_ref, gpre_ref, 128)
        rkeep_ref[iu % 2] = ru_ref[...]

    @pl.when(fd == 0)
    def _():
        o_ref[...] = jnp.zeros(o_ref.shape, o_ref.dtype)

    def conv(a, cw_ref, cb_ref):
        a0, a1, a2 = [(a if k == 0 else pltpu.roll(a, k, axis=0))[HALO_ROWS:, :].astype(BF16)
                      for k in range(3)]
        cw = cw_ref[...].astype(BF16)
        out = cb_ref[...].astype(BF16) + a2 * cw[0:1, :]
        out = out + a1 * cw[1:2, :]
        out = out + a0 * cw[2:3, :]
        return out

    def step(act_u_ref, act_d_ref):
        gate_ref[...] = jnp.dot(h_ref[...], wg_ref[...], preferred_element_type=F32)
        val = conv(jnp.dot(h_ref[...], wv_ref[...], preferred_element_type=F32), cwv_ref, cbv_ref)
        act_u_ref[...] = _gelu_tanh(conv(gate_ref[...], cwg_ref, cbg_ref)) * val
        o_ref[...] += jnp.dot(act_d_ref[...], wd_ref[...], preferred_element_type=F32)

    @pl.when(t % 2 == 0)
    def _():
        step(act0_ref, act1_ref)

    @pl.when(t % 2 == 1)
    def _():
        step(act1_ref, act0_ref)

    @pl.when(jnp.logical_and(fd == nf - 1, t > 0))
    def _():
        chunk = min(128, tm)
        slot = ((t - 1) // nf) % 2

        def body(c, carry):
            r0 = pl.multiple_of(c * chunk, chunk)
            rows = pl.ds(r0, chunk)
            o_ref[rows, :] = rkeep_ref[slot, rows, :] + _rms(o_ref[rows, :], gpost_ref[...])
            return carry

        lax.fori_loop(0, tm // chunk, body, 0)


def _conv_ffn(r, r_prefix, g_pre, w_up, conv_w, conv_b, w_down, g_post, *, batches, name):
    rows, d = r.shape
    dff = w_down.shape[0]
    seq = rows // batches
    tm = _pick_tile(seq, 512)
    tf = _pick_tile(dff, 512)
    nf = dff // tf
    assert nf >= 2
    n_steps = (rows // tm) * nf
    tiles_per_batch = seq // tm
    halo_blocks = tm // HALO_ROWS
    kw = conv_w.shape[0]

    def up_item(t):
        tu = jnp.minimum(t, n_steps - 1)
        return tu // nf, tu % nf

    def down_item(t):
        td = jnp.maximum(t - 1, 0)
        return td // nf, td % nf

    return pl.pallas_call(
        functools.partial(_ffn_body, nf=nf, n_steps=n_steps, tiles_per_batch=tiles_per_batch),
        grid=(n_steps + 1,),
        in_specs=[
            pl.BlockSpec((tm, d), lambda t: (up_item(t)[0], 0)),
            pl.BlockSpec((HALO_ROWS, d), lambda t: (jnp.maximum(up_item(t)[0] * halo_blocks - 1, 0), 0)),
            pl.BlockSpec((HALO_ROWS, d), lambda t: (r_prefix.shape[0] // HALO_ROWS - 1, 0)),
            pl.BlockSpec((1, d), lambda t: (0, 0)),
            pl.BlockSpec((d, tf), lambda t: (0, up_item(t)[1])),
            pl.BlockSpec((d, tf), lambda t: (0, nf + up_item(t)[1])),
            pl.BlockSpec((kw, tf), lambda t: (0, up_item(t)[1])),
            pl.BlockSpec((kw, tf), lambda t: (0, nf + up_item(t)[1])),
            pl.BlockSpec((1, tf), lambda t: (0, up_item(t)[1])),
            pl.BlockSpec((1, tf), lambda t: (0, nf + up_item(t)[1])),
            pl.BlockSpec((tf, d), lambda t: (down_item(t)[1], 0)),
            pl.BlockSpec((1, d), lambda t: (0, 0)),
        ],
        out_specs=pl.BlockSpec((tm, d), lambda t: (down_item(t)[0], 0)),
        out_shape=jax.ShapeDtypeStruct((rows, d), F32),
        scratch_shapes=[
            pltpu.VMEM((HALO_ROWS + tm, d), BF16),
            pltpu.VMEM((HALO_ROWS + tm, tf), F32),
            pltpu.VMEM((tm, tf), BF16),
            pltpu.VMEM((tm, tf), BF16),
            pltpu.VMEM((2, tm, d), F32),
        ],
        compiler_params=_params(("arbitrary",)),
        name=name,
    )(r, r, r_prefix, g_pre, w_up, w_up, conv_w, conv_w, conv_b, conv_b, w_down, g_post)


def kernel(x, meta_tokens, mix_pre_g, w_in, b_in, w_attn_o, pool_w, pool_scale, w_pool_o, w_out,
           mix_post_g, ffn_pre_g, w_ffn_up, ffn_conv_w, ffn_conv_b, w_ffn_down, ffn_post_g):
    batches, seq, d = x.shape
    n_meta = meta_tokens.shape[0]
    assert w_in.shape[0] == 1, "single-layer block"
    assert n_meta == HALO_ROWS and n_meta >= max(POOL_WINDOWS) and ffn_conv_w.shape[1] <= HALO_ROWS
    d_attn = w_attn_o.shape[1]
    d_pool = w_pool_o.shape[1]
    heads = d_attn // HEAD_DIM
    assert d_attn == d_pool and 2 * d_attn == d
    assert w_in.shape[2] == 3 * d_attn + heads + d_pool + 2 * d

    f0 = 3 * d_attn
    w_t = w_in[0].T
    b = b_in[0]
    b_main = jnp.concatenate([b[:f0], b[f0 + heads:]])[None, :]
    b_f = jnp.pad(b[f0:f0 + heads], (0, LANES - heads))[None, :]
    meta_rows = meta_tokens.astype(x.dtype)
    w_main, w_f, z_m, flog_m = _stage_in_weights(w_t, meta_rows, mix_pre_g, b_main, b_f,
                                                  skip_start=f0, skip=heads, name="stage_w_in")
    pw = pool_w[0].astype(BF16)
    later_weights = (w_attn_o[0], w_pool_o[0], w_out[0], w_ffn_up[0], w_ffn_down[0])
    ublock = f0 // d_pool
    gate_block = (f0 + d_pool) // d

    def in_project(rows_in, nb, tag, casts=()):
        s = rows_in.shape[0] // nb
        return _norm_matmul(rows_in, mix_pre_g, w_main, b_main, w_f, b_f, casts,
                            tm=_pick_tile(s, 1024), tn=_pick_tile(w_main.shape[0], 1024),
                            name=f"inproj_{tag}")

    def mixer(rows_in, projected, prefix, nb, tag):
        n, _ = rows_in.shape
        z, flog = projected
        if prefix is None:
            flog = jnp.pad(flog, ((0, LANES - n), (0, 0)))
            c_col, c_row = _forget_cumsum(flog, batches=nb, heads=heads, name=f"cumsum_{tag}")
            attn = _attention_prefix(z, c_row[0, :, :n].reshape(heads, 1, n), heads=heads,
                                     name=f"attention_{tag}")
        else:
            c_col, _ = _forget_cumsum(flog, batches=nb, heads=heads, name=f"cumsum_{tag}")
            attn = _attention_main(z, c_col, prefix[0], prefix[1], batches=nb, heads=heads,
                                   name=f"attention_{tag}")
        pool = _pool(z, pw, pool_scale, None if prefix is None else prefix[0],
                     batches=nb, ublock=ublock, name=f"pool_{tag}")
        r = _mix(attn, pool, z, rows_in, wa, wp, wo, mix_post_g,
                 gate_block=gate_block, name=f"mix_{tag}")
        return r, z, c_col

    x_rows = x.reshape(batches * seq, d)
    z_x, flog_x, (wa, wp, wo, w_up, w_down) = in_project(x_rows, batches, "main", later_weights)
    r_meta, z_meta, c_meta = mixer(meta_rows, (z_m, flog_m), None, 1, "prefix")
    r, _, _ = mixer(x_rows, (z_x, flog_x), (z_meta, c_meta), batches, "main")
    out = _conv_ffn(r, r_meta, ffn_pre_g, w_up, ffn_conv_w[0], ffn_conv_b, w_down, ffn_post_g,
                    batches=batches, name="conv_ffn")
    return out.reshape(batches, seq, d)
```
